```python
import jax, jax.numpy as jnp
from jax import lax
import numpy as np

D_MODEL = 2048
BATCH = 2
SEQ = 8192
DEPTH = 1

HEAD_DIM = 128
NSA_HEADS = 8
NSA_KV = 2
NSA_WIDTH = NSA_HEADS * HEAD_DIM
DSA_HEADS = 8
DSA_KV = 2
DSA_WIDTH = DSA_HEADS * HEAD_DIM
CMP_LEN = 32
CMP_STRIDE = 16
CMP_HIDDEN = 256
SEL_BLOCK = 64
SEL_TOPN = 16
WINDOW = 512
IDX_HEADS = 16
IDX_DIM = 64
DSA_TOPK_MAX = 256
PLE_DIM = 256
Q_BLOCK = 128
EPS = 1e-6
NEG = -1e30
FORCE_SCORE = 1e4

IN_SPLITS = (
    NSA_HEADS * HEAD_DIM,
    NSA_KV * HEAD_DIM,
    NSA_KV * HEAD_DIM,
    NSA_KV * HEAD_DIM,
    NSA_KV * HEAD_DIM,
    NSA_KV * HEAD_DIM,
    NSA_KV * HEAD_DIM,
    NSA_HEADS * 3,
    NSA_WIDTH,
    DSA_HEADS * HEAD_DIM,
    DSA_KV * HEAD_DIM,
    DSA_KV * HEAD_DIM,
    IDX_HEADS * IDX_DIM,
    IDX_DIM,
    IDX_HEADS,
    DSA_WIDTH,
    D_MODEL,
    D_MODEL,
)
IN_WIDTH = sum(IN_SPLITS)

kernel_name = "hybrid_nsa_dsa_gated_merge"


def _rms(x, g):
    xf = x.astype(jnp.float32)
    y = xf * lax.rsqrt(jnp.mean(xf * xf, axis=-1, keepdims=True) + EPS)
    return (y * g.astype(jnp.float32)).astype(x.dtype)


def _slopes(n):
    return 2.0 ** (-8.0 * jnp.arange(1, n + 1, dtype=jnp.float32) / n)


def _masked_softmax(s, mask):
    return jax.nn.softmax(jnp.where(mask, s, NEG), axis=-1)


def _compress(kv, pe, w1, w2):
    B, S, G, dh = kv.shape
    n_cmp = (S - CMP_LEN) // CMP_STRIDE + 1
    idx = jnp.arange(n_cmp)[:, None] * CMP_STRIDE + jnp.arange(CMP_LEN)[None, :]
    blk = kv[:, idx] + pe[None, None, :, None, :].astype(kv.dtype)
    blk = jnp.moveaxis(blk, 3, 2).reshape(B, n_cmp, G, CMP_LEN * dh)
    return jax.nn.silu(blk @ w1) @ w2


def _cmp_to_sel(n_cmp, n_sel):
    c0 = np.arange(n_cmp)[:, None] * CMP_STRIDE
    s0 = np.arange(n_sel)[None, :] * SEL_BLOCK
    ov = np.clip(np.minimum(c0 + CMP_LEN, s0 + SEL_BLOCK) - np.maximum(c0, s0), 0, None)
    return jnp.asarray(ov / CMP_LEN, dtype=jnp.float32)


def _nsa_mixer(q, kc, vc, ks, vs, kw, vw, gates):
    B, S, H, dh = q.shape
    G = ks.shape[2]
    Hg = H // G
    scale = dh ** -0.5
    slopes = _slopes(H).reshape(G, Hg)
    t = jnp.arange(S)
    qg = q.reshape(B, S, G, Hg, dh)

    n_cmp = kc.shape[1]
    c_end = jnp.arange(n_cmp) * CMP_STRIDE + CMP_LEN - 1
    c_mid = c_end.astype(jnp.float32) - (CMP_LEN - 1) / 2.0
    vis = c_end[None, :] <= t[:, None]
    s = jnp.einsum('btgjd,bngd->bgjtn', qg, kc).astype(jnp.float32) * scale
    s = s - slopes[:, :, None, None] * (t[:, None].astype(jnp.float32) - c_mid[None, :])
    p_cmp = _masked_softmax(s, vis) * (t >= CMP_LEN - 1)[:, None].astype(jnp.float32)
    o_cmp = jnp.einsum('bgjtn,bngd->btgjd', p_cmp.astype(vc.dtype), vc).reshape(B, S, H, dh)

    n_sel = S // SEL_BLOCK
    top_n = min(SEL_TOPN, n_sel)
    imp = jnp.einsum('bgtn,nm->btgm', p_cmp.sum(2), _cmp_to_sel(n_cmp, n_sel))
    m = jnp.arange(n_sel)[None, :]
    blk_t = (t // SEL_BLOCK)[:, None]
    valid = (m <= blk_t)[:, None, :]
    forced = ((m == 0) | (m == blk_t) | (m == blk_t - 1))[:, None, :]
    imp = jnp.where(valid & forced, FORCE_SCORE, jnp.where(valid, imp, -1.0))
    _, sel = lax.top_k(imp, top_n)

    ks_blk = jnp.moveaxis(ks.reshape(B, n_sel, SEL_BLOCK, G, dh), 3, 1).reshape(B, G, n_sel, SEL_BLOCK * dh)
    vs_blk = jnp.moveaxis(vs.reshape(B, n_sel, SEL_BLOCK, G, dh), 3, 1).reshape(B, G, n_sel, SEL_BLOCK * dh)
    kw_pad = jnp.pad(kw, ((0, 0), (WINDOW, 0), (0, 0), (0, 0)))
    vw_pad = jnp.pad(vw, ((0, 0), (WINDOW, 0), (0, 0), (0, 0)))
    n_keys = top_n * SEL_BLOCK

    def chunk(c):
        t0 = c * Q_BLOCK
        tq = t0 + jnp.arange(Q_BLOCK)
        qc = lax.dynamic_slice_in_dim(qg, t0, Q_BLOCK, axis=1)
        sc = jnp.moveaxis(lax.dynamic_slice_in_dim(sel, t0, Q_BLOCK, axis=1), 2, 1)
        idx = sc.reshape(B, G, Q_BLOCK * top_n, 1)
        kg = jnp.take_along_axis(ks_blk, idx, axis=2).reshape(B, G, Q_BLOCK, n_keys, dh)
        vg = jnp.take_along_axis(vs_blk, idx, axis=2).reshape(B, G, Q_BLOCK, n_keys, dh)
        kpos = (sc[..., None] * SEL_BLOCK + jnp.arange(SEL_BLOCK)).reshape(B, G, Q_BLOCK, n_keys)
        dist = (tq[:, None] - kpos).astype(jnp.float32)
        s1 = jnp.einsum('bqgjd,bgqkd->bgjqk', qc, kg).astype(jnp.float32) * scale
        s1 = s1 - slopes[None, :, :, None, None] * dist[:, :, None]
        p1 = _masked_softmax(s1, (dist >= 0)[:, :, None])
        o_slc = jnp.einsum('bgjqk,bgqkd->bqgjd', p1.astype(vg.dtype), vg)
        kwc = lax.dynamic_slice_in_dim(kw_pad, t0, WINDOW + Q_BLOCK, axis=1)
        vwc = lax.dynamic_slice_in_dim(vw_pad, t0, WINDOW + Q_BLOCK, axis=1)
        kpos_w = t0 - WINDOW + jnp.arange(WINDOW + Q_BLOCK)
        dw = tq[:, None] - kpos_w[None, :]
        mw = (dw >= 0) & (dw < WINDOW) & (kpos_w[None, :] >= 0)
        s2 = jnp.einsum('bqgjd,bkgd->bgjqk', qc, kwc).astype(jnp.float32) * scale
        s2 = s2 - slopes[:, :, None, None] * dw.astype(jnp.float32)
        p2 = _masked_softmax(s2, mw)
        o_win = jnp.einsum('bgjqk,bkgd->bqgjd', p2.astype(vwc.dtype), vwc)
        return o_slc, o_win

    o_slc, o_win = lax.map(chunk, jnp.arange(S // Q_BLOCK))
    o_slc = jnp.moveaxis(o_slc, 0, 1).reshape(B, S, H, dh)
    o_win = jnp.moveaxis(o_win, 0, 1).reshape(B, S, H, dh)
    g = jax.nn.sigmoid(gates)
    o = g[..., 0:1] * o_cmp + g[..., 1:2] * o_slc + g[..., 2:3] * o_win
    return o.reshape(B, S, H * dh)


def _dsa_mixer(q, k, v, qi, ki, wi):
    B, S, H, dh = q.shape
    G = k.shape[2]
    Hg = H // G
    scale = dh ** -0.5
    slopes = _slopes(H).reshape(G, Hg)
    topk = min(DSA_TOPK_MAX, S // 4)
    qg = q.reshape(B, S, G, Hg, dh)
    wi = wi.astype(jnp.float32) * IDX_HEADS ** -0.5
    s_all = jnp.arange(S)

    def chunk(c):
        t0 = c * Q_BLOCK
        tq = t0 + jnp.arange(Q_BLOCK)
        qic = lax.dynamic_slice_in_dim(qi, t0, Q_BLOCK, axis=1)
        wic = lax.dynamic_slice_in_dim(wi, t0, Q_BLOCK, axis=1)
        qc = lax.dynamic_slice_in_dim(qg, t0, Q_BLOCK, axis=1)
        rel = jax.nn.relu(jnp.einsum('bqhd,bsd->bqhs', qic, ki).astype(jnp.float32) * IDX_DIM ** -0.5)
        score = jnp.einsum('bqhs,bqh->bqs', rel, wic)
        score = jnp.where(s_all[None, :] <= tq[:, None], score, NEG)
        _, idx = lax.top_k(score, topk)
        kg = jax.vmap(lambda a, i: a[i])(k, idx)
        vg = jax.vmap(lambda a, i: a[i])(v, idx)
        dist = (tq[None, :, None] - idx).astype(jnp.float32)
        s = jnp.einsum('bqgjd,bqkgd->bgjqk', qc, kg).astype(jnp.float32) * scale
        s = s - slopes[:, :, None, None] * dist[:, None, None]
        pr = _masked_softmax(s, (dist >= 0)[:, None, None])
        return jnp.einsum('bgjqk,bqkgd->bqgjd', pr.astype(vg.dtype), vg)

    o = lax.map(chunk, jnp.arange(S // Q_BLOCK))
    return jnp.moveaxis(o, 0, 1).reshape(B, S, H * dh)


def setup_inputs(seed: int = 0) -> dict:
    key = jax.random.key(seed)
    ks = jax.random.split(key, 24)
    L = DEPTH

    def nrm(k, shape, scale):
        return jax.random.normal(k, shape, jnp.float32) * scale

    def gain(k, shape):
        return 1.0 + 0.01 * jax.random.normal(k, shape, jnp.float32)

    return {
        "x": nrm(ks[0], (BATCH, SEQ, D_MODEL), 1.0),
        "p": nrm(ks[1], (DEPTH, BATCH, SEQ, PLE_DIM), 1.0),
        "norm_g": gain(ks[2], (L, D_MODEL)),
        "w_in": nrm(ks[3], (L, D_MODEL, IN_WIDTH), D_MODEL ** -0.5),
        "nsa_q_g": gain(ks[4], (L, HEAD_DIM)),
        "nsa_kc_g": gain(ks[5], (L, HEAD_DIM)),
        "nsa_ks_g": gain(ks[6], (L, HEAD_DIM)),
        "nsa_kw_g": gain(ks[7], (L, HEAD_DIM)),
        "cmp_pe_k": nrm(ks[8], (L, CMP_LEN, HEAD_DIM), 0.1),
        "cmp_w1_k": nrm(ks[9], (L, CMP_LEN * HEAD_DIM, CMP_HIDDEN), (CMP_LEN * HEAD_DIM) ** -0.5),
        "cmp_w2_k": nrm(ks[10], (L, CMP_HIDDEN, HEAD_DIM), CMP_HIDDEN ** -0.5),
        "cmp_pe_v": nrm(ks[11], (L, CMP_LEN, HEAD_DIM), 0.1),
        "cmp_w1_v": nrm(ks[12], (L, CMP_LEN * HEAD_DIM, CMP_HIDDEN), (CMP_LEN * HEAD_DIM) ** -0.5),
        "cmp_w2_v": nrm(ks[13], (L, CMP_HIDDEN, HEAD_DIM), CMP_HIDDEN ** -0.5),
        "dsa_q_g": gain(ks[14], (L, HEAD_DIM)),
        "dsa_k_g": gain(ks[15], (L, HEAD_DIM)),
        "w_up_nsa": nrm(ks[16], (L, NSA_WIDTH, D_MODEL), NSA_WIDTH ** -0.5),
        "w_up_dsa": nrm(ks[17], (L, DSA_WIDTH, D_MODEL), DSA_WIDTH ** -0.5),
        "w_out": nrm(ks[18], (L, D_MODEL, D_MODEL), D_MODEL ** -0.5),
        "ple_norm_g": gain(ks[19], (L, D_MODEL)),
        "w_ple_gate": nrm(ks[20], (L, D_MODEL, D_MODEL), D_MODEL ** -0.5),
        "w_ple_proj": nrm(ks[21], (L, PLE_DIM, D_MODEL), PLE_DIM ** -0.5),
    }


def reference(x, p, norm_g, w_in, nsa_q_g, nsa_kc_g, nsa_ks_g, nsa_kw_g,
              cmp_pe_k, cmp_w1_k, cmp_w2_k, cmp_pe_v, cmp_w1_v, cmp_w2_v,
              dsa_q_g, dsa_k_g, w_up_nsa, w_up_dsa, w_out,
              ple_norm_g, w_ple_gate, w_ple_proj):
    B, S, _ = x.shape
    split_points = [int(v) for v in np.cumsum(IN_SPLITS)[:-1]]

    def heads(a):
        return a.reshape(B, S, -1, HEAD_DIM)

    for i in range(DEPTH):
        h = _rms(x, norm_g[i])
        proj = h @ w_in[i]
        (nq, nkc, nvc, nks, nvs, nkw, nvw, ng, nz,
         dq, dk, dv, iq, ik, iw, dz, mg_nsa, mg_dsa) = jnp.split(proj, split_points, axis=-1)

        q_n = _rms(heads(nq), nsa_q_g[i])
        kc = _rms(_compress(heads(nkc), cmp_pe_k[i], cmp_w1_k[i], cmp_w2_k[i]), nsa_kc_g[i])
        vc = _compress(heads(nvc), cmp_pe_v[i], cmp_w1_v[i], cmp_w2_v[i])
        o_nsa = _nsa_mixer(q_n, kc, vc,
                           _rms(heads(nks), nsa_ks_g[i]), heads(nvs),
                           _rms(heads(nkw), nsa_kw_g[i]), heads(nvw),
                           ng.reshape(B, S, NSA_HEADS, 3))
        o_nsa = o_nsa * jax.nn.silu(nz)

        o_dsa = _dsa_mixer(_rms(heads(dq), dsa_q_g[i]), _rms(heads(dk), dsa_k_g[i]), heads(dv),
                           iq.reshape(B, S, IDX_HEADS, IDX_DIM), ik, iw)
        o_dsa = o_dsa * jax.nn.silu(dz)

        y = jax.nn.sigmoid(mg_nsa) * (o_nsa @ w_up_nsa[i]) + jax.nn.sigmoid(mg_dsa) * (o_dsa @ w_up_dsa[i])
        x = x + y @ w_out[i]

        x = x + (p[i] @ w_ple_proj[i]) * jax.nn.sigmoid(_rms(x, ple_norm_g[i]) @ w_ple_gate[i])
    return x
```

```python
import functools

import numpy as np
import jax
import jax.numpy as jnp
from jax import lax
from jax.experimental import pallas as pl
from jax.experimental.pallas import tpu as pltpu

D_MODEL = 2048
HEAD_DIM = 128
NSA_HEADS = 8
NSA_KV = 2
DSA_HEADS = 8
DSA_KV = 2
HEADS_PER_GROUP = 4
CMP_LEN = 32
CMP_STRIDE = 16
CMP_HIDDEN = 256
SEL_BLOCK = 64
SEL_TOPN = 16
WINDOW = 512
IDX_HEADS = 16
IDX_DIM = 64
DSA_TOPK_MAX = 256
PLE_DIM = 256
EPS = 1e-6
NEG = -1e30
FORCE_SCORE = 1e4
ATTN_SCALE = HEAD_DIM ** -0.5
LANES = 128
INT_MIN = -(2 ** 31)

VMEM_LIMIT_BYTES = 56 * 1024 * 1024

F32 = jnp.float32
BF16 = jnp.bfloat16


def _params(semantics):
    return pltpu.CompilerParams(dimension_semantics=semantics, vmem_limit_bytes=VMEM_LIMIT_BYTES)


def _sigmoid(x):
    return 1.0 / (1.0 + jnp.exp(-x))


def _dot(a, b):
    return jnp.dot(a, b, preferred_element_type=F32)


def _dot_nt(a, b):
    return lax.dot_general(a, b, (((1,), (1,)), ((), ())), preferred_element_type=F32)


def _log2(n):
    assert n > 0 and n & (n - 1) == 0, n
    return n.bit_length() - 1


def _div_pow2(x, n):
    return lax.shift_right_logical(x, jnp.int32(_log2(n)))


def _group_slope_scale(g):
    return jnp.where(g == 0, 1.0, 2.0 ** -HEADS_PER_GROUP).astype(F32)


def _rmsnorm_kernel(x_ref, g_ref, o_ref):
    x = x_ref[...]
    ms = jnp.mean(x * x, axis=-1, keepdims=True)
    o_ref[...] = (x * lax.rsqrt(ms + EPS) * g_ref[...]).astype(o_ref.dtype)


def _rmsnorm(x, g, tm):
    t, d = x.shape
    return pl.pallas_call(
        _rmsnorm_kernel,
        grid=(t // tm,),
        in_specs=[pl.BlockSpec((tm, d), lambda i: (i, 0)), pl.BlockSpec((1, d), lambda i: (0, 0))],
        out_specs=pl.BlockSpec((tm, d), lambda i: (i, 0)),
        out_shape=jax.ShapeDtypeStruct((t, d), BF16),
        compiler_params=_params(("parallel",)),
        name="in_rmsnorm",
    )(x, g.reshape(1, d))


def _proj_headnorm_kernel(h_ref, w_ref, g_ref, o_ref):
    y = _dot(h_ref[...], w_ref[...])
    for hd in range(y.shape[1] // HEAD_DIM):
        sl = slice(hd * HEAD_DIM, (hd + 1) * HEAD_DIM)
        yh = y[:, sl]
        ms = jnp.mean(yh * yh, axis=-1, keepdims=True)
        o_ref[:, sl] = (yh * lax.rsqrt(ms + EPS) * g_ref[:, sl]).astype(o_ref.dtype)


def _proj_act_kernel(h_ref, w_ref, o_ref, *, act):
    y = _dot(h_ref[...], w_ref[...])
    if act == "silu":
        y = y * _sigmoid(y)
    elif act == "sigmoid":
        y = _sigmoid(y)
    o_ref[...] = y.astype(o_ref.dtype)


def _proj_split_kernel(h_ref, w_ref, *o_refs):
    y = _dot(h_ref[...], w_ref[...])
    off = 0
    for o_ref in o_refs:
        n = o_ref.shape[1]
        o_ref[...] = y[:, off:off + n].astype(o_ref.dtype)
        off += n


def _proj(h, w, tm, tn, out_dtype, act=None, gain=None):
    t, d = h.shape
    n = w.shape[1]
    in_specs = [pl.BlockSpec((tm, d), lambda i, j: (i, 0)), pl.BlockSpec((d, tn), lambda i, j: (0, j))]
    args = [h, w]
    if gain is not None:
        body = _proj_headnorm_kernel
        in_specs.append(pl.BlockSpec((1, tn), lambda i, j: (0, j)))
        args.append(gain.reshape(1, n))
    else:
        body = functools.partial(_proj_act_kernel, act=act)
    return pl.pallas_call(
        body,
        grid=(t // tm, n // tn),
        in_specs=in_specs,
        out_specs=pl.BlockSpec((tm, tn), lambda i, j: (i, j)),
        out_shape=jax.ShapeDtypeStruct((t, n), out_dtype),
        compiler_params=_params(("parallel", "arbitrary")),
        name="in_proj_" + ("headnorm" if gain is not None else str(act)),
    )(*args)


def _proj_split(h, w, tm, widths):
    t, d = h.shape
    n = w.shape[1]
    return pl.pallas_call(
        _proj_split_kernel,
        grid=(t // tm,),
        in_specs=[pl.BlockSpec((tm, d), lambda i: (i, 0)), pl.BlockSpec((d, n), lambda i: (0, 0))],
        out_specs=[pl.BlockSpec((tm, wd), lambda i: (i, 0)) for wd in widths],
        out_shape=[jax.ShapeDtypeStruct((t, wd), F32) for wd in widths],
        compiler_params=_params(("parallel",)),
        name="in_proj_split",
    )(h, w)


def _compress_kernel(x_ref, pea_ref, peb_ref, wa_ref, wb_ref, w2_ref, g_ref, o_ref, *, normalize):
    x = x_ref[...]
    nch = x.shape[0]
    ha = _dot((x + pea_ref[...]).astype(BF16), wa_ref[...])
    hb = _dot((x + peb_ref[...]).astype(BF16), wb_ref[...])
    hid = ha + pltpu.roll(hb, nch - 1, axis=0)
    act = (hid * _sigmoid(hid)).astype(BF16)
    for g in range(NSA_KV):
        o = _dot(act[:, g * CMP_HIDDEN:(g + 1) * CMP_HIDDEN], w2_ref[...])
        if normalize:
            ms = jnp.mean(o * o, axis=-1, keepdims=True)
            o = o * lax.rsqrt(ms + EPS) * g_ref[...]
        o_ref[0, :, g * HEAD_DIM:(g + 1) * HEAD_DIM] = o.astype(o_ref.dtype)


def _expand_compress_params(pe, w1):
    half = CMP_LEN // 2
    w = w1.reshape(CMP_LEN, HEAD_DIM, CMP_HIDDEN)
    eye = jnp.eye(NSA_KV, dtype=w1.dtype)

    def expand_w(wh):
        return jnp.einsum("ldj,hg->lhdgj", wh, eye).reshape(half * NSA_KV * HEAD_DIM, NSA_KV * CMP_HIDDEN)

    def expand_pe(ph):
        return jnp.broadcast_to(ph[:, None, :], (half, NSA_KV, HEAD_DIM)).reshape(1, half * NSA_KV * HEAD_DIM)

    return (expand_pe(pe[:half]), expand_pe(pe[half:]),
            expand_w(w[:half]).astype(BF16), expand_w(w[half:]).astype(BF16))


def _compress(raw, pe, w1, w2, gain, batch, seq):
    nch = seq // CMP_STRIDE
    width = CMP_STRIDE * NSA_KV * HEAD_DIM
    x = raw.reshape(batch * nch, width)
    pea, peb, wa, wb = _expand_compress_params(pe, w1)
    normalize = gain is not None
    g = (gain if normalize else jnp.ones((HEAD_DIM,), F32)).reshape(1, HEAD_DIM)
    const = lambda b: (0, 0)
    return pl.pallas_call(
        functools.partial(_compress_kernel, normalize=normalize),
        grid=(batch,),
        in_specs=[pl.BlockSpec((nch, width), lambda b: (b, 0)),
                  pl.BlockSpec((1, width), const), pl.BlockSpec((1, width), const),
                  pl.BlockSpec(wa.shape, const), pl.BlockSpec(wb.shape, const),
                  pl.BlockSpec((CMP_HIDDEN, HEAD_DIM), const), pl.BlockSpec((1, HEAD_DIM), const)],
        out_specs=pl.BlockSpec((1, nch, NSA_KV * HEAD_DIM), lambda b: (b, 0, 0)),
        out_shape=jax.ShapeDtypeStruct((batch, nch, NSA_KV * HEAD_DIM), BF16),
        compiler_params=_params(("parallel",)),
        name="nsa_compress",
    )(x, pea, peb, wa, wb, w2.astype(BF16), g)


def _cmp_to_sel_matrix(nch, n_sel_pad):
    n_cmp = nch - 1
    c0 = np.arange(nch)[:, None] * CMP_STRIDE
    s0 = np.arange(n_sel_pad)[None, :] * SEL_BLOCK
    ov = np.clip(np.minimum(c0 + CMP_LEN, s0 + SEL_BLOCK) - np.maximum(c0, s0), 0, None) / CMP_LEN
    ov[n_cmp:] = 0.0
    return jnp.asarray(ov, dtype=BF16)


def _nsa_cmp_kernel(q_ref, kc_ref, vc_ref, m_ref, ocmp_ref, selb_ref, *, tq, nch, n_sel):
    g = pl.program_id(1)
    i = pl.program_id(2)
    t = i * tq + lax.broadcasted_iota(jnp.int32, (tq, 1), 0)
    c_end = lax.broadcasted_iota(jnp.int32, (1, nch), 1) * CMP_STRIDE + (CMP_LEN - 1)
    vis = c_end <= t
    distc = t.astype(F32) - (c_end.astype(F32) - (CMP_LEN - 1) / 2.0)
    live = (t >= CMP_LEN - 1).astype(F32)
    kc = kc_ref[0]
    vc = vc_ref[0]
    gsc = _group_slope_scale(g)
    psum = jnp.zeros((tq, nch), F32)
    for j in range(HEADS_PER_GROUP):
        slope = (2.0 ** -(j + 1)) * gsc
        sl = slice(j * HEAD_DIM, (j + 1) * HEAD_DIM)
        s = _dot_nt(q_ref[:, sl], kc) * ATTN_SCALE - slope * distc
        s = jnp.where(vis, s, NEG)
        e = jnp.exp(s - jnp.max(s, axis=-1, keepdims=True))
        p = e * (live / jnp.sum(e, axis=-1, keepdims=True))
        psum = psum + p
        ocmp_ref[:, sl] = _dot(p.astype(BF16), vc)

    imp = _dot(psum.astype(BF16), m_ref[...])
    lane = lax.broadcasted_iota(jnp.int32, (tq, LANES), 1)
    blk_t = _div_pow2(t, SEL_BLOCK)
    valid = lane <= blk_t
    forced = (lane == 0) | (lane == blk_t) | (lane == blk_t - 1)
    vals = jnp.where(valid & forced, FORCE_SCORE, jnp.where(valid, imp, -1.0))
    vals = jnp.where(lane < n_sel, vals, -2.0)
    sel = jnp.zeros((tq, LANES), jnp.bool_)
    for _ in range(SEL_TOPN):
        mx = jnp.max(vals, axis=-1, keepdims=True)
        first = jnp.min(jnp.where(vals == mx, lane, LANES), axis=-1, keepdims=True)
        pick = lane == first
        sel = sel | pick
        vals = jnp.where(pick, -3.0, vals)
    selb_ref[...] = jnp.where(sel & valid, 0.0, NEG).astype(selb_ref.dtype)


def _nsa_cmp(qkn, kc, vc, batch, seq, tq):
    t = batch * seq
    nch = seq // CMP_STRIDE
    n_sel = seq // SEL_BLOCK
    assert n_sel <= LANES
    nq = seq // tq
    gw = HEADS_PER_GROUP * HEAD_DIM
    m = _cmp_to_sel_matrix(nch, LANES)
    row = lambda b, g, i: (b * nq + i, g)
    return pl.pallas_call(
        functools.partial(_nsa_cmp_kernel, tq=tq, nch=nch, n_sel=n_sel),
        grid=(batch, NSA_KV, nq),
        in_specs=[pl.BlockSpec((tq, gw), row),
                  pl.BlockSpec((1, nch, HEAD_DIM), lambda b, g, i: (b, 0, g)),
                  pl.BlockSpec((1, nch, HEAD_DIM), lambda b, g, i: (b, 0, g)),
                  pl.BlockSpec((nch, LANES), lambda b, g, i: (0, 0))],
        out_specs=[pl.BlockSpec((tq, gw), row), pl.BlockSpec((tq, LANES), row)],
        out_shape=[jax.ShapeDtypeStruct((t, NSA_HEADS * HEAD_DIM), F32),
                   jax.ShapeDtypeStruct((t, NSA_KV * LANES), BF16)],
        compiler_params=_params(("parallel", "parallel", "parallel")),
        name="nsa_cmp_select",
    )(qkn, kc, vc, m)


def _stack_rows(t0, tq, gsc):
    rows = HEADS_PER_GROUP * tq
    r = lax.broadcasted_iota(jnp.int32, (rows, 1), 0)
    t_rows = t0 + (r & (tq - 1))
    j_rows = _div_pow2(r, tq)
    slope = jnp.zeros((rows, 1), F32)
    for j in range(HEADS_PER_GROUP):
        slope = jnp.where(j_rows == j, 2.0 ** -(j + 1), slope)
    return t_rows, slope * gsc


def _online_update(s, v, m, l, acc):
    m_new = jnp.maximum(m, jnp.max(s, axis=-1, keepdims=True))
    alpha = jnp.exp(m - m_new)
    p = jnp.exp(s - m_new)
    l = alpha * l + jnp.sum(p, axis=-1, keepdims=True)
    acc = alpha * acc + _dot(p.astype(BF16), v)
    return m_new, l, acc


def _nsa_main_kernel(q_ref, ks_ref, vs_ref, kw_ref, vw_ref, selb_ref, ocmp_ref, gate_ref, z_ref, o_ref,
                     *, tq, tk, seq):
    g = pl.program_id(1)
    i = pl.program_id(2)
    t0 = i * tq
    rows = HEADS_PER_GROUP * tq
    gsc = _group_slope_scale(g)
    t_rows, slope_rows = _stack_rows(t0, tq, gsc)
    selb = selb_ref[...]
    heads = [q_ref[:, j * HEAD_DIM:(j + 1) * HEAD_DIM] for j in range(HEADS_PER_GROUP)]
    q_plain = jnp.concatenate(heads, axis=0)
    q_aug = jnp.concatenate([jnp.concatenate([qh, selb], axis=1) for qh in heads], axis=0)

    def sel_step(kt, carry):
        j0 = pl.multiple_of(kt * tk, tk)
        k = ks_ref[pl.ds(j0, tk), :]
        v = vs_ref[pl.ds(j0, tk), :]
        blk = _div_pow2(j0 + lax.broadcasted_iota(jnp.int32, (tk, LANES), 0), SEL_BLOCK)
        onehot = jnp.where(blk == lax.broadcasted_iota(jnp.int32, (tk, LANES), 1), 1.0, 0.0).astype(BF16)
        s = _dot_nt(q_aug, jnp.concatenate([k, onehot], axis=1)) * ATTN_SCALE
        dist = t_rows - (j0 + lax.broadcasted_iota(jnp.int32, (1, tk), 1))
        s = s - slope_rows * dist.astype(F32)
        s = jnp.where(dist >= 0, s, NEG)
        return _online_update(s, v, *carry)

    n_kt = _div_pow2(t0 + tq + tk - 1, tk)
    init = (jnp.full((rows, 1), NEG, F32), jnp.zeros((rows, 1), F32), jnp.zeros((rows, HEAD_DIM), F32))
    _, l_s, acc_s = lax.fori_loop(0, n_kt, sel_step, init)
    o_slc = acc_s / l_s

    wk = WINDOW + tq
    start = pl.multiple_of(jnp.maximum(t0 - WINDOW, 0), tq)
    kwin = kw_ref[pl.ds(start, wk), :]
    vwin = vw_ref[pl.ds(start, wk), :]
    dw = t_rows - (start + lax.broadcasted_iota(jnp.int32, (1, wk), 1))
    s = _dot_nt(q_plain, kwin) * ATTN_SCALE - slope_rows * dw.astype(F32)
    s = jnp.where((dw >= 0) & (dw < WINDOW), s, NEG)
    e = jnp.exp(s - jnp.max(s, axis=-1, keepdims=True))
    o_win = _dot(e.astype(BF16), vwin) / jnp.sum(e, axis=-1, keepdims=True)

    for j in range(HEADS_PER_GROUP):
        sl = slice(j * HEAD_DIM, (j + 1) * HEAD_DIM)
        rs = slice(j * tq, (j + 1) * tq)
        gates = [_sigmoid(gate_ref[:, 3 * j + c:3 * j + c + 1]) for c in range(3)]
        o = gates[0] * ocmp_ref[:, sl] + gates[1] * o_slc[rs] + gates[2] * o_win[rs]
        o_ref[:, sl] = (o * z_ref[:, sl]).astype(o_ref.dtype)


def _nsa_main(qkn, vals, selb, ocmp, gates, zact, batch, seq, tq, tk):
    t = batch * seq
    nq = seq // tq
    gw = HEADS_PER_GROUP * HEAD_DIM
    row = lambda b, g, i: (b * nq + i, g)
    ks_blk = (NSA_HEADS * HEAD_DIM) // HEAD_DIM
    kw_blk = ks_blk + NSA_KV
    return pl.pallas_call(
        functools.partial(_nsa_main_kernel, tq=tq, tk=tk, seq=seq),
        grid=(batch, NSA_KV, nq),
        in_specs=[pl.BlockSpec((tq, gw), row),
                  pl.BlockSpec((seq, HEAD_DIM), lambda b, g, i: (b, ks_blk + g)),
                  pl.BlockSpec((seq, HEAD_DIM), lambda b, g, i: (b, g)),
                  pl.BlockSpec((seq, HEAD_DIM), lambda b, g, i: (b, kw_blk + g)),
                  pl.BlockSpec((seq, HEAD_DIM), lambda b, g, i: (b, NSA_KV + g)),
                  pl.BlockSpec((tq, LANES), row),
                  pl.BlockSpec((tq, gw), row),
                  pl.BlockSpec((tq, LANES), row),
                  pl.BlockSpec((tq, gw), row)],
        out_specs=pl.BlockSpec((tq, gw), row),
        out_shape=jax.ShapeDtypeStruct((t, NSA_HEADS * HEAD_DIM), BF16),
        compiler_params=_params(("parallel", "parallel", "arbitrary")),
        name="nsa_select_window",
    )(qkn, qkn, vals, qkn, vals, selb, ocmp, gates, zact)


def _sortable_key(x):
    bits = pltpu.bitcast(x, jnp.int32)
    bits = jnp.where(bits == INT_MIN, 0, bits)
    return jnp.where(bits < 0, bits ^ 0x7FFFFFFF, bits)


def _dsa_index_kernel(qi_ref, ki_ref, wi_ref, tri_ref, bias_ref, key_scr, *, tq, tk, seq, topk):
    i = pl.program_id(1)
    t0 = i * tq
    n_all = seq // tk
    n_kt = _div_pow2(t0 + tq + tk - 1, tk)
    t = t0 + lax.broadcasted_iota(jnp.int32, (tq, 1), 0)
    wcol = [wi_ref[:, h:h + 1] * (IDX_HEADS ** -0.5 * IDX_DIM ** -0.5) for h in range(IDX_HEADS)]
    hb = 4

    def col_ids(kt):
        return kt * tk + lax.broadcasted_iota(jnp.int32, (1, tk), 1)

    def score_step(kt, _):
        j0 = pl.multiple_of(kt * tk, tk)
        ki = ki_ref[pl.ds(j0, tk), :]
        score = jnp.zeros((tq, tk), F32)
        for h0 in range(0, IDX_HEADS, hb):
            x = _dot_nt(qi_ref[0, h0:h0 + hb].reshape(hb * tq, IDX_DIM), ki)
            for h in range(hb):
                score = score + jnp.maximum(x[h * tq:(h + 1) * tq], 0.0) * wcol[h0 + h]
        key_scr[kt] = jnp.where(col_ids(kt) <= t, _sortable_key(score), INT_MIN)
        return 0

    lax.fori_loop(0, n_kt, score_step, 0)

    def count(pred):
        def step(kt, c):
            key = key_scr[kt]
            for lc in range(tk // LANES):
                c = c + jnp.where(pred(key[:, lc * LANES:(lc + 1) * LANES]), 1, 0)
            return c
        c = lax.fori_loop(0, n_kt, step, jnp.zeros((tq, LANES), jnp.int32))
        return jnp.sum(c, axis=-1, keepdims=True)

    v = jnp.where(count(lambda k: k >= 0) >= topk, 0, INT_MIN)

    def bit_step(b, v):
        cand = v | (jnp.int32(1) << (30 - b))
        return jnp.where(count(lambda k: k >= cand) >= topk, cand, v)

    v = lax.fori_loop(0, 31, bit_step, v)
    need = (topk - count(lambda k: k > v)).astype(F32)

    def out_step(kt, carry):
        key = key_scr[kt]
        eq = key == v
        eqf = jnp.where(eq, 1.0, 0.0)
        before = carry + _dot(eqf.astype(BF16), tri_ref[...]) - eqf
        keep = ((key > v) | (eq & (before < need))) & (col_ids(kt) <= t)
        bias_ref[0, 0, kt] = jnp.where(keep, 0.0, NEG).astype(bias_ref.dtype)
        return carry + jnp.sum(eqf, axis=-1, keepdims=True)

    lax.fori_loop(0, n_kt, out_step, jnp.zeros((tq, 1), F32))

    def fill_step(kt, _):
        bias_ref[0, 0, kt] = jnp.full((tq, tk), NEG, bias_ref.dtype)
        return 0

    lax.fori_loop(n_kt, n_all, fill_step, 0)


def _dsa_index(qi_heads, ki, wi, batch, seq, tq, tk):
    nq = seq // tq
    n_all = seq // tk
    topk = min(DSA_TOPK_MAX, seq // 4)
    tri = jnp.asarray(np.triu(np.ones((tk, tk), np.float32)), dtype=BF16)
    return pl.pallas_call(
        functools.partial(_dsa_index_kernel, tq=tq, tk=tk, seq=seq, topk=topk),
        grid=(batch, nq),
        in_specs=[pl.BlockSpec((1, IDX_HEADS, tq, IDX_DIM), lambda b, i: (b, 0, i, 0)),
                  pl.BlockSpec((seq, IDX_DIM), lambda b, i: (b, 0)),
                  pl.BlockSpec((tq, IDX_HEADS), lambda b, i: (b * nq + i, 0)),
                  pl.BlockSpec((tk, tk), lambda b, i: (0, 0))],
        out_specs=pl.BlockSpec((1, 1, n_all, tq, tk), lambda b, i: (b, i, 0, 0, 0)),
        out_shape=jax.ShapeDtypeStruct((batch, nq, n_all, tq, tk), BF16),
        scratch_shapes=[pltpu.VMEM((n_all, tq, tk), jnp.int32)],
        compiler_params=_params(("parallel", "arbitrary")),
        name="dsa_index_topk",
    )(qi_heads, ki, wi, tri)


def _dsa_attn_kernel(q_ref, k_ref, v_ref, bias_ref, z_ref, o_ref, *, tq, tk):
    g = pl.program_id(1)
    i = pl.program_id(2)
    t0 = i * tq
    rows = HEADS_PER_GROUP * tq
    t_rows, slope_rows = _stack_rows(t0, tq, _group_slope_scale(g))
    q = jnp.concatenate([q_ref[:, j * HEAD_DIM:(j + 1) * HEAD_DIM] for j in range(HEADS_PER_GROUP)], axis=0)

    def step(kt, carry):
        j0 = pl.multiple_of(kt * tk, tk)
        k = k_ref[pl.ds(j0, tk), :]
        v = v_ref[pl.ds(j0, tk), :]
        bias = bias_ref[0, 0, kt].astype(F32)
        dist = t_rows - (j0 + lax.broadcasted_iota(jnp.int32, (1, tk), 1))
        s = _dot_nt(q, k) * ATTN_SCALE - slope_rows * dist.astype(F32)
        s = s + jnp.concatenate([bias] * HEADS_PER_GROUP, axis=0)
        return _online_update(s, v, *carry)

    n_kt = _div_pow2(t0 + tq + tk - 1, tk)
    init = (jnp.full((rows, 1), NEG, F32), jnp.zeros((rows, 1), F32), jnp.zeros((rows, HEAD_DIM), F32))
    _, l, acc = lax.fori_loop(0, n_kt, step, init)
    o = acc / l
    for j in range(HEADS_PER_GROUP):
        sl = slice(j * HEAD_DIM, (j + 1) * HEAD_DIM)
        o_ref[:, sl] = (o[j * tq:(j + 1) * tq] * z_ref[:, sl]).astype(o_ref.dtype)


def _dsa_attn(qkn, vals, bias, zact, batch, seq, tq, tk):
    t = batch * seq
    nq = seq // tq
    n_all = seq // tk
    gw = HEADS_PER_GROUP * HEAD_DIM
    row = lambda b, g, i: (b * nq + i, g)
    q_blk = (NSA_HEADS + 2 * NSA_KV) * HEAD_DIM // gw
    k_blk = (NSA_HEADS + 2 * NSA_KV + DSA_HEADS)
    return pl.pallas_call(
        functools.partial(_dsa_attn_kernel, tq=tq, tk=tk),
        grid=(batch, DSA_KV, nq),
        in_specs=[pl.BlockSpec((tq, gw), lambda b, g, i: (b * nq + i, q_blk + g)),
                  pl.BlockSpec((seq, HEAD_DIM), lambda b, g, i: (b, k_blk + g)),
                  pl.BlockSpec((seq, HEAD_DIM), lambda b, g, i: (b, 2 * NSA_KV + g)),
                  pl.BlockSpec((1, 1, n_all, tq, tk), lambda b, g, i: (b, i, 0, 0, 0)),
                  pl.BlockSpec((tq, gw), lambda b, g, i: (b * nq + i, NSA_KV + g))],
        out_specs=pl.BlockSpec((tq, gw), row),
        out_shape=jax.ShapeDtypeStruct((t, DSA_HEADS * HEAD_DIM), BF16),
        compiler_params=_params(("parallel", "parallel", "arbitrary")),
        name="dsa_attention",
    )(qkn, qkn, vals, bias, zact)


def _merge_kernel(on_ref, od_ref, sn_ref, sd_ref, x_ref, wun_ref, wud_ref, wout_ref, o_ref):
    y = sn_ref[...] * _dot(on_ref[...], wun_ref[...]) + sd_ref[...] * _dot(od_ref[...], wud_ref[...])
    o_ref[...] = x_ref[...] + _dot(y.astype(BF16), wout_ref[...])


def _merge(o_nsa, o_dsa, gate_act, x, w_up_nsa, w_up_dsa, w_out, tm):
    t, d = x.shape
    const = lambda i: (0, 0)
    return pl.pallas_call(
        _merge_kernel,
        grid=(t // tm,),
        in_specs=[pl.BlockSpec((tm, o_nsa.shape[1]), lambda i: (i, 0)),
                  pl.BlockSpec((tm, o_dsa.shape[1]), lambda i: (i, 0)),
                  pl.BlockSpec((tm, d), lambda i: (i, 0)),
                  pl.BlockSpec((tm, d), lambda i: (i, 1)),
                  pl.BlockSpec((tm, d), lambda i: (i, 0)),
                  pl.BlockSpec(w_up_nsa.shape, const), pl.BlockSpec(w_up_dsa.shape, const),
                  pl.BlockSpec(w_out.shape, const)],
        out_specs=pl.BlockSpec((tm, d), lambda i: (i, 0)),
        out_shape=jax.ShapeDtypeStruct((t, d), F32),
        compiler_params=_params(("parallel",)),
        name="merge_out_proj",
    )(o_nsa, o_dsa, gate_act, gate_act, x, w_up_nsa, w_up_dsa, w_out)


def _ple_kernel(x_ref, p_ref, g_ref, wg_ref, wp_ref, o_ref):
    x = x_ref[...]
    ms = jnp.mean(x * x, axis=-1, keepdims=True)
    r = (x * lax.rsqrt(ms + EPS) * g_ref[...]).astype(BF16)
    gate = _sigmoid(_dot(r, wg_ref[...]))
    o_ref[...] = x + _dot(p_ref[...].astype(BF16), wp_ref[...]) * gate


def _ple(x, p, g, w_gate, w_proj, tm):
    t, d = x.shape
    const = lambda i: (0, 0)
    return pl.pallas_call(
        _ple_kernel,
        grid=(t // tm,),
        in_specs=[pl.BlockSpec((tm, d), lambda i: (i, 0)),
                  pl.BlockSpec((tm, p.shape[1]), lambda i: (i, 0)),
                  pl.BlockSpec((1, d), const),
                  pl.BlockSpec(w_gate.shape, const), pl.BlockSpec(w_proj.shape, const)],
        out_specs=pl.BlockSpec((tm, d), lambda i: (i, 0)),
        out_shape=jax.ShapeDtypeStruct((t, d), F32),
        compiler_params=_params(("parallel",)),
        name="ple_gate",
    )(x, p, g.reshape(1, d), w_gate, w_proj)


def _split_in_proj(w):
    widths = [NSA_HEADS * HEAD_DIM] + [NSA_KV * HEAD_DIM] * 6 + [NSA_HEADS * 3, NSA_HEADS * HEAD_DIM,
              DSA_HEADS * HEAD_DIM, DSA_KV * HEAD_DIM, DSA_KV * HEAD_DIM, IDX_HEADS * IDX_DIM, IDX_DIM,
              IDX_HEADS, DSA_HEADS * HEAD_DIM, D_MODEL, D_MODEL]
    offs = np.concatenate([[0], np.cumsum(widths)])
    assert offs[-1] == w.shape[1]
    (nq, nkc, nvc, nks, nvs, nkw, nvw, ng, nz, dq, dk, dv, iq, ik, iw, dz, mgn, mgd) = [
        w[:, offs[k]:offs[k + 1]] for k in range(len(widths))]
    pad = jnp.zeros((w.shape[0], LANES - IDX_DIM - IDX_HEADS - NSA_HEADS * 3), w.dtype)
    return dict(
        normed=jnp.concatenate([nq, nks, nkw, dq, dk], axis=1).astype(BF16),
        vals=jnp.concatenate([nvs, nvw, dv, iq], axis=1).astype(BF16),
        raw=jnp.concatenate([nkc, nvc, ik, iw, ng, pad], axis=1).astype(BF16),
        zact=jnp.concatenate([nz, dz], axis=1).astype(BF16),
        merge=jnp.concatenate([mgn, mgd], axis=1).astype(BF16),
    )


def _layer(x, p, norm_g, w_in, nsa_q_g, nsa_kc_g, nsa_ks_g, nsa_kw_g, cmp_pe_k, cmp_w1_k, cmp_w2_k,
           cmp_pe_v, cmp_w1_v, cmp_w2_v, dsa_q_g, dsa_k_g, w_up_nsa, w_up_dsa, w_out, ple_norm_g,
           w_ple_gate, w_ple_proj):
    batch, seq, d = x.shape
    t = batch * seq
    assert seq % LANES == 0 and seq >= WINDOW + LANES
    x2 = x.reshape(t, d)
    tm = min(1024, t)

    w = _split_in_proj(w_in)
    h = _rmsnorm(x2, norm_g, min(512, t))

    gains = jnp.concatenate([jnp.tile(nsa_q_g, NSA_HEADS), jnp.tile(nsa_ks_g, NSA_KV), jnp.tile(nsa_kw_g, NSA_KV),
                             jnp.tile(dsa_q_g, DSA_HEADS), jnp.tile(dsa_k_g, DSA_KV)])
    qkn = _proj(h, w["normed"], tm, w["normed"].shape[1] // 2, BF16, gain=gains)
    vals = _proj(h, w["vals"], tm, w["vals"].shape[1] // 2, BF16)
    kc_raw, vc_raw, misc = _proj_split(h, w["raw"], tm, (NSA_KV * HEAD_DIM, NSA_KV * HEAD_DIM, LANES))
    zact = _proj(h, w["zact"], tm, 1024, F32, act="silu")
    gate_act = _proj(h, w["merge"], tm, 1024, F32, act="sigmoid")

    ki = misc[:, :IDX_DIM].astype(BF16)
    wi = misc[:, IDX_DIM:IDX_DIM + IDX_HEADS]
    ng = misc[:, IDX_DIM + IDX_HEADS:IDX_DIM + IDX_HEADS + NSA_HEADS * 3]
    per_group = HEADS_PER_GROUP * 3
    gates = jnp.pad(ng.reshape(t, NSA_KV, per_group), ((0, 0), (0, 0), (0, LANES - per_group))).reshape(t, NSA_KV * LANES)
    iq_off = 3 * NSA_KV * HEAD_DIM
    qi_heads = vals[:, iq_off:].reshape(batch, seq, IDX_HEADS, IDX_DIM).transpose(0, 2, 1, 3)

    kc = _compress(kc_raw, cmp_pe_k, cmp_w1_k, cmp_w2_k, nsa_kc_g, batch, seq)
    vc = _compress(vc_raw, cmp_pe_v, cmp_w1_v, cmp_w2_v, None, batch, seq)

    tq = 128
    tk = min(512, seq)
    ocmp, selb = _nsa_cmp(qkn, kc, vc, batch, seq, min(256, seq))
    o_nsa = _nsa_main(qkn, vals, selb, ocmp, gates, zact, batch, seq, tq, tk)

    bias = _dsa_index(qi_heads, ki, wi, batch, seq, tq, tk)
    o_dsa = _dsa_attn(qkn, vals, bias, zact, batch, seq, tq, tk)

    x1 = _merge(o_nsa, o_dsa, gate_act, x2, w_up_nsa.astype(BF16), w_up_dsa.astype(BF16), w_out.astype(BF16),
                min(256, t))
    x3 = _ple(x1, p.reshape(t, PLE_DIM), ple_norm_g, w_ple_gate.astype(BF16), w_ple_proj.astype(BF16), min(256, t))
    return x3.reshape(batch, seq, d)


def kernel(x, p, norm_g, w_in, nsa_q_g, nsa_kc_g, nsa_ks_g, nsa_kw_g, cmp_pe_k, cmp_w1_k, cmp_w2_k, cmp_pe_v, cmp_w1_v, cmp_w2_v, dsa_q_g, dsa_k_g, w_up_nsa, w_up_dsa, w_out, ple_norm_g, w_ple_gate, w_ple_proj):
    depth = w_in.shape[0]
    for i in range(depth):
        x = _layer(x, p[i], norm_g[i], w_in[i], nsa_q_g[i], nsa_kc_g[i], nsa_ks_g[i], nsa_kw_g[i],
                   cmp_pe_k[i], cmp_w1_k[i], cmp_w2_k[i], cmp_pe_v[i], cmp_w1_v[i], cmp_w2_v[i],
                   dsa_q_g[i], dsa_k_g[i], w_up_nsa[i], w_up_dsa[i], w_out[i], ple_norm_g[i],
                   w_ple_gate[i], w_ple_proj[i])
    return x
```

```python
import functools

import numpy as np
import jax
import jax.numpy as jnp
from jax import lax
from jax.experimental import pallas as pl
from jax.experimental.pallas import tpu as pltpu

D_MODEL = 2048
HEAD_DIM = 128
NSA_HEADS = 8
NSA_KV = 2
DSA_HEADS = 8
DSA_KV = 2
HEADS_PER_GROUP = 4
CMP_LEN = 32
CMP_STRIDE = 16
CMP_HIDDEN = 256
SEL_BLOCK = 64
SEL_TOPN = 16
WINDOW = 512
IDX_HEADS = 16
IDX_DIM = 64
DSA_TOPK_MAX = 256
PLE_DIM = 256
EPS = 1e-6
NEG = -1e30
FORCE_SCORE = 1e4
ATTN_SCALE = HEAD_DIM ** -0.5
LANES = 128
INT_MIN = -(2 ** 31)

VMEM_LIMIT_BYTES = 56 * 1024 * 1024

F32 = jnp.float32
BF16 = jnp.bfloat16


def _params(semantics):
    return pltpu.CompilerParams(dimension_semantics=semantics, vmem_limit_bytes=VMEM_LIMIT_BYTES)


def _sigmoid(x):
    return 1.0 / (1.0 + jnp.exp(-x))


def _dot(a, b):
    return jnp.dot(a, b, preferred_element_type=F32)


def _dot_nt(a, b):
    return lax.dot_general(a, b, (((1,), (1,)), ((), ())), preferred_element_type=F32)


def _log2(n):
    assert n > 0 and n & (n - 1) == 0, n
    return n.bit_length() - 1


def _div_pow2(x, n):
    return lax.shift_right_logical(x, jnp.int32(_log2(n)))


def _group_slope_scale(g):
    return jnp.where(g == 0, 1.0, 2.0 ** -HEADS_PER_GROUP).astype(F32)


def _rmsnorm_kernel(x_ref, g_ref, o_ref):
    x = x_ref[...]
    ms = jnp.mean(x * x, axis=-1, keepdims=True)
    o_ref[...] = (x * lax.rsqrt(ms + EPS) * g_ref[...]).astype(o_ref.dtype)


def _rmsnorm(x, g, tm):
    t, d = x.shape
    return pl.pallas_call(
        _rmsnorm_kernel,
        grid=(t // tm,),
        in_specs=[pl.BlockSpec((tm, d), lambda i: (i, 0)), pl.BlockSpec((1, d), lambda i: (0, 0))],
        out_specs=pl.BlockSpec((tm, d), lambda i: (i, 0)),
        out_shape=jax.ShapeDtypeStruct((t, d), BF16),
        compiler_params=_params(("parallel",)),
        name="in_rmsnorm",
    )(x, g.reshape(1, d))


def _proj_headnorm_kernel(h_ref, w_ref, g_ref, o_ref):
    y = _dot(h_ref[...], w_ref[...])
    for hd in range(y.shape[1] // HEAD_DIM):
        sl = slice(hd * HEAD_DIM, (hd + 1) * HEAD_DIM)
        yh = y[:, sl]
        ms = jnp.mean(yh * yh, axis=-1, keepdims=True)
        o_ref[:, sl] = (yh * lax.rsqrt(ms + EPS) * g_ref[:, sl]).astype(o_ref.dtype)


def _proj_act_kernel(h_ref, w_ref, o_ref, *, act):
    y = _dot(h_ref[...], w_ref[...])
    if act == "silu":
        y = y * _sigmoid(y)
    elif act == "sigmoid":
        y = _sigmoid(y)
    o_ref[...] = y.astype(o_ref.dtype)


def _proj_split_kernel(h_ref, w_ref, *o_refs):
    y = _dot(h_ref[...], w_ref[...])
    off = 0
    for o_ref in o_refs:
        n = o_ref.shape[1]
        o_ref[...] = y[:, off:off + n].astype(o_ref.dtype)
        off += n


def _proj(h, w, tm, tn, out_dtype, act=None, gain=None):
    t, d = h.shape
    n = w.shape[1]
    in_specs = [pl.BlockSpec((tm, d), lambda i, j: (i, 0)), pl.BlockSpec((d, tn), lambda i, j: (0, j))]
    args = [h, w]
    if gain is not None:
        body = _proj_headnorm_kernel
        in_specs.append(pl.BlockSpec((1, tn), lambda i, j: (0, j)))
        args.append(gain.reshape(1, n))
    else:
        body = functools.partial(_proj_act_kernel, act=act)
    return pl.pallas_call(
        body,
        grid=(t // tm, n // tn),
        in_specs=in_specs,
        out_specs=pl.BlockSpec((tm, tn), lambda i, j: (i, j)),
        out_shape=jax.ShapeDtypeStruct((t, n), out_dtype),
        compiler_params=_params(("parallel", "arbitrary")),
        name="in_proj_" + ("headnorm" if gain is not None else str(act)),
    )(*args)


def _proj_split(h, w, tm, widths):
    t, d = h.shape
    n = w.shape[1]
    return pl.pallas_call(
        _proj_split_kernel,
        grid=(t // tm,),
        in_specs=[pl.BlockSpec((tm, d), lambda i: (i, 0)), pl.BlockSpec((d, n), lambda i: (0, 0))],
        out_specs=[pl.BlockSpec((tm, wd), lambda i: (i, 0)) for wd in widths],
        out_shape=[jax.ShapeDtypeStruct((t, wd), F32) for wd in widths],
        compiler_params=_params(("parallel",)),
        name="in_proj_split",
    )(h, w)


def _compress_kernel(x_ref, pea_ref, peb_ref, wa_ref, wb_ref, w2_ref, g_ref, o_ref, *, normalize):
    x = x_ref[...]
    nch = x.shape[0]
    ha = _dot((x + pea_ref[...]).astype(BF16), wa_ref[...])
    hb = _dot((x + peb_ref[...]).astype(BF16), wb_ref[...])
    hid = ha + pltpu.roll(hb, nch - 1, axis=0)
    act = (hid * _sigmoid(hid)).astype(BF16)
    for g in range(NSA_KV):
        o = _dot(act[:, g * CMP_HIDDEN:(g + 1) * CMP_HIDDEN], w2_ref[...])
        if normalize:
            ms = jnp.mean(o * o, axis=-1, keepdims=True)
            o = o * lax.rsqrt(ms + EPS) * g_ref[...]
        o_ref[0, :, g * HEAD_DIM:(g + 1) * HEAD_DIM] = o.astype(o_ref.dtype)


def _expand_compress_params(pe, w1):
    half = CMP_LEN // 2
    w = w1.reshape(CMP_LEN, HEAD_DIM, CMP_HIDDEN)
    eye = jnp.eye(NSA_KV, dtype=w1.dtype)

    def expand_w(wh):
        return jnp.einsum("ldj,hg->lhdgj", wh, eye).reshape(half * NSA_KV * HEAD_DIM, NSA_KV * CMP_HIDDEN)

    def expand_pe(ph):
        return jnp.broadcast_to(ph[:, None, :], (half, NSA_KV, HEAD_DIM)).reshape(1, half * NSA_KV * HEAD_DIM)

    return (expand_pe(pe[:half]), expand_pe(pe[half:]),
            expand_w(w[:half]).astype(BF16), expand_w(w[half:]).astype(BF16))


def _compress(raw, pe, w1, w2, gain, batch, seq):
    nch = seq // CMP_STRIDE
    width = CMP_STRIDE * NSA_KV * HEAD_DIM
    x = raw.reshape(batch * nch, width)
    pea, peb, wa, wb = _expand_compress_params(pe, w1)
    normalize = gain is not None
    g = (gain if normalize else jnp.ones((HEAD_DIM,), F32)).reshape(1, HEAD_DIM)
    const = lambda b: (0, 0)
    return pl.pallas_call(
        functools.partial(_compress_kernel, normalize=normalize),
        grid=(batch,),
        in_specs=[pl.BlockSpec((nch, width), lambda b: (b, 0)),
                  pl.BlockSpec((1, width), const), pl.BlockSpec((1, width), const),
                  pl.BlockSpec(wa.shape, const), pl.BlockSpec(wb.shape, const),
                  pl.BlockSpec((CMP_HIDDEN, HEAD_DIM), const), pl.BlockSpec((1, HEAD_DIM), const)],
        out_specs=pl.BlockSpec((1, nch, NSA_KV * HEAD_DIM), lambda b: (b, 0, 0)),
        out_shape=jax.ShapeDtypeStruct((batch, nch, NSA_KV * HEAD_DIM), BF16),
        compiler_params=_params(("parallel",)),
        name="nsa_compress",
    )(x, pea, peb, wa, wb, w2.astype(BF16), g)


def _cmp_to_sel_matrix(nch, n_sel_pad):
    n_cmp = nch - 1
    c0 = np.arange(nch)[:, None] * CMP_STRIDE
    s0 = np.arange(n_sel_pad)[None, :] * SEL_BLOCK
    ov = np.clip(np.minimum(c0 + CMP_LEN, s0 + SEL_BLOCK) - np.maximum(c0, s0), 0, None) / CMP_LEN
    ov[n_cmp:] = 0.0
    return jnp.asarray(ov, dtype=BF16)


def _nsa_cmp_kernel(q_ref, kc_ref, vc_ref, m_ref, ocmp_ref, selb_ref, *, tq, nch, n_sel):
    g = pl.program_id(1)
    i = pl.program_id(2)
    t = i * tq + lax.broadcasted_iota(jnp.int32, (tq, 1), 0)
    c_end = lax.broadcasted_iota(jnp.int32, (1, nch), 1) * CMP_STRIDE + (CMP_LEN - 1)
    vis = c_end <= t
    distc = t.astype(F32) - (c_end.astype(F32) - (CMP_LEN - 1) / 2.0)
    live = (t >= CMP_LEN - 1).astype(F32)
    kc = kc_ref[0]
    vc = vc_ref[0]
    gsc = _group_slope_scale(g)
    psum = jnp.zeros((tq, nch), F32)
    for j in range(HEADS_PER_GROUP):
        slope = (2.0 ** -(j + 1)) * gsc
        sl = slice(j * HEAD_DIM, (j + 1) * HEAD_DIM)
        s = _dot_nt(q_ref[:, sl], kc) - slope * distc
        s = jnp.where(vis, s, NEG)
        e = jnp.exp(s - jnp.max(s, axis=-1, keepdims=True))
        p = e * (live / jnp.sum(e, axis=-1, keepdims=True))
        psum = psum + p
        ocmp_ref[:, sl] = _dot(p.astype(BF16), vc)

    imp = _dot(psum.astype(BF16), m_ref[...])
    lane = lax.broadcasted_iota(jnp.int32, (tq, LANES), 1)
    blk_t = _div_pow2(t, SEL_BLOCK)
    valid = lane <= blk_t
    forced = (lane == 0) | (lane == blk_t) | (lane == blk_t - 1)
    vals = jnp.where(valid & forced, FORCE_SCORE, jnp.where(valid, imp, -1.0))
    vals = jnp.where(lane < n_sel, vals, -2.0)
    sel = jnp.zeros((tq, LANES), jnp.bool_)
    for _ in range(SEL_TOPN):
        mx = jnp.max(vals, axis=-1, keepdims=True)
        first = jnp.min(jnp.where(vals == mx, lane, LANES), axis=-1, keepdims=True)
        pick = lane == first
        sel = sel | pick
        vals = jnp.where(pick, -3.0, vals)
    selb_ref[...] = jnp.where(sel & valid, 0.0, NEG).astype(selb_ref.dtype)


def _nsa_cmp(qkn, kc, vc, batch, seq, tq):
    t = batch * seq
    nch = seq // CMP_STRIDE
    n_sel = seq // SEL_BLOCK
    assert n_sel <= LANES
    nq = seq // tq
    gw = HEADS_PER_GROUP * HEAD_DIM
    m = _cmp_to_sel_matrix(nch, LANES)
    row = lambda b, g, i: (b * nq + i, g)
    return pl.pallas_call(
        functools.partial(_nsa_cmp_kernel, tq=tq, nch=nch, n_sel=n_sel),
        grid=(batch, NSA_KV, nq),
        in_specs=[pl.BlockSpec((tq, gw), row),
                  pl.BlockSpec((1, nch, HEAD_DIM), lambda b, g, i: (b, 0, g)),
                  pl.BlockSpec((1, nch, HEAD_DIM), lambda b, g, i: (b, 0, g)),
                  pl.BlockSpec((nch, LANES), lambda b, g, i: (0, 0))],
        out_specs=[pl.BlockSpec((tq, gw), row), pl.BlockSpec((tq, LANES), row)],
        out_shape=[jax.ShapeDtypeStruct((t, NSA_HEADS * HEAD_DIM), F32),
                   jax.ShapeDtypeStruct((t, NSA_KV * LANES), BF16)],
        compiler_params=_params(("parallel", "parallel", "parallel")),
        name="nsa_cmp_select",
    )(qkn, kc, vc, m)


def _stack_rows(t0, tq, gsc):
    rows = HEADS_PER_GROUP * tq
    r = lax.broadcasted_iota(jnp.int32, (rows, 1), 0)
    t_rows = t0 + (r & (tq - 1))
    j_rows = _div_pow2(r, tq)
    slope = jnp.zeros((rows, 1), F32)
    for j in range(HEADS_PER_GROUP):
        slope = jnp.where(j_rows == j, 2.0 ** -(j + 1), slope)
    return t_rows, slope * gsc


def _key_aux(seq, tk):
    assert tk <= 2 * 256
    j = np.arange(seq)
    aux = (j[:, None] // SEL_BLOCK == np.arange(LANES)[None, :]).astype(np.float32)
    aux[:, 0] = (j % tk) - tk // 2
    return jnp.asarray(aux, dtype=BF16)


def _query_aug(q_ref, slopes, mask_cols):
    lane0 = lax.broadcasted_iota(jnp.int32, mask_cols.shape, 1) == 0
    parts = []
    for j in range(HEADS_PER_GROUP):
        aux = jnp.where(lane0, slopes[j], mask_cols).astype(BF16)
        parts.append(jnp.concatenate([q_ref[:, j * HEAD_DIM:(j + 1) * HEAD_DIM], aux], axis=1))
    return jnp.concatenate(parts, axis=0)


def _tile_origin_bias(slope_rows, t_rows, j0, tk):
    return slope_rows * (j0 + tk // 2 - t_rows).astype(F32)


def _flash_step(s, v, off, carry):
    m, l, acc = carry
    m_new = jnp.maximum(m, jnp.max(s, axis=-1, keepdims=True) + off)
    alpha = jnp.exp(m - m_new)
    p = jnp.exp(s - (m_new - off)).astype(BF16)
    pv = _dot(p, jnp.concatenate([v, jnp.ones_like(v)], axis=1))
    l = alpha * l + pv[:, HEAD_DIM:HEAD_DIM + 1]
    acc = alpha * acc + pv[:, :HEAD_DIM]
    return m_new, l, acc


def _flash_init(rows):
    return (jnp.full((rows, 1), NEG, F32), jnp.zeros((rows, 1), F32), jnp.zeros((rows, HEAD_DIM), F32))


def _pipelined_tiles(n, n_all, scores_fn, consume_fn, s_a, s_b, carry):
    s_a[...] = scores_fn(0)

    def pair(u, carry):
        k = 2 * u
        s_b[...] = scores_fn(k + 1)
        carry = consume_fn(k, s_a[...], carry)
        s_a[...] = scores_fn(jnp.minimum(k + 2, n_all - 1))
        return consume_fn(k + 1, s_b[...], carry)

    carry = lax.fori_loop(0, lax.shift_right_logical(n, 1), pair, carry)
    return lax.cond((n & 1) == 1, lambda c: consume_fn(n - 1, s_a[...], c), lambda c: c, carry)


def _nsa_main_kernel(q_ref, ks_ref, vs_ref, kw_ref, vw_ref, kaux_ref, selb_ref, ocmp_ref, gate_ref, z_ref, o_ref,
                     s_a, s_b, *, tq, tk, seq):
    g = pl.program_id(1)
    i = pl.program_id(2)
    t0 = pl.multiple_of(i * tq, tq)
    rows = HEADS_PER_GROUP * tq
    gsc = _group_slope_scale(g)
    t_rows, slope_rows = _stack_rows(t0, tq, gsc)
    slopes = [(2.0 ** -(j + 1)) * gsc for j in range(HEADS_PER_GROUP)]
    q_plain = jnp.concatenate([q_ref[:, j * HEAD_DIM:(j + 1) * HEAD_DIM] for j in range(HEADS_PER_GROUP)], axis=0)

    lane = lax.broadcasted_iota(jnp.int32, (tq, LANES), 1)
    selb = jnp.where(lane >= _div_pow2(t0, SEL_BLOCK), NEG, selb_ref[...].astype(F32))
    q_aug = _query_aug(q_ref, slopes, selb)

    def scores(kt):
        j0 = pl.multiple_of(kt * tk, tk)
        return _dot_nt(q_aug, jnp.concatenate([ks_ref[pl.ds(j0, tk), :], kaux_ref[pl.ds(j0, tk), :]], axis=1))

    def consume(kt, s, carry):
        j0 = pl.multiple_of(kt * tk, tk)
        return _flash_step(s, vs_ref[pl.ds(j0, tk), :], _tile_origin_bias(slope_rows, t_rows, j0, tk), carry)

    n_before = _div_pow2(t0 + tk - 1, tk)
    carry = _pipelined_tiles(n_before, seq // tk, scores, consume, s_a, s_b, _flash_init(rows))

    dd = t_rows - (t0 + lax.broadcasted_iota(jnp.int32, (1, tq), 1))
    s = _dot_nt(q_plain, ks_ref[pl.ds(t0, tq), :]) - slope_rows * dd.astype(F32)
    s = jnp.where(dd >= 0, s, NEG)
    _, l_s, acc_s = _flash_step(s, vs_ref[pl.ds(t0, tq), :], jnp.zeros((rows, 1), F32), carry)
    o_slc = acc_s / l_s

    wk = WINDOW + tq
    start = pl.multiple_of(jnp.maximum(t0 - WINDOW, 0), tq)
    kwin = kw_ref[pl.ds(start, wk), :]
    vwin = vw_ref[pl.ds(start, wk), :]
    dw = t_rows - (start + lax.broadcasted_iota(jnp.int32, (1, wk), 1))
    s = _dot_nt(q_plain, kwin) - slope_rows * dw.astype(F32)
    s = jnp.where((dw >= 0) & (dw < WINDOW), s, NEG)
    e = jnp.exp(s - jnp.max(s, axis=-1, keepdims=True))
    o_win = _dot(e.astype(BF16), vwin) / jnp.sum(e, axis=-1, keepdims=True)

    for j in range(HEADS_PER_GROUP):
        sl = slice(j * HEAD_DIM, (j + 1) * HEAD_DIM)
        rs = slice(j * tq, (j + 1) * tq)
        gates = [_sigmoid(gate_ref[:, 3 * j + c:3 * j + c + 1]) for c in range(3)]
        o = gates[0] * ocmp_ref[:, sl] + gates[1] * o_slc[rs] + gates[2] * o_win[rs]
        o_ref[:, sl] = (o * z_ref[:, sl]).astype(o_ref.dtype)


def _nsa_main(qkn, vals, selb, ocmp, gates, zact, batch, seq, tq, tk):
    t = batch * seq
    nq = seq // tq
    gw = HEADS_PER_GROUP * HEAD_DIM
    row = lambda b, g, i: (b * nq + i, g)
    ks_blk = (NSA_HEADS * HEAD_DIM) // HEAD_DIM
    kw_blk = ks_blk + NSA_KV
    return pl.pallas_call(
        functools.partial(_nsa_main_kernel, tq=tq, tk=tk, seq=seq),
        grid=(batch, NSA_KV, nq),
        in_specs=[pl.BlockSpec((tq, gw), row),
                  pl.BlockSpec((seq, HEAD_DIM), lambda b, g, i: (b, ks_blk + g)),
                  pl.BlockSpec((seq, HEAD_DIM), lambda b, g, i: (b, g)),
                  pl.BlockSpec((seq, HEAD_DIM), lambda b, g, i: (b, kw_blk + g)),
                  pl.BlockSpec((seq, HEAD_DIM), lambda b, g, i: (b, NSA_KV + g)),
                  pl.BlockSpec((seq, LANES), lambda b, g, i: (0, 0)),
                  pl.BlockSpec((tq, LANES), row),
                  pl.BlockSpec((tq, gw), row),
                  pl.BlockSpec((tq, LANES), row),
                  pl.BlockSpec((tq, gw), row)],
        out_specs=pl.BlockSpec((tq, gw), row),
        out_shape=jax.ShapeDtypeStruct((t, NSA_HEADS * HEAD_DIM), BF16),
        scratch_shapes=[pltpu.VMEM((HEADS_PER_GROUP * tq, tk), F32)] * 2,
        compiler_params=_params(("parallel", "parallel", "arbitrary")),
        name="nsa_select_window",
    )(qkn, qkn, vals, qkn, vals, _key_aux(seq, tk), selb, ocmp, gates, zact)


def _sortable_key(x):
    bits = pltpu.bitcast(x, jnp.int32)
    bits = jnp.where(bits == INT_MIN, 0, bits)
    return jnp.where(bits < 0, bits ^ 0x7FFFFFFF, bits)


def _dsa_index_kernel(qi_ref, ki_ref, wi_ref, tri_ref, bias_ref, key_scr, *, tq, tk, seq, topk):
    i = pl.program_id(1)
    t0 = i * tq
    n_all = seq // tk
    n_kt = _div_pow2(t0 + tq + tk - 1, tk)
    t = t0 + lax.broadcasted_iota(jnp.int32, (tq, 1), 0)
    wcol = [wi_ref[:, h:h + 1] * (IDX_HEADS ** -0.5 * IDX_DIM ** -0.5) for h in range(IDX_HEADS)]
    hb = 4

    def col_ids(kt):
        return kt * tk + lax.broadcasted_iota(jnp.int32, (1, tk), 1)

    def score_step(kt, _):
        j0 = pl.multiple_of(kt * tk, tk)
        ki = ki_ref[pl.ds(j0, tk), :]
        score = jnp.zeros((tq, tk), F32)
        for h0 in range(0, IDX_HEADS, hb):
            x = _dot_nt(qi_ref[0, h0:h0 + hb].reshape(hb * tq, IDX_DIM), ki)
            for h in range(hb):
                score = score + jnp.maximum(x[h * tq:(h + 1) * tq], 0.0) * wcol[h0 + h]
        key_scr[kt] = jnp.where(col_ids(kt) <= t, _sortable_key(score), INT_MIN)
        return 0

    lax.fori_loop(0, n_kt, score_step, 0)

    def count(pred):
        def step(kt, c):
            key = key_scr[kt]
            for lc in range(tk // LANES):
                c = c + jnp.where(pred(key[:, lc * LANES:(lc + 1) * LANES]), 1, 0)
            return c
        c = lax.fori_loop(0, n_kt, step, jnp.zeros((tq, LANES), jnp.int32))
        return jnp.sum(c, axis=-1, keepdims=True)

    v = jnp.where(count(lambda k: k >= 0) >= topk, 0, INT_MIN)

    def bit_step(b, v):
        cand = v | (jnp.int32(1) << (30 - b))
        return jnp.where(count(lambda k: k >= cand) >= topk, cand, v)

    v = lax.fori_loop(0, 31, bit_step, v)
    need = (topk - count(lambda k: k > v)).astype(F32)

    def out_step(kt, carry):
        key = key_scr[kt]
        eq = key == v
        eqf = jnp.where(eq, 1.0, 0.0)
        before = carry + _dot(eqf.astype(BF16), tri_ref[...]) - eqf
        keep = ((key > v) | (eq & (before < need))) & (col_ids(kt) <= t)
        bias_ref[0, 0, kt] = jnp.where(keep, 0.0, NEG).astype(bias_ref.dtype)
        return carry + jnp.sum(eqf, axis=-1, keepdims=True)

    lax.fori_loop(0, n_kt, out_step, jnp.zeros((tq, 1), F32))

    def fill_step(kt, _):
        bias_ref[0, 0, kt] = jnp.full((tq, tk), NEG, bias_ref.dtype)
        return 0

    lax.fori_loop(n_kt, n_all, fill_step, 0)


def _dsa_index(qi_heads, ki, wi, batch, seq, tq, tk):
    nq = seq // tq
    n_all = seq // tk
    topk = min(DSA_TOPK_MAX, seq // 4)
    tri = jnp.asarray(np.triu(np.ones((tk, tk), np.float32)), dtype=BF16)
    return pl.pallas_call(
        functools.partial(_dsa_index_kernel, tq=tq, tk=tk, seq=seq, topk=topk),
        grid=(batch, nq),
        in_specs=[pl.BlockSpec((1, IDX_HEADS, tq, IDX_DIM), lambda b, i: (b, 0, i, 0)),
                  pl.BlockSpec((seq, IDX_DIM), lambda b, i: (b, 0)),
                  pl.BlockSpec((tq, IDX_HEADS), lambda b, i: (b * nq + i, 0)),
                  pl.BlockSpec((tk, tk), lambda b, i: (0, 0))],
        out_specs=pl.BlockSpec((1, 1, n_all, tq, tk), lambda b, i: (b, i, 0, 0, 0)),
        out_shape=jax.ShapeDtypeStruct((batch, nq, n_all, tq, tk), BF16),
        scratch_shapes=[pltpu.VMEM((n_all, tq, tk), jnp.int32)],
        compiler_params=_params(("parallel", "arbitrary")),
        name="dsa_index_topk",
    )(qi_heads, ki, wi, tri)


def _dsa_attn_kernel(q_ref, k_ref, v_ref, kaux_ref, bias_ref, z_ref, o_ref, s_a, s_b, *, tq, tk, seq):
    g = pl.program_id(1)
    i = pl.program_id(2)
    t0 = i * tq
    rows = HEADS_PER_GROUP * tq
    gsc = _group_slope_scale(g)
    t_rows, slope_rows = _stack_rows(t0, tq, gsc)
    slopes = [(2.0 ** -(j + 1)) * gsc for j in range(HEADS_PER_GROUP)]
    q_aug = _query_aug(q_ref, slopes, jnp.zeros((tq, LANES), F32))

    def scores(kt):
        j0 = pl.multiple_of(kt * tk, tk)
        return _dot_nt(q_aug, jnp.concatenate([k_ref[pl.ds(j0, tk), :], kaux_ref[pl.ds(j0, tk), :]], axis=1))

    def consume(kt, s, carry):
        j0 = pl.multiple_of(kt * tk, tk)
        bias = bias_ref[0, 0, kt].astype(F32)
        s = s + jnp.concatenate([bias] * HEADS_PER_GROUP, axis=0)
        return _flash_step(s, v_ref[pl.ds(j0, tk), :], _tile_origin_bias(slope_rows, t_rows, j0, tk), carry)

    n_kt = _div_pow2(t0 + tq + tk - 1, tk)
    _, l, acc = _pipelined_tiles(n_kt, seq // tk, scores, consume, s_a, s_b, _flash_init(rows))
    o = acc / l
    for j in range(HEADS_PER_GROUP):
        sl = slice(j * HEAD_DIM, (j + 1) * HEAD_DIM)
        o_ref[:, sl] = (o[j * tq:(j + 1) * tq] * z_ref[:, sl]).astype(o_ref.dtype)


def _dsa_attn(qkn, vals, bias, zact, batch, seq, tq, tk):
    t = batch * seq
    nq = seq // tq
    n_all = seq // tk
    gw = HEADS_PER_GROUP * HEAD_DIM
    row = lambda b, g, i: (b * nq + i, g)
    q_blk = (NSA_HEADS + 2 * NSA_KV) * HEAD_DIM // gw
    k_blk = (NSA_HEADS + 2 * NSA_KV + DSA_HEADS)
    return pl.pallas_call(
        functools.partial(_dsa_attn_kernel, tq=tq, tk=tk, seq=seq),
        grid=(batch, DSA_KV, nq),
        in_specs=[pl.BlockSpec((tq, gw), lambda b, g, i: (b * nq + i, q_blk + g)),
                  pl.BlockSpec((seq, HEAD_DIM), lambda b, g, i: (b, k_blk + g)),
                  pl.BlockSpec((seq, HEAD_DIM), lambda b, g, i: (b, 2 * NSA_KV + g)),
                  pl.BlockSpec((seq, LANES), lambda b, g, i: (0, 0)),
                  pl.BlockSpec((1, 1, n_all, tq, tk), lambda b, g, i: (b, i, 0, 0, 0)),
                  pl.BlockSpec((tq, gw), lambda b, g, i: (b * nq + i, NSA_KV + g))],
        out_specs=pl.BlockSpec((tq, gw), row),
        out_shape=jax.ShapeDtypeStruct((t, DSA_HEADS * HEAD_DIM), BF16),
        scratch_shapes=[pltpu.VMEM((HEADS_PER_GROUP * tq, tk), F32)] * 2,
        compiler_params=_params(("parallel", "parallel", "arbitrary")),
        name="dsa_attention",
    )(qkn, qkn, vals, _key_aux(seq, tk), bias, zact)


def _merge_kernel(on_ref, od_ref, sn_ref, sd_ref, x_ref, wun_ref, wud_ref, wout_ref, o_ref):
    y = sn_ref[...] * _dot(on_ref[...], wun_ref[...]) + sd_ref[...] * _dot(od_ref[...], wud_ref[...])
    o_ref[...] = x_ref[...] + _dot(y.astype(BF16), wout_ref[...])


def _merge(o_nsa, o_dsa, gate_act, x, w_up_nsa, w_up_dsa, w_out, tm):
    t, d = x.shape
    const = lambda i: (0, 0)
    return pl.pallas_call(
        _merge_kernel,
        grid=(t // tm,),
        in_specs=[pl.BlockSpec((tm, o_nsa.shape[1]), lambda i: (i, 0)),
                  pl.BlockSpec((tm, o_dsa.shape[1]), lambda i: (i, 0)),
                  pl.BlockSpec((tm, d), lambda i: (i, 0)),
                  pl.BlockSpec((tm, d), lambda i: (i, 1)),
                  pl.BlockSpec((tm, d), lambda i: (i, 0)),
                  pl.BlockSpec(w_up_nsa.shape, const), pl.BlockSpec(w_up_dsa.shape, const),
                  pl.BlockSpec(w_out.shape, const)],
        out_specs=pl.BlockSpec((tm, d), lambda i: (i, 0)),
        out_shape=jax.ShapeDtypeStruct((t, d), F32),
        compiler_params=_params(("parallel",)),
        name="merge_out_proj",
    )(o_nsa, o_dsa, gate_act, gate_act, x, w_up_nsa, w_up_dsa, w_out)


def _ple_kernel(x_ref, p_ref, g_ref, wg_ref, wp_ref, o_ref):
    x = x_ref[...]
    ms = jnp.mean(x * x, axis=-1, keepdims=True)
    r = (x * lax.rsqrt(ms + EPS) * g_ref[...]).astype(BF16)
    gate = _sigmoid(_dot(r, wg_ref[...]))
    o_ref[...] = x + _dot(p_ref[...].astype(BF16), wp_ref[...]) * gate


def _ple(x, p, g, w_gate, w_proj, tm):
    t, d = x.shape
    const = lambda i: (0, 0)
    return pl.pallas_call(
        _ple_kernel,
        grid=(t // tm,),
        in_specs=[pl.BlockSpec((tm, d), lambda i: (i, 0)),
                  pl.BlockSpec((tm, p.shape[1]), lambda i: (i, 0)),
                  pl.BlockSpec((1, d), const),
                  pl.BlockSpec(w_gate.shape, const), pl.BlockSpec(w_proj.shape, const)],
        out_specs=pl.BlockSpec((tm, d), lambda i: (i, 0)),
        out_shape=jax.ShapeDtypeStruct((t, d), F32),
        compiler_params=_params(("parallel",)),
        name="ple_gate",
    )(x, p, g.reshape(1, d), w_gate, w_proj)


def _split_in_proj(w):
    widths = [NSA_HEADS * HEAD_DIM] + [NSA_KV * HEAD_DIM] * 6 + [NSA_HEADS * 3, NSA_HEADS * HEAD_DIM,
              DSA_HEADS * HEAD_DIM, DSA_KV * HEAD_DIM, DSA_KV * HEAD_DIM, IDX_HEADS * IDX_DIM, IDX_DIM,
              IDX_HEADS, DSA_HEADS * HEAD_DIM, D_MODEL, D_MODEL]
    offs = np.concatenate([[0], np.cumsum(widths)])
    assert offs[-1] == w.shape[1]
    (nq, nkc, nvc, nks, nvs, nkw, nvw, ng, nz, dq, dk, dv, iq, ik, iw, dz, mgn, mgd) = [
        w[:, offs[k]:offs[k + 1]] for k in range(len(widths))]
    pad = jnp.zeros((w.shape[0], LANES - IDX_DIM - IDX_HEADS - NSA_HEADS * 3), w.dtype)
    return dict(
        normed=jnp.concatenate([nq, nks, nkw, dq, dk], axis=1).astype(BF16),
        vals=jnp.concatenate([nvs, nvw, dv, iq], axis=1).astype(BF16),
        raw=jnp.concatenate([nkc, nvc, ik, iw, ng, pad], axis=1).astype(BF16),
        zact=jnp.concatenate([nz, dz], axis=1).astype(BF16),
        merge=jnp.concatenate([mgn, mgd], axis=1).astype(BF16),
    )


def _layer(x, p, norm_g, w_in, nsa_q_g, nsa_kc_g, nsa_ks_g, nsa_kw_g, cmp_pe_k, cmp_w1_k, cmp_w2_k,
           cmp_pe_v, cmp_w1_v, cmp_w2_v, dsa_q_g, dsa_k_g, w_up_nsa, w_up_dsa, w_out, ple_norm_g,
           w_ple_gate, w_ple_proj):
    batch, seq, d = x.shape
    t = batch * seq
    assert seq % LANES == 0 and seq >= WINDOW + LANES
    x2 = x.reshape(t, d)
    tm = min(1024, t)

    w = _split_in_proj(w_in)
    h = _rmsnorm(x2, norm_g, min(512, t))

    gains = jnp.concatenate([jnp.tile(nsa_q_g * ATTN_SCALE, NSA_HEADS), jnp.tile(nsa_ks_g, NSA_KV),
                             jnp.tile(nsa_kw_g, NSA_KV), jnp.tile(dsa_q_g * ATTN_SCALE, DSA_HEADS),
                             jnp.tile(dsa_k_g, DSA_KV)])
    qkn = _proj(h, w["normed"], tm, w["normed"].shape[1] // 2, BF16, gain=gains)
    vals = _proj(h, w["vals"], tm, w["vals"].shape[1] // 2, BF16)
    kc_raw, vc_raw, misc = _proj_split(h, w["raw"], tm, (NSA_KV * HEAD_DIM, NSA_KV * HEAD_DIM, LANES))
    zact = _proj(h, w["zact"], tm, 1024, F32, act="silu")
    gate_act = _proj(h, w["merge"], tm, 1024, F32, act="sigmoid")

    ki = misc[:, :IDX_DIM].astype(BF16)
    wi = misc[:, IDX_DIM:IDX_DIM + IDX_HEADS]
    ng = misc[:, IDX_DIM + IDX_HEADS:IDX_DIM + IDX_HEADS + NSA_HEADS * 3]
    per_group = HEADS_PER_GROUP * 3
    gates = jnp.pad(ng.reshape(t, NSA_KV, per_group), ((0, 0), (0, 0), (0, LANES - per_group))).reshape(t, NSA_KV * LANES)
    iq_off = 3 * NSA_KV * HEAD_DIM
    qi_heads = vals[:, iq_off:].reshape(batch, seq, IDX_HEADS, IDX_DIM).transpose(0, 2, 1, 3)

    kc = _compress(kc_raw, cmp_pe_k, cmp_w1_k, cmp_w2_k, nsa_kc_g, batch, seq)
    vc = _compress(vc_raw, cmp_pe_v, cmp_w1_v, cmp_w2_v, None, batch, seq)

    tq = 128
    tk = min(512, seq)
    ocmp, selb = _nsa_cmp(qkn, kc, vc, batch, seq, min(256, seq))
    o_nsa = _nsa_main(qkn, vals, selb, ocmp, gates, zact, batch, seq, tq, tk)

    bias = _dsa_index(qi_heads, ki, wi, batch, seq, tq, tk)
    o_dsa = _dsa_attn(qkn, vals, bias, zact, batch, seq, tq, tk)

    x1 = _merge(o_nsa, o_dsa, gate_act, x2, w_up_nsa.astype(BF16), w_up_dsa.astype(BF16), w_out.astype(BF16),
                min(256, t))
    x3 = _ple(x1, p.reshape(t, PLE_DIM), ple_norm_g, w_ple_gate.astype(BF16), w_ple_proj.astype(BF16), min(256, t))
    return x3.reshape(batch, seq, d)


def kernel(x, p, norm_g, w_in, nsa_q_g, nsa_kc_g, nsa_ks_g, nsa_kw_g, cmp_pe_k, cmp_w1_k, cmp_w2_k, cmp_pe_v, cmp_w1_v, cmp_w2_v, dsa_q_g, dsa_k_g, w_up_nsa, w_up_dsa, w_out, ple_norm_g, w_ple_gate, w_ple_proj):
    depth = w_in.shape[0]
    for i in range(depth):
        x = _layer(x, p[i], norm_g[i], w_in[i], nsa_q_g[i], nsa_kc_g[i], nsa_ks_g[i], nsa_kw_g[i],
                   cmp_pe_k[i], cmp_w1_k[i], cmp_w2_k[i], cmp_pe_v[i], cmp_w1_v[i], cmp_w2_v[i],
                   dsa_q_g[i], dsa_k_g[i], w_up_nsa[i], w_up_dsa[i], w_out[i], ple_norm_g[i],
                   w_ple_gate[i], w_ple_proj[i])
    return x
```

```python
import functools

import numpy as np
import jax
import jax.numpy as jnp
from jax import lax
from jax.experimental import pallas as pl
from jax.experimental.pallas import tpu as pltpu

D_MODEL = 2048
HEAD_DIM = 128
NSA_HEADS = 8
NSA_KV = 2
DSA_HEADS = 8
DSA_KV = 2
HEADS_PER_GROUP = 4
CMP_LEN = 32
CMP_STRIDE = 16
CMP_HIDDEN = 256
SEL_BLOCK = 64
SEL_TOPN = 16
WINDOW = 512
IDX_HEADS = 16
IDX_DIM = 64
DSA_TOPK_MAX = 256
PLE_DIM = 256
EPS = 1e-6
NEG = -1e30
FORCE_SCORE = 1e4
ATTN_SCALE = HEAD_DIM ** -0.5
LANES = 128
INT_MIN = -(2 ** 31)

VMEM_LIMIT_BYTES = 56 * 1024 * 1024

F32 = jnp.float32
BF16 = jnp.bfloat16


def _params(semantics):
    return pltpu.CompilerParams(dimension_semantics=semantics, vmem_limit_bytes=VMEM_LIMIT_BYTES)


def _sigmoid(x):
    return 1.0 / (1.0 + jnp.exp(-x))


def _dot(a, b):
    return jnp.dot(a, b, preferred_element_type=F32)


def _dot_nt(a, b):
    return lax.dot_general(a, b, (((1,), (1,)), ((), ())), preferred_element_type=F32)


def _log2(n):
    assert n > 0 and n & (n - 1) == 0, n
    return n.bit_length() - 1


def _div_pow2(x, n):
    return lax.shift_right_logical(x, jnp.int32(_log2(n)))


def _group_slope_scale(g):
    return jnp.where(g == 0, 1.0, 2.0 ** -HEADS_PER_GROUP).astype(F32)


def _rmsnorm_kernel(x_ref, g_ref, o_ref):
    x = x_ref[...]
    ms = jnp.mean(x * x, axis=-1, keepdims=True)
    o_ref[...] = (x * lax.rsqrt(ms + EPS) * g_ref[...]).astype(o_ref.dtype)


def _rmsnorm(x, g, tm):
    t, d = x.shape
    return pl.pallas_call(
        _rmsnorm_kernel,
        grid=(t // tm,),
        in_specs=[pl.BlockSpec((tm, d), lambda i: (i, 0)), pl.BlockSpec((1, d), lambda i: (0, 0))],
        out_specs=pl.BlockSpec((tm, d), lambda i: (i, 0)),
        out_shape=jax.ShapeDtypeStruct((t, d), BF16),
        compiler_params=_params(("parallel",)),
        name="in_rmsnorm",
    )(x, g.reshape(1, d))


def _proj_headnorm_kernel(h_ref, w_ref, g_ref, o_ref):
    y = _dot(h_ref[...], w_ref[...])
    for hd in range(y.shape[1] // HEAD_DIM):
        sl = slice(hd * HEAD_DIM, (hd + 1) * HEAD_DIM)
        yh = y[:, sl]
        ms = jnp.mean(yh * yh, axis=-1, keepdims=True)
        o_ref[:, sl] = (yh * lax.rsqrt(ms + EPS) * g_ref[:, sl]).astype(o_ref.dtype)


def _proj_act_kernel(h_ref, w_ref, o_ref, *, act):
    y = _dot(h_ref[...], w_ref[...])
    if act == "silu":
        y = y * _sigmoid(y)
    elif act == "sigmoid":
        y = _sigmoid(y)
    o_ref[...] = y.astype(o_ref.dtype)


def _proj_split_kernel(h_ref, w_ref, *o_refs):
    y = _dot(h_ref[...], w_ref[...])
    off = 0
    for o_ref in o_refs:
        n = o_ref.shape[1]
        o_ref[...] = y[:, off:off + n].astype(o_ref.dtype)
        off += n


def _proj(h, w, tm, tn, out_dtype, act=None, gain=None):
    t, d = h.shape
    n = w.shape[1]
    in_specs = [pl.BlockSpec((tm, d), lambda i, j: (i, 0)), pl.BlockSpec((d, tn), lambda i, j: (0, j))]
    args = [h, w]
    if gain is not None:
        body = _proj_headnorm_kernel
        in_specs.append(pl.BlockSpec((1, tn), lambda i, j: (0, j)))
        args.append(gain.reshape(1, n))
    else:
        body = functools.partial(_proj_act_kernel, act=act)
    return pl.pallas_call(
        body,
        grid=(t // tm, n // tn),
        in_specs=in_specs,
        out_specs=pl.BlockSpec((tm, tn), lambda i, j: (i, j)),
        out_shape=jax.ShapeDtypeStruct((t, n), out_dtype),
        compiler_params=_params(("parallel", "arbitrary")),
        name="in_proj_" + ("headnorm" if gain is not None else str(act)),
    )(*args)


def _proj_heads_kernel(h_ref, w_ref, o_ref):
    y = _dot(h_ref[...], w_ref[...])
    for hd in range(o_ref.shape[1]):
        o_ref[0, hd] = y[:, hd * IDX_DIM:(hd + 1) * IDX_DIM].astype(o_ref.dtype)


def _proj_heads(h, w, tm, batch, seq):
    t, d = h.shape
    per_batch = seq // tm
    return pl.pallas_call(
        _proj_heads_kernel,
        grid=(t // tm,),
        in_specs=[pl.BlockSpec((tm, d), lambda i: (i, 0)), pl.BlockSpec(w.shape, lambda i: (0, 0))],
        out_specs=pl.BlockSpec((1, IDX_HEADS, tm, IDX_DIM), lambda i: (i // per_batch, 0, i % per_batch, 0)),
        out_shape=jax.ShapeDtypeStruct((batch, IDX_HEADS, seq, IDX_DIM), BF16),
        compiler_params=_params(("parallel",)),
        name="in_proj_heads",
    )(h, w)


def _proj_split(h, w, tm, widths):
    t, d = h.shape
    n = w.shape[1]
    return pl.pallas_call(
        _proj_split_kernel,
        grid=(t // tm,),
        in_specs=[pl.BlockSpec((tm, d), lambda i: (i, 0)), pl.BlockSpec((d, n), lambda i: (0, 0))],
        out_specs=[pl.BlockSpec((tm, wd), lambda i: (i, 0)) for wd in widths],
        out_shape=[jax.ShapeDtypeStruct((t, wd), F32) for wd in widths],
        compiler_params=_params(("parallel",)),
        name="in_proj_split",
    )(h, w)


def _compress_kernel(x_ref, pea_ref, peb_ref, wa_ref, wb_ref, w2_ref, g_ref, o_ref, *, normalize):
    x = x_ref[...]
    nch = x.shape[0]
    ha = _dot((x + pea_ref[...]).astype(BF16), wa_ref[...])
    hb = _dot((x + peb_ref[...]).astype(BF16), wb_ref[...])
    hid = ha + pltpu.roll(hb, nch - 1, axis=0)
    act = (hid * _sigmoid(hid)).astype(BF16)
    for g in range(NSA_KV):
        o = _dot(act[:, g * CMP_HIDDEN:(g + 1) * CMP_HIDDEN], w2_ref[...])
        if normalize:
            ms = jnp.mean(o * o, axis=-1, keepdims=True)
            o = o * lax.rsqrt(ms + EPS) * g_ref[...]
        o_ref[0, :, g * HEAD_DIM:(g + 1) * HEAD_DIM] = o.astype(o_ref.dtype)


def _expand_compress_params(pe, w1):
    half = CMP_LEN // 2
    w = w1.reshape(CMP_LEN, HEAD_DIM, CMP_HIDDEN)
    eye = jnp.eye(NSA_KV, dtype=w1.dtype)

    def expand_w(wh):
        return jnp.einsum("ldj,hg->lhdgj", wh, eye).reshape(half * NSA_KV * HEAD_DIM, NSA_KV * CMP_HIDDEN)

    def expand_pe(ph):
        return jnp.broadcast_to(ph[:, None, :], (half, NSA_KV, HEAD_DIM)).reshape(1, half * NSA_KV * HEAD_DIM)

    return (expand_pe(pe[:half]), expand_pe(pe[half:]),
            expand_w(w[:half]).astype(BF16), expand_w(w[half:]).astype(BF16))


def _compress(raw, pe, w1, w2, gain, batch, seq):
    nch = seq // CMP_STRIDE
    width = CMP_STRIDE * NSA_KV * HEAD_DIM
    x = raw.reshape(batch * nch, width)
    pea, peb, wa, wb = _expand_compress_params(pe, w1)
    normalize = gain is not None
    g = (gain if normalize else jnp.ones((HEAD_DIM,), F32)).reshape(1, HEAD_DIM)
    const = lambda b: (0, 0)
    return pl.pallas_call(
        functools.partial(_compress_kernel, normalize=normalize),
        grid=(batch,),
        in_specs=[pl.BlockSpec((nch, width), lambda b: (b, 0)),
                  pl.BlockSpec((1, width), const), pl.BlockSpec((1, width), const),
                  pl.BlockSpec(wa.shape, const), pl.BlockSpec(wb.shape, const),
                  pl.BlockSpec((CMP_HIDDEN, HEAD_DIM), const), pl.BlockSpec((1, HEAD_DIM), const)],
        out_specs=pl.BlockSpec((1, nch, NSA_KV * HEAD_DIM), lambda b: (b, 0, 0)),
        out_shape=jax.ShapeDtypeStruct((batch, nch, NSA_KV * HEAD_DIM), BF16),
        compiler_params=_params(("parallel",)),
        name="nsa_compress",
    )(x, pea, peb, wa, wb, w2.astype(BF16), g)


def _cmp_to_sel_matrix(nch, n_sel_pad):
    n_cmp = nch - 1
    c0 = np.arange(nch)[:, None] * CMP_STRIDE
    s0 = np.arange(n_sel_pad)[None, :] * SEL_BLOCK
    ov = np.clip(np.minimum(c0 + CMP_LEN, s0 + SEL_BLOCK) - np.maximum(c0, s0), 0, None) / CMP_LEN
    ov[n_cmp:] = 0.0
    return jnp.asarray(ov.T, dtype=BF16)


def _nsa_cmp_kernel(q_ref, kc_ref, vc_ref, mt_ref, eye_ref, ocmp_ref, selb_ref, *, tq, nch, n_sel):
    g = pl.program_id(1)
    i = pl.program_id(2)
    t = i * tq + lax.broadcasted_iota(jnp.int32, (tq, 1), 0)
    c_end = lax.broadcasted_iota(jnp.int32, (1, nch), 1) * CMP_STRIDE + (CMP_LEN - 1)
    vis = c_end <= t
    distc = t.astype(F32) - (c_end.astype(F32) - (CMP_LEN - 1) / 2.0)
    live = (t >= CMP_LEN - 1).astype(F32)
    kc = kc_ref[0]
    vc = vc_ref[0]
    gsc = _group_slope_scale(g)
    psum = jnp.zeros((tq, nch), F32)
    for j in range(HEADS_PER_GROUP):
        slope = (2.0 ** -(j + 1)) * gsc
        sl = slice(j * HEAD_DIM, (j + 1) * HEAD_DIM)
        s = _dot_nt(q_ref[:, sl], kc) - slope * distc
        s = jnp.where(vis, s, NEG)
        e = jnp.exp(s - jnp.max(s, axis=-1, keepdims=True))
        p = e * (live / jnp.sum(e, axis=-1, keepdims=True))
        psum = psum + p
        ocmp_ref[:, sl] = _dot(p.astype(BF16), vc)

    imp = _dot_nt(mt_ref[...], psum.astype(BF16))
    blk = lax.broadcasted_iota(jnp.int32, (LANES, 1), 0)
    blk_t = _div_pow2(i * tq + lax.broadcasted_iota(jnp.int32, (1, tq), 1), SEL_BLOCK)
    valid = blk <= blk_t
    forced = (blk == 0) | (blk == blk_t) | (blk == blk_t - 1)
    vals = jnp.where(valid & jnp.logical_not(forced) & (blk < n_sel), imp, -1.0)
    sel = valid & forced
    for _ in range(SEL_TOPN - 3):
        mx = jnp.max(vals, axis=0, keepdims=True)
        first = jnp.min(jnp.where(vals == mx, blk, LANES), axis=0, keepdims=True)
        pick = blk == first
        sel = sel | pick
        vals = jnp.where(pick, -3.0, vals)
    masked = jnp.where(sel & valid, 0.0, NEG).astype(BF16)
    selb_ref[...] = _dot_nt(eye_ref[...], masked).astype(selb_ref.dtype)


def _nsa_cmp(qkn, kc, vc, batch, seq, tq):
    t = batch * seq
    nch = seq // CMP_STRIDE
    n_sel = seq // SEL_BLOCK
    assert n_sel <= LANES
    nq = seq // tq
    gw = HEADS_PER_GROUP * HEAD_DIM
    mt = _cmp_to_sel_matrix(nch, LANES)
    eye = jnp.asarray(np.eye(tq, dtype=np.float32), dtype=BF16)
    row = lambda b, g, i: (b * nq + i, g)
    return pl.pallas_call(
        functools.partial(_nsa_cmp_kernel, tq=tq, nch=nch, n_sel=n_sel),
        grid=(batch, NSA_KV, nq),
        in_specs=[pl.BlockSpec((tq, gw), row),
                  pl.BlockSpec((1, nch, HEAD_DIM), lambda b, g, i: (b, 0, g)),
                  pl.BlockSpec((1, nch, HEAD_DIM), lambda b, g, i: (b, 0, g)),
                  pl.BlockSpec((LANES, nch), lambda b, g, i: (0, 0)),
                  pl.BlockSpec((tq, tq), lambda b, g, i: (0, 0))],
        out_specs=[pl.BlockSpec((tq, gw), row), pl.BlockSpec((tq, LANES), row)],
        out_shape=[jax.ShapeDtypeStruct((t, NSA_HEADS * HEAD_DIM), F32),
                   jax.ShapeDtypeStruct((t, NSA_KV * LANES), BF16)],
        compiler_params=_params(("parallel", "parallel", "parallel")),
        name="nsa_cmp_select",
    )(qkn, kc, vc, mt, eye)


def _stack_rows(t0, tq, gsc):
    rows = HEADS_PER_GROUP * tq
    r = lax.broadcasted_iota(jnp.int32, (rows, 1), 0)
    t_rows = t0 + (r & (tq - 1))
    j_rows = _div_pow2(r, tq)
    slope = jnp.zeros((rows, 1), F32)
    for j in range(HEADS_PER_GROUP):
        slope = jnp.where(j_rows == j, 2.0 ** -(j + 1), slope)
    return t_rows, slope * gsc


def _key_aux(seq, tk):
    assert tk <= 2 * 256
    j = np.arange(seq)
    aux = (j[:, None] // SEL_BLOCK == np.arange(LANES)[None, :]).astype(np.float32)
    aux[:, 0] = (j % tk) - tk // 2
    return jnp.asarray(aux, dtype=BF16)


def _query_aug(q_ref, slopes, mask_cols):
    lane0 = lax.broadcasted_iota(jnp.int32, mask_cols.shape, 1) == 0
    parts = []
    for j in range(HEADS_PER_GROUP):
        aux = jnp.where(lane0, slopes[j], mask_cols).astype(BF16)
        parts.append(jnp.concatenate([q_ref[:, j * HEAD_DIM:(j + 1) * HEAD_DIM], aux], axis=1))
    return jnp.concatenate(parts, axis=0)


def _tile_origin_bias(slope_rows, t_rows, j0, tk):
    return slope_rows * (j0 + tk // 2 - t_rows).astype(F32)


def _flash_step(s, v, off, carry):
    m, l, acc = carry
    m_new = jnp.maximum(m, jnp.max(s, axis=-1, keepdims=True) + off)
    alpha = jnp.exp(m - m_new)
    p = jnp.exp(s - (m_new - off)).astype(BF16)
    pv = _dot(p, jnp.concatenate([v, jnp.ones_like(v)], axis=1))
    l = alpha * l + pv[:, HEAD_DIM:HEAD_DIM + 1]
    acc = alpha * acc + pv[:, :HEAD_DIM]
    return m_new, l, acc


def _flash_init(rows):
    return (jnp.full((rows, 1), NEG, F32), jnp.zeros((rows, 1), F32), jnp.zeros((rows, HEAD_DIM), F32))


def _pipelined_tiles(n, n_all, scores_fn, consume_fn, s_a, s_b, carry):
    s_a[...] = scores_fn(0)

    def pair(u, carry):
        k = 2 * u
        s_b[...] = scores_fn(k + 1)
        carry = consume_fn(k, s_a[...], carry)
        s_a[...] = scores_fn(jnp.minimum(k + 2, n_all - 1))
        return consume_fn(k + 1, s_b[...], carry)

    carry = lax.fori_loop(0, lax.shift_right_logical(n, 1), pair, carry)
    return lax.cond((n & 1) == 1, lambda c: consume_fn(n - 1, s_a[...], c), lambda c: c, carry)


def _nsa_main_kernel(q_ref, ks_ref, vs_ref, kw_ref, vw_ref, kaux_ref, selb_ref, ocmp_ref, gate_ref, z_ref, o_ref,
                     s_a, s_b, *, tq, tk, seq):
    g = pl.program_id(1)
    i = pl.program_id(2)
    t0 = pl.multiple_of(i * tq, tq)
    rows = HEADS_PER_GROUP * tq
    gsc = _group_slope_scale(g)
    t_rows, slope_rows = _stack_rows(t0, tq, gsc)
    slopes = [(2.0 ** -(j + 1)) * gsc for j in range(HEADS_PER_GROUP)]
    q_plain = jnp.concatenate([q_ref[:, j * HEAD_DIM:(j + 1) * HEAD_DIM] for j in range(HEADS_PER_GROUP)], axis=0)

    lane = lax.broadcasted_iota(jnp.int32, (tq, LANES), 1)
    selb = jnp.where(lane >= _div_pow2(t0, SEL_BLOCK), NEG, selb_ref[...].astype(F32))
    q_aug = _query_aug(q_ref, slopes, selb)

    def scores(kt):
        j0 = pl.multiple_of(kt * tk, tk)
        return _dot_nt(q_aug, jnp.concatenate([ks_ref[pl.ds(j0, tk), :], kaux_ref[pl.ds(j0, tk), :]], axis=1))

    def consume(kt, s, carry):
        j0 = pl.multiple_of(kt * tk, tk)
        return _flash_step(s, vs_ref[pl.ds(j0, tk), :], _tile_origin_bias(slope_rows, t_rows, j0, tk), carry)

    n_before = _div_pow2(t0 + tk - 1, tk)
    carry = _pipelined_tiles(n_before, seq // tk, scores, consume, s_a, s_b, _flash_init(rows))

    dd = t_rows - (t0 + lax.broadcasted_iota(jnp.int32, (1, tq), 1))
    s = _dot_nt(q_plain, ks_ref[pl.ds(t0, tq), :]) - slope_rows * dd.astype(F32)
    s = jnp.where(dd >= 0, s, NEG)
    _, l_s, acc_s = _flash_step(s, vs_ref[pl.ds(t0, tq), :], jnp.zeros((rows, 1), F32), carry)
    o_slc = acc_s / l_s

    wk = WINDOW + tq
    start = pl.multiple_of(jnp.maximum(t0 - WINDOW, 0), tq)
    kwin = kw_ref[pl.ds(start, wk), :]
    vwin = vw_ref[pl.ds(start, wk), :]
    dw = t_rows - (start + lax.broadcasted_iota(jnp.int32, (1, wk), 1))
    s = _dot_nt(q_plain, kwin) - slope_rows * dw.astype(F32)
    s = jnp.where((dw >= 0) & (dw < WINDOW), s, NEG)
    e = jnp.exp(s - jnp.max(s, axis=-1, keepdims=True))
    o_win = _dot(e.astype(BF16), vwin) / jnp.sum(e, axis=-1, keepdims=True)

    for j in range(HEADS_PER_GROUP):
        sl = slice(j * HEAD_DIM, (j + 1) * HEAD_DIM)
        rs = slice(j * tq, (j + 1) * tq)
        gates = [_sigmoid(gate_ref[:, 3 * j + c:3 * j + c + 1]) for c in range(3)]
        o = gates[0] * ocmp_ref[:, sl] + gates[1] * o_slc[rs] + gates[2] * o_win[rs]
        o_ref[:, sl] = (o * z_ref[:, sl]).astype(o_ref.dtype)


def _nsa_main(qkn, vals, selb, ocmp, gates, zact, batch, seq, tq, tk):
    t = batch * seq
    nq = seq // tq
    gw = HEADS_PER_GROUP * HEAD_DIM
    row = lambda b, g, i: (b * nq + i, g)
    ks_blk = (NSA_HEADS * HEAD_DIM) // HEAD_DIM
    kw_blk = ks_blk + NSA_KV
    return pl.pallas_call(
        functools.partial(_nsa_main_kernel, tq=tq, tk=tk, seq=seq),
        grid=(batch, NSA_KV, nq),
        in_specs=[pl.BlockSpec((tq, gw), row),
                  pl.BlockSpec((seq, HEAD_DIM), lambda b, g, i: (b, ks_blk + g)),
                  pl.BlockSpec((seq, HEAD_DIM), lambda b, g, i: (b, g)),
                  pl.BlockSpec((seq, HEAD_DIM), lambda b, g, i: (b, kw_blk + g)),
                  pl.BlockSpec((seq, HEAD_DIM), lambda b, g, i: (b, NSA_KV + g)),
                  pl.BlockSpec((seq, LANES), lambda b, g, i: (0, 0)),
                  pl.BlockSpec((tq, LANES), row),
                  pl.BlockSpec((tq, gw), row),
                  pl.BlockSpec((tq, LANES), row),
                  pl.BlockSpec((tq, gw), row)],
        out_specs=pl.BlockSpec((tq, gw), row),
        out_shape=jax.ShapeDtypeStruct((t, NSA_HEADS * HEAD_DIM), BF16),
        scratch_shapes=[pltpu.VMEM((HEADS_PER_GROUP * tq, tk), F32)] * 2,
        compiler_params=_params(("parallel", "parallel", "arbitrary")),
        name="nsa_select_window",
    )(qkn, qkn, vals, qkn, vals, _key_aux(seq, tk), selb, ocmp, gates, zact)


def _sortable_key(x):
    bits = pltpu.bitcast(x, jnp.int32)
    bits = jnp.where(bits == INT_MIN, 0, bits)
    return jnp.where(bits < 0, bits ^ 0x7FFFFFFF, bits)


def _dsa_index_kernel(qi_ref, ki_ref, wi_ref, eye_ref, tri_ref, bias_ref, key_scr, *, tq, tk, seq, topk):
    i = pl.program_id(1)
    t0 = i * tq
    n_all = seq // tk
    n_kt = _div_pow2(t0 + tq + tk - 1, tk)
    t = t0 + lax.broadcasted_iota(jnp.int32, (1, tq), 1)
    wrow = wi_ref[...] * (IDX_HEADS ** -0.5 * IDX_DIM ** -0.5)
    hb = 4

    def key_ids(kt):
        return kt * tk + lax.broadcasted_iota(jnp.int32, (tk, 1), 0)

    def score_step(kt, _):
        j0 = pl.multiple_of(kt * tk, tk)
        ki = ki_ref[pl.ds(j0, tk), :]
        score = jnp.zeros((tk, tq), F32)
        for h0 in range(0, IDX_HEADS, hb):
            x = _dot_nt(ki, qi_ref[0, h0:h0 + hb].reshape(hb * tq, IDX_DIM))
            for h in range(hb):
                score = score + jnp.maximum(x[:, h * tq:(h + 1) * tq], 0.0) * wrow[h0 + h:h0 + h + 1, :]
        key_scr[kt] = jnp.where(key_ids(kt) <= t, _sortable_key(score), INT_MIN)
        return 0

    lax.fori_loop(0, n_kt, score_step, 0)

    def count(pred):
        def step(kt, c):
            hit = jnp.where(pred(key_scr[kt]), 1, 0)
            return c + jnp.sum(hit.reshape(tk // 8, 8, tq), axis=0)
        c = lax.fori_loop(0, n_kt, step, jnp.zeros((8, tq), jnp.int32))
        return jnp.sum(c, axis=0, keepdims=True)

    v = jnp.where(count(lambda k: k >= 0) >= topk, 0, INT_MIN)

    def bit_step(b, v):
        cand = v | (jnp.int32(1) << (30 - b))
        return jnp.where(count(lambda k: k >= cand) >= topk, cand, v)

    v = lax.fori_loop(0, 31, bit_step, v)
    def emit(kt, keep):
        masked = jnp.where(keep & (key_ids(kt) <= t), 0.0, NEG).astype(BF16)
        bias_ref[0, 0, kt] = _dot_nt(eye_ref[...], masked).astype(bias_ref.dtype)

    def emit_no_ties(_):
        def step(kt, c):
            emit(kt, key_scr[kt] >= v)
            return c
        return lax.fori_loop(0, n_kt, step, 0)

    def emit_with_ties(_):
        need = (topk - count(lambda k: k > v)).astype(F32)

        def step(kt, carry):
            key = key_scr[kt]
            eq = key == v
            eqf = jnp.where(eq, 1.0, 0.0)
            before = carry + _dot(tri_ref[...], eqf.astype(BF16)) - eqf
            emit(kt, (key > v) | (eq & (before < need)))
            return carry + jnp.sum(eqf, axis=0, keepdims=True)

        lax.fori_loop(0, n_kt, step, jnp.zeros((1, tq), F32))
        return 0

    over = (count(lambda k: k >= v) > topk) & (v > INT_MIN)
    lax.cond(jnp.max(jnp.where(over, 1, 0)) > 0, emit_with_ties, emit_no_ties, 0)

    def fill_step(kt, _):
        bias_ref[0, 0, kt] = jnp.full((tq, tk), NEG, bias_ref.dtype)
        return 0

    lax.fori_loop(n_kt, n_all, fill_step, 0)


def _dsa_index(qi_heads, ki, wi_t, batch, seq, tq, tk):
    nq = seq // tq
    n_all = seq // tk
    topk = min(DSA_TOPK_MAX, seq // 4)
    tri = jnp.asarray(np.tril(np.ones((tk, tk), np.float32)), dtype=BF16)
    eye = jnp.asarray(np.eye(tq, dtype=np.float32), dtype=BF16)
    return pl.pallas_call(
        functools.partial(_dsa_index_kernel, tq=tq, tk=tk, seq=seq, topk=topk),
        grid=(batch, nq),
        in_specs=[pl.BlockSpec((1, IDX_HEADS, tq, IDX_DIM), lambda b, i: (b, 0, i, 0)),
                  pl.BlockSpec((seq, IDX_DIM), lambda b, i: (b, 0)),
                  pl.BlockSpec((IDX_HEADS, tq), lambda b, i: (0, b * nq + i)),
                  pl.BlockSpec((tq, tq), lambda b, i: (0, 0)),
                  pl.BlockSpec((tk, tk), lambda b, i: (0, 0))],
        out_specs=pl.BlockSpec((1, 1, n_all, tq, tk), lambda b, i: (b, i, 0, 0, 0)),
        out_shape=jax.ShapeDtypeStruct((batch, nq, n_all, tq, tk), BF16),
        scratch_shapes=[pltpu.VMEM((n_all, tk, tq), jnp.int32)],
        compiler_params=_params(("parallel", "arbitrary")),
        name="dsa_index_topk",
    )(qi_heads, ki, wi_t, eye, tri)


def _dsa_attn_kernel(q_ref, k_ref, v_ref, kaux_ref, bias_ref, z_ref, o_ref, s_a, s_b, *, tq, tk, seq):
    g = pl.program_id(1)
    i = pl.program_id(2)
    t0 = i * tq
    rows = HEADS_PER_GROUP * tq
    gsc = _group_slope_scale(g)
    t_rows, slope_rows = _stack_rows(t0, tq, gsc)
    slopes = [(2.0 ** -(j + 1)) * gsc for j in range(HEADS_PER_GROUP)]
    q_aug = _query_aug(q_ref, slopes, jnp.zeros((tq, LANES), F32))

    def scores(kt):
        j0 = pl.multiple_of(kt * tk, tk)
        return _dot_nt(q_aug, jnp.concatenate([k_ref[pl.ds(j0, tk), :], kaux_ref[pl.ds(j0, tk), :]], axis=1))

    def consume(kt, s, carry):
        j0 = pl.multiple_of(kt * tk, tk)
        bias = bias_ref[0, 0, kt].astype(F32)
        s = s + jnp.concatenate([bias] * HEADS_PER_GROUP, axis=0)
        return _flash_step(s, v_ref[pl.ds(j0, tk), :], _tile_origin_bias(slope_rows, t_rows, j0, tk), carry)

    n_kt = _div_pow2(t0 + tq + tk - 1, tk)
    _, l, acc = _pipelined_tiles(n_kt, seq // tk, scores, consume, s_a, s_b, _flash_init(rows))
    o = acc / l
    for j in range(HEADS_PER_GROUP):
        sl = slice(j * HEAD_DIM, (j + 1) * HEAD_DIM)
        o_ref[:, sl] = (o[j * tq:(j + 1) * tq] * z_ref[:, sl]).astype(o_ref.dtype)


def _dsa_attn(qkn, vals, bias, zact, batch, seq, tq, tk):
    t = batch * seq
    nq = seq // tq
    n_all = seq // tk
    gw = HEADS_PER_GROUP * HEAD_DIM
    row = lambda b, g, i: (b * nq + i, g)
    q_blk = (NSA_HEADS + 2 * NSA_KV) * HEAD_DIM // gw
    k_blk = (NSA_HEADS + 2 * NSA_KV + DSA_HEADS)
    return pl.pallas_call(
        functools.partial(_dsa_attn_kernel, tq=tq, tk=tk, seq=seq),
        grid=(batch, DSA_KV, nq),
        in_specs=[pl.BlockSpec((tq, gw), lambda b, g, i: (b * nq + i, q_blk + g)),
                  pl.BlockSpec((seq, HEAD_DIM), lambda b, g, i: (b, k_blk + g)),
                  pl.BlockSpec((seq, HEAD_DIM), lambda b, g, i: (b, 2 * NSA_KV + g)),
                  pl.BlockSpec((seq, LANES), lambda b, g, i: (0, 0)),
                  pl.BlockSpec((1, 1, n_all, tq, tk), lambda b, g, i: (b, i, 0, 0, 0)),
                  pl.BlockSpec((tq, gw), lambda b, g, i: (b * nq + i, NSA_KV + g))],
        out_specs=pl.BlockSpec((tq, gw), row),
        out_shape=jax.ShapeDtypeStruct((t, DSA_HEADS * HEAD_DIM), BF16),
        scratch_shapes=[pltpu.VMEM((HEADS_PER_GROUP * tq, tk), F32)] * 2,
        compiler_params=_params(("parallel", "parallel", "arbitrary")),
        name="dsa_attention",
    )(qkn, qkn, vals, _key_aux(seq, tk), bias, zact)


def _merge_kernel(on_ref, od_ref, sn_ref, sd_ref, x_ref, wun_ref, wud_ref, wout_ref, o_ref):
    y = sn_ref[...] * _dot(on_ref[...], wun_ref[...]) + sd_ref[...] * _dot(od_ref[...], wud_ref[...])
    o_ref[...] = x_ref[...] + _dot(y.astype(BF16), wout_ref[...])


def _merge(o_nsa, o_dsa, gate_act, x, w_up_nsa, w_up_dsa, w_out, tm):
    t, d = x.shape
    const = lambda i: (0, 0)
    return pl.pallas_call(
        _merge_kernel,
        grid=(t // tm,),
        in_specs=[pl.BlockSpec((tm, o_nsa.shape[1]), lambda i: (i, 0)),
                  pl.BlockSpec((tm, o_dsa.shape[1]), lambda i: (i, 0)),
                  pl.BlockSpec((tm, d), lambda i: (i, 0)),
                  pl.BlockSpec((tm, d), lambda i: (i, 1)),
                  pl.BlockSpec((tm, d), lambda i: (i, 0)),
                  pl.BlockSpec(w_up_nsa.shape, const), pl.BlockSpec(w_up_dsa.shape, const),
                  pl.BlockSpec(w_out.shape, const)],
        out_specs=pl.BlockSpec((tm, d), lambda i: (i, 0)),
        out_shape=jax.ShapeDtypeStruct((t, d), F32),
        compiler_params=_params(("parallel",)),
        name="merge_out_proj",
    )(o_nsa, o_dsa, gate_act, gate_act, x, w_up_nsa, w_up_dsa, w_out)


def _ple_kernel(x_ref, p_ref, g_ref, wg_ref, wp_ref, o_ref):
    x = x_ref[...]
    ms = jnp.mean(x * x, axis=-1, keepdims=True)
    r = (x * lax.rsqrt(ms + EPS) * g_ref[...]).astype(BF16)
    gate = _sigmoid(_dot(r, wg_ref[...]))
    o_ref[...] = x + _dot(p_ref[...].astype(BF16), wp_ref[...]) * gate


def _ple(x, p, g, w_gate, w_proj, tm):
    t, d = x.shape
    const = lambda i: (0, 0)
    return pl.pallas_call(
        _ple_kernel,
        grid=(t // tm,),
        in_specs=[pl.BlockSpec((tm, d), lambda i: (i, 0)),
                  pl.BlockSpec((tm, p.shape[1]), lambda i: (i, 0)),
                  pl.BlockSpec((1, d), const),
                  pl.BlockSpec(w_gate.shape, const), pl.BlockSpec(w_proj.shape, const)],
        out_specs=pl.BlockSpec((tm, d), lambda i: (i, 0)),
        out_shape=jax.ShapeDtypeStruct((t, d), F32),
        compiler_params=_params(("parallel",)),
        name="ple_gate",
    )(x, p, g.reshape(1, d), w_gate, w_proj)


def _split_in_proj(w):
    widths = [NSA_HEADS * HEAD_DIM] + [NSA_KV * HEAD_DIM] * 6 + [NSA_HEADS * 3, NSA_HEADS * HEAD_DIM,
              DSA_HEADS * HEAD_DIM, DSA_KV * HEAD_DIM, DSA_KV * HEAD_DIM, IDX_HEADS * IDX_DIM, IDX_DIM,
              IDX_HEADS, DSA_HEADS * HEAD_DIM, D_MODEL, D_MODEL]
    offs = np.concatenate([[0], np.cumsum(widths)])
    assert offs[-1] == w.shape[1]
    (nq, nkc, nvc, nks, nvs, nkw, nvw, ng, nz, dq, dk, dv, iq, ik, iw, dz, mgn, mgd) = [
        w[:, offs[k]:offs[k + 1]] for k in range(len(widths))]
    pad = jnp.zeros((w.shape[0], LANES - IDX_DIM - IDX_HEADS - NSA_HEADS * 3), w.dtype)
    return dict(
        normed=jnp.concatenate([nq, nks, nkw, dq, dk], axis=1).astype(BF16),
        vals=jnp.concatenate([nvs, nvw, dv], axis=1).astype(BF16),
        iq=iq.astype(BF16),
        raw=jnp.concatenate([nkc, nvc, ik, iw, ng, pad], axis=1).astype(BF16),
        zact=jnp.concatenate([nz, dz], axis=1).astype(BF16),
        merge=jnp.concatenate([mgn, mgd], axis=1).astype(BF16),
    )


def _layer(x, p, norm_g, w_in, nsa_q_g, nsa_kc_g, nsa_ks_g, nsa_kw_g, cmp_pe_k, cmp_w1_k, cmp_w2_k,
           cmp_pe_v, cmp_w1_v, cmp_w2_v, dsa_q_g, dsa_k_g, w_up_nsa, w_up_dsa, w_out, ple_norm_g,
           w_ple_gate, w_ple_proj):
    batch, seq, d = x.shape
    t = batch * seq
    assert seq % LANES == 0 and seq >= WINDOW + LANES
    x2 = x.reshape(t, d)
    tm = min(1024, t)

    w = _split_in_proj(w_in)
    h = _rmsnorm(x2, norm_g, min(512, t))

    gains = jnp.concatenate([jnp.tile(nsa_q_g * ATTN_SCALE, NSA_HEADS), jnp.tile(nsa_ks_g, NSA_KV),
                             jnp.tile(nsa_kw_g, NSA_KV), jnp.tile(dsa_q_g * ATTN_SCALE, DSA_HEADS),
                             jnp.tile(dsa_k_g, DSA_KV)])
    qkn = _proj(h, w["normed"], tm, w["normed"].shape[1] // 2, BF16, gain=gains)
    vals = _proj(h, w["vals"], tm, w["vals"].shape[1], BF16)
    qi_heads = _proj_heads(h, w["iq"], min(tm, seq), batch, seq)
    kc_raw, vc_raw, misc = _proj_split(h, w["raw"], tm, (NSA_KV * HEAD_DIM, NSA_KV * HEAD_DIM, LANES))
    zact = _proj(h, w["zact"], tm, 1024, F32, act="silu")
    gate_act = _proj(h, w["merge"], tm, 1024, F32, act="sigmoid")

    ki = misc[:, :IDX_DIM].astype(BF16)
    wi = misc[:, IDX_DIM:IDX_DIM + IDX_HEADS]
    ng = misc[:, IDX_DIM + IDX_HEADS:IDX_DIM + IDX_HEADS + NSA_HEADS * 3]
    per_group = HEADS_PER_GROUP * 3
    gates = jnp.pad(ng.reshape(t, NSA_KV, per_group), ((0, 0), (0, 0), (0, LANES - per_group))).reshape(t, NSA_KV * LANES)

    kc = _compress(kc_raw, cmp_pe_k, cmp_w1_k, cmp_w2_k, nsa_kc_g, batch, seq)
    vc = _compress(vc_raw, cmp_pe_v, cmp_w1_v, cmp_w2_v, None, batch, seq)

    tq = 128
    tk = min(512, seq)
    ocmp, selb = _nsa_cmp(qkn, kc, vc, batch, seq, min(256, seq))
    o_nsa = _nsa_main(qkn, vals, selb, ocmp, gates, zact, batch, seq, tq, tk)

    bias = _dsa_index(qi_heads, ki, wi.T, batch, seq, tq, tk)
    o_dsa = _dsa_attn(qkn, vals, bias, zact, batch, seq, tq, tk)

    x1 = _merge(o_nsa, o_dsa, gate_act, x2, w_up_nsa.astype(BF16), w_up_dsa.astype(BF16), w_out.astype(BF16),
                min(256, t))
    x3 = _ple(x1, p.reshape(t, PLE_DIM), ple_norm_g, w_ple_gate.astype(BF16), w_ple_proj.astype(BF16), min(256, t))
    return x3.reshape(batch, seq, d)


def kernel(x, p, norm_g, w_in, nsa_q_g, nsa_kc_g, nsa_ks_g, nsa_kw_g, cmp_pe_k, cmp_w1_k, cmp_w2_k, cmp_pe_v, cmp_w1_v, cmp_w2_v, dsa_q_g, dsa_k_g, w_up_nsa, w_up_dsa, w_out, ple_norm_g, w_ple_gate, w_ple_proj):
    depth = w_in.shape[0]
    for i in range(depth):
        x = _layer(x, p[i], norm_g[i], w_in[i], nsa_q_g[i], nsa_kc_g[i], nsa_ks_g[i], nsa_kw_g[i],
                   cmp_pe_k[i], cmp_w1_k[i], cmp_w2_k[i], cmp_pe_v[i], cmp_w1_v[i], cmp_w2_v[i],
                   dsa_q_g[i], dsa_k_g[i], w_up_nsa[i], w_up_dsa[i], w_out[i], ple_norm_g[i],
                   w_ple_gate[i], w_ple_proj[i])
    return x
```

```python
import functools

import numpy as np
import jax
import jax.numpy as jnp
from jax import lax
from jax.experimental import pallas as pl
from jax.experimental.pallas import tpu as pltpu

D_MODEL = 2048
HEAD_DIM = 128
NSA_HEADS = 8
NSA_KV = 2
DSA_HEADS = 8
DSA_KV = 2
HEADS_PER_GROUP = 4
CMP_LEN = 32
CMP_STRIDE = 16
CMP_HIDDEN = 256
SEL_BLOCK = 64
SEL_TOPN = 16
WINDOW = 512
IDX_HEADS = 16
IDX_DIM = 64
DSA_TOPK_MAX = 256
PLE_DIM = 256
EPS = 1e-6
NEG = -1e30
FORCE_SCORE = 1e4
ATTN_SCALE = HEAD_DIM ** -0.5
LANES = 128
INT_MIN = -(2 ** 31)

VMEM_LIMIT_BYTES = 56 * 1024 * 1024

F32 = jnp.float32
BF16 = jnp.bfloat16


def _params(semantics):
    return pltpu.CompilerParams(dimension_semantics=semantics, vmem_limit_bytes=VMEM_LIMIT_BYTES)


def _sigmoid(x):
    return 1.0 / (1.0 + jnp.exp(-x))


def _dot(a, b):
    return jnp.dot(a, b, preferred_element_type=F32)


def _dot_nt(a, b):
    return lax.dot_general(a, b, (((1,), (1,)), ((), ())), preferred_element_type=F32)


def _log2(n):
    assert n > 0 and n & (n - 1) == 0, n
    return n.bit_length() - 1


def _div_pow2(x, n):
    return lax.shift_right_logical(x, jnp.int32(_log2(n)))


def _group_slope_scale(g):
    return jnp.where(g == 0, 1.0, 2.0 ** -HEADS_PER_GROUP).astype(F32)


def _rmsnorm_kernel(x_ref, g_ref, o_ref):
    x = x_ref[...]
    ms = jnp.mean(x * x, axis=-1, keepdims=True)
    o_ref[...] = (x * lax.rsqrt(ms + EPS) * g_ref[...]).astype(o_ref.dtype)


def _rmsnorm(x, g, tm):
    t, d = x.shape
    return pl.pallas_call(
        _rmsnorm_kernel,
        grid=(t // tm,),
        in_specs=[pl.BlockSpec((tm, d), lambda i: (i, 0)), pl.BlockSpec((1, d), lambda i: (0, 0))],
        out_specs=pl.BlockSpec((tm, d), lambda i: (i, 0)),
        out_shape=jax.ShapeDtypeStruct((t, d), BF16),
        compiler_params=_params(("parallel",)),
        name="in_rmsnorm",
    )(x, g.reshape(1, d))


def _proj_headnorm_kernel(h_ref, w_ref, g_ref, o_ref):
    y = _dot(h_ref[...], w_ref[...])
    for hd in range(y.shape[1] // HEAD_DIM):
        sl = slice(hd * HEAD_DIM, (hd + 1) * HEAD_DIM)
        yh = y[:, sl]
        ms = jnp.mean(yh * yh, axis=-1, keepdims=True)
        o_ref[:, sl] = (yh * lax.rsqrt(ms + EPS) * g_ref[:, sl]).astype(o_ref.dtype)


def _proj_act_kernel(h_ref, w_ref, o_ref, *, act):
    y = _dot(h_ref[...], w_ref[...])
    if act == "silu":
        y = y * _sigmoid(y)
    elif act == "sigmoid":
        y = _sigmoid(y)
    o_ref[...] = y.astype(o_ref.dtype)


def _proj_split_kernel(h_ref, w_ref, *o_refs):
    y = _dot(h_ref[...], w_ref[...])
    off = 0
    for o_ref in o_refs:
        n = o_ref.shape[1]
        o_ref[...] = y[:, off:off + n].astype(o_ref.dtype)
        off += n


def _proj(h, w, tm, tn, out_dtype, act=None, gain=None):
    t, d = h.shape
    n = w.shape[1]
    in_specs = [pl.BlockSpec((tm, d), lambda i, j: (i, 0)), pl.BlockSpec((d, tn), lambda i, j: (0, j))]
    args = [h, w]
    if gain is not None:
        body = _proj_headnorm_kernel
        in_specs.append(pl.BlockSpec((1, tn), lambda i, j: (0, j)))
        args.append(gain.reshape(1, n))
    else:
        body = functools.partial(_proj_act_kernel, act=act)
    return pl.pallas_call(
        body,
        grid=(t // tm, n // tn),
        in_specs=in_specs,
        out_specs=pl.BlockSpec((tm, tn), lambda i, j: (i, j)),
        out_shape=jax.ShapeDtypeStruct((t, n), out_dtype),
        compiler_params=_params(("parallel", "arbitrary")),
        name="in_proj_" + ("headnorm" if gain is not None else str(act)),
    )(*args)


def _proj_heads_kernel(h_ref, w_ref, o_ref):
    y = _dot(h_ref[...], w_ref[...])
    for hd in range(o_ref.shape[1]):
        o_ref[0, hd] = y[:, hd * IDX_DIM:(hd + 1) * IDX_DIM].astype(o_ref.dtype)


def _proj_heads(h, w, tm, batch, seq):
    t, d = h.shape
    per_batch = seq // tm
    return pl.pallas_call(
        _proj_heads_kernel,
        grid=(t // tm,),
        in_specs=[pl.BlockSpec((tm, d), lambda i: (i, 0)), pl.BlockSpec(w.shape, lambda i: (0, 0))],
        out_specs=pl.BlockSpec((1, IDX_HEADS, tm, IDX_DIM), lambda i: (i // per_batch, 0, i % per_batch, 0)),
        out_shape=jax.ShapeDtypeStruct((batch, IDX_HEADS, seq, IDX_DIM), BF16),
        compiler_params=_params(("parallel",)),
        name="in_proj_heads",
    )(h, w)


def _proj_split(h, w, tm, widths):
    t, d = h.shape
    n = w.shape[1]
    return pl.pallas_call(
        _proj_split_kernel,
        grid=(t // tm,),
        in_specs=[pl.BlockSpec((tm, d), lambda i: (i, 0)), pl.BlockSpec((d, n), lambda i: (0, 0))],
        out_specs=[pl.BlockSpec((tm, wd), lambda i: (i, 0)) for wd in widths],
        out_shape=[jax.ShapeDtypeStruct((t, wd), F32) for wd in widths],
        compiler_params=_params(("parallel",)),
        name="in_proj_split",
    )(h, w)


def _compress_kernel(x_ref, pea_ref, peb_ref, wa_ref, wb_ref, w2_ref, g_ref, o_ref, *, normalize):
    x = x_ref[...]
    nch = x.shape[0]
    ha = _dot((x + pea_ref[...]).astype(BF16), wa_ref[...])
    hb = _dot((x + peb_ref[...]).astype(BF16), wb_ref[...])
    hid = ha + pltpu.roll(hb, nch - 1, axis=0)
    act = (hid * _sigmoid(hid)).astype(BF16)
    for g in range(NSA_KV):
        o = _dot(act[:, g * CMP_HIDDEN:(g + 1) * CMP_HIDDEN], w2_ref[...])
        if normalize:
            ms = jnp.mean(o * o, axis=-1, keepdims=True)
            o = o * lax.rsqrt(ms + EPS) * g_ref[...]
        o_ref[0, :, g * HEAD_DIM:(g + 1) * HEAD_DIM] = o.astype(o_ref.dtype)


def _expand_compress_params(pe, w1):
    half = CMP_LEN // 2
    w = w1.reshape(CMP_LEN, HEAD_DIM, CMP_HIDDEN)
    eye = jnp.eye(NSA_KV, dtype=w1.dtype)

    def expand_w(wh):
        return jnp.einsum("ldj,hg->lhdgj", wh, eye).reshape(half * NSA_KV * HEAD_DIM, NSA_KV * CMP_HIDDEN)

    def expand_pe(ph):
        return jnp.broadcast_to(ph[:, None, :], (half, NSA_KV, HEAD_DIM)).reshape(1, half * NSA_KV * HEAD_DIM)

    return (expand_pe(pe[:half]), expand_pe(pe[half:]),
            expand_w(w[:half]).astype(BF16), expand_w(w[half:]).astype(BF16))


def _compress(raw, pe, w1, w2, gain, batch, seq):
    nch = seq // CMP_STRIDE
    width = CMP_STRIDE * NSA_KV * HEAD_DIM
    x = raw.reshape(batch * nch, width)
    pea, peb, wa, wb = _expand_compress_params(pe, w1)
    normalize = gain is not None
    g = (gain if normalize else jnp.ones((HEAD_DIM,), F32)).reshape(1, HEAD_DIM)
    const = lambda b: (0, 0)
    return pl.pallas_call(
        functools.partial(_compress_kernel, normalize=normalize),
        grid=(batch,),
        in_specs=[pl.BlockSpec((nch, width), lambda b: (b, 0)),
                  pl.BlockSpec((1, width), const), pl.BlockSpec((1, width), const),
                  pl.BlockSpec(wa.shape, const), pl.BlockSpec(wb.shape, const),
                  pl.BlockSpec((CMP_HIDDEN, HEAD_DIM), const), pl.BlockSpec((1, HEAD_DIM), const)],
        out_specs=pl.BlockSpec((1, nch, NSA_KV * HEAD_DIM), lambda b: (b, 0, 0)),
        out_shape=jax.ShapeDtypeStruct((batch, nch, NSA_KV * HEAD_DIM), BF16),
        compiler_params=_params(("parallel",)),
        name="nsa_compress",
    )(x, pea, peb, wa, wb, w2.astype(BF16), g)


def _cmp_to_sel_matrix(nch, n_sel_pad):
    n_cmp = nch - 1
    c0 = np.arange(nch)[:, None] * CMP_STRIDE
    s0 = np.arange(n_sel_pad)[None, :] * SEL_BLOCK
    ov = np.clip(np.minimum(c0 + CMP_LEN, s0 + SEL_BLOCK) - np.maximum(c0, s0), 0, None) / CMP_LEN
    ov[n_cmp:] = 0.0
    return jnp.asarray(ov.T, dtype=BF16)


def _nsa_cmp_kernel(q_ref, kc_ref, vc_ref, mt_ref, eye_ref, tile_of_ref, ocmp_ref, selb_ref, flags_ref,
                    *, tq, nch, n_sel):
    g = pl.program_id(1)
    i = pl.program_id(2)
    t = i * tq + lax.broadcasted_iota(jnp.int32, (tq, 1), 0)
    c_end = lax.broadcasted_iota(jnp.int32, (1, nch), 1) * CMP_STRIDE + (CMP_LEN - 1)
    vis = c_end <= t
    distc = t.astype(F32) - (c_end.astype(F32) - (CMP_LEN - 1) / 2.0)
    live = (t >= CMP_LEN - 1).astype(F32)
    kc = kc_ref[0]
    vc = vc_ref[0]
    gsc = _group_slope_scale(g)
    psum = jnp.zeros((tq, nch), F32)
    for j in range(HEADS_PER_GROUP):
        slope = (2.0 ** -(j + 1)) * gsc
        sl = slice(j * HEAD_DIM, (j + 1) * HEAD_DIM)
        s = _dot_nt(q_ref[:, sl], kc) - slope * distc
        s = jnp.where(vis, s, NEG)
        e = jnp.exp(s - jnp.max(s, axis=-1, keepdims=True))
        p = e * (live / jnp.sum(e, axis=-1, keepdims=True))
        psum = psum + p
        ocmp_ref[:, sl] = _dot(p.astype(BF16), vc)

    imp = _dot_nt(mt_ref[...], psum.astype(BF16))
    blk = lax.broadcasted_iota(jnp.int32, (LANES, 1), 0)
    blk_t = _div_pow2(i * tq + lax.broadcasted_iota(jnp.int32, (1, tq), 1), SEL_BLOCK)
    valid = blk <= blk_t
    forced = (blk == 0) | (blk == blk_t) | (blk == blk_t - 1)
    vals = jnp.where(valid & jnp.logical_not(forced) & (blk < n_sel), imp, -1.0)
    sel = valid & forced
    for _ in range(SEL_TOPN - 3):
        mx = jnp.max(vals, axis=0, keepdims=True)
        first = jnp.min(jnp.where(vals == mx, blk, LANES), axis=0, keepdims=True)
        pick = blk == first
        sel = sel | pick
        vals = jnp.where(pick, -3.0, vals)
    sel = sel & valid
    masked = jnp.where(sel, 0.0, NEG).astype(BF16)
    selb_ref[...] = _dot_nt(eye_ref[...], masked).astype(selb_ref.dtype)

    t_lane = i * tq + lax.broadcasted_iota(jnp.int32, (1, tq), 1)
    early = blk < (SEGMENT // SEL_BLOCK) * _div_pow2(t_lane, SEGMENT)
    per_query = _dot(tile_of_ref[...], jnp.where(sel & early, 1.0, 0.0).astype(BF16))
    out_lane = lax.broadcasted_iota(jnp.int32, flags_ref.shape[1:], 1)
    flags = jnp.zeros(flags_ref.shape[1:], jnp.int32)
    for h in range(tq // SEGMENT):
        hit = jnp.max(per_query[:, h * SEGMENT:(h + 1) * SEGMENT], axis=1, keepdims=True) > 0.0
        flags = jnp.where((out_lane == h) & hit, 1, flags)
    flags_ref[0] = flags


SEGMENT = 128
TILE_ROWS = 16


def _nsa_cmp(qkn, kc, vc, batch, seq, tq, tk):
    t = batch * seq
    nch = seq // CMP_STRIDE
    n_sel = seq // SEL_BLOCK
    n_all = seq // tk
    assert n_sel <= LANES and n_all <= TILE_ROWS and tq % SEGMENT == 0
    nq = seq // tq
    gw = HEADS_PER_GROUP * HEAD_DIM
    mt = _cmp_to_sel_matrix(nch, LANES)
    eye = jnp.asarray(np.eye(tq, dtype=np.float32), dtype=BF16)
    tile_of = jnp.asarray(np.arange(LANES)[None, :] // (tk // SEL_BLOCK) == np.arange(TILE_ROWS)[:, None], dtype=BF16)
    row = lambda b, g, i: (b * nq + i, g)
    ocmp, selb, flags = pl.pallas_call(
        functools.partial(_nsa_cmp_kernel, tq=tq, nch=nch, n_sel=n_sel),
        grid=(batch, NSA_KV, nq),
        in_specs=[pl.BlockSpec((tq, gw), row),
                  pl.BlockSpec((1, nch, HEAD_DIM), lambda b, g, i: (b, 0, g)),
                  pl.BlockSpec((1, nch, HEAD_DIM), lambda b, g, i: (b, 0, g)),
                  pl.BlockSpec((LANES, nch), lambda b, g, i: (0, 0)),
                  pl.BlockSpec((tq, tq), lambda b, g, i: (0, 0)),
                  pl.BlockSpec((TILE_ROWS, LANES), lambda b, g, i: (0, 0))],
        out_specs=[pl.BlockSpec((tq, gw), row), pl.BlockSpec((tq, LANES), row),
                   pl.BlockSpec((1, TILE_ROWS, LANES), lambda b, g, i: ((b * NSA_KV + g) * nq + i, 0, 0))],
        out_shape=[jax.ShapeDtypeStruct((t, NSA_HEADS * HEAD_DIM), F32),
                   jax.ShapeDtypeStruct((t, NSA_KV * LANES), BF16),
                   jax.ShapeDtypeStruct((batch * NSA_KV * nq, TILE_ROWS, LANES), jnp.int32)],
        compiler_params=_params(("parallel", "parallel", "parallel")),
        name="nsa_cmp_select",
    )(qkn, kc, vc, mt, eye, tile_of)
    segs = tq // SEGMENT
    tile_flags = flags[:, :n_all, :segs].transpose(0, 2, 1).reshape(batch, NSA_KV, nq * segs, n_all)
    return ocmp, selb, tile_flags


def _stack_rows(t0, tq):
    r = lax.broadcasted_iota(jnp.int32, (HEADS_PER_GROUP * tq, 1), 0)
    return t0 + (r & (tq - 1))


def _key_aux_blocks(seq):
    j = np.arange(seq)
    aux = (j[:, None] // SEL_BLOCK == np.arange(LANES)[None, :]).astype(np.float32)
    aux[:, 0] = j % SEL_BLOCK
    return jnp.asarray(aux, dtype=BF16)


_POS_SPLIT = 256


def _key_aux_position(seq):
    assert seq <= _POS_SPLIT * _POS_SPLIT
    j = np.arange(seq)
    aux = np.zeros((seq, LANES), np.float32)
    aux[:, 0] = j % _POS_SPLIT
    aux[:, 1] = j - j % _POS_SPLIT
    return jnp.asarray(aux, dtype=BF16)


def _query_aug(q_ref, aux_cols):
    parts = [jnp.concatenate([q_ref[:, j * HEAD_DIM:(j + 1) * HEAD_DIM], aux_cols[j].astype(BF16)], axis=1)
             for j in range(HEADS_PER_GROUP)]
    return jnp.concatenate(parts, axis=0)


def _flash_step(s, v, carry):
    m, l, acc = carry
    m_new = jnp.maximum(m, jnp.max(s, axis=-1, keepdims=True))
    alpha = jnp.exp(m - m_new)
    p = jnp.exp(s - m_new).astype(BF16)
    pv = _dot(p, jnp.concatenate([v, jnp.ones_like(v)], axis=1))
    l = alpha * l + pv[:, HEAD_DIM:HEAD_DIM + 1]
    acc = alpha * acc + pv[:, :HEAD_DIM]
    return m_new, l, acc


def _flash_init(rows):
    return (jnp.full((rows, 1), NEG, F32), jnp.zeros((rows, 1), F32), jnp.zeros((rows, HEAD_DIM), F32))


def _pipelined_tiles(n, n_all, scores_fn, consume_fn, s_a, s_b, carry):
    s_a[...] = scores_fn(0)

    def pair(u, carry):
        k = 2 * u
        s_b[...] = scores_fn(k + 1)
        carry = consume_fn(k, s_a[...], carry)
        s_a[...] = scores_fn(jnp.minimum(k + 2, n_all - 1))
        return consume_fn(k + 1, s_b[...], carry)

    carry = lax.fori_loop(0, lax.shift_right_logical(n, 1), pair, carry)
    return lax.cond((n & 1) == 1, lambda c: consume_fn(n - 1, s_a[...], c), lambda c: c, carry)


def _nsa_main_kernel(tiles_ref, count_ref, q_ref, ks_ref, vs_ref, kw_ref, vw_ref, kaux_ref, kpos_ref, selb_ref,
                     ocmp_ref, gate_ref, z_ref, o_ref, s_a, s_b, *, tq, tk, seq):
    b = pl.program_id(0)
    g = pl.program_id(1)
    i = pl.program_id(2)
    n_all = seq // tk
    t0 = pl.multiple_of(i * tq, tq)
    rows = HEADS_PER_GROUP * tq
    gsc = _group_slope_scale(g)
    t_rows = _stack_rows(t0, tq)
    slopes = [(2.0 ** -(j + 1)) * gsc for j in range(HEADS_PER_GROUP)]

    lane = lax.broadcasted_iota(jnp.int32, (tq, LANES), 1)
    chosen = (selb_ref[...].astype(F32) > -1.0) & (lane < _div_pow2(t0, SEL_BLOCK))
    block_start = (lane * SEL_BLOCK).astype(F32)
    q_aug = _query_aug(q_ref, [jnp.where(lane == 0, sl, jnp.where(chosen, sl * block_start, NEG)) for sl in slopes])

    slot = (b * NSA_KV + g) * (seq // tq) + i
    base = slot * n_all

    def scores(u):
        j0 = pl.multiple_of(tiles_ref[base + u] * tk, tk)
        return _dot_nt(q_aug, jnp.concatenate([ks_ref[pl.ds(j0, tk), :], kaux_ref[pl.ds(j0, tk), :]], axis=1))

    def consume(u, s, carry):
        j0 = pl.multiple_of(tiles_ref[base + u] * tk, tk)
        return _flash_step(s, vs_ref[pl.ds(j0, tk), :], carry)

    carry = _pipelined_tiles(count_ref[slot], n_all, scores, consume, s_a, s_b, _flash_init(rows))

    q_pos = _query_aug(q_ref, [jnp.where(lane < 2, sl, 0.0) for sl in slopes])

    j_own = t0 + lax.broadcasted_iota(jnp.int32, (1, tq), 1)
    s = _dot_nt(q_pos, jnp.concatenate([ks_ref[pl.ds(t0, tq), :], kpos_ref[pl.ds(t0, tq), :]], axis=1))
    s = jnp.where(j_own <= t_rows, s, NEG)
    _, l_s, acc_s = _flash_step(s, vs_ref[pl.ds(t0, tq), :], carry)
    o_slc = acc_s / l_s

    wk = WINDOW + tq
    start = pl.multiple_of(jnp.maximum(t0 - WINDOW, 0), tq)
    j_win = start + lax.broadcasted_iota(jnp.int32, (1, wk), 1)
    s = _dot_nt(q_pos, jnp.concatenate([kw_ref[pl.ds(start, wk), :], kpos_ref[pl.ds(start, wk), :]], axis=1))
    s = jnp.where((j_win <= t_rows) & (j_win > t_rows - WINDOW), s, NEG)
    e = jnp.exp(s - jnp.max(s, axis=-1, keepdims=True))
    o_win = _dot(e.astype(BF16), vw_ref[pl.ds(start, wk), :]) / jnp.sum(e, axis=-1, keepdims=True)

    for j in range(HEADS_PER_GROUP):
        sl = slice(j * HEAD_DIM, (j + 1) * HEAD_DIM)
        rs = slice(j * tq, (j + 1) * tq)
        gates = [_sigmoid(gate_ref[:, 3 * j + c:3 * j + c + 1]) for c in range(3)]
        o = gates[0] * ocmp_ref[:, sl] + gates[1] * o_slc[rs] + gates[2] * o_win[rs]
        o_ref[:, sl] = (o * z_ref[:, sl]).astype(o_ref.dtype)


def _nsa_main(qkn, vals, selb, tile_flags, ocmp, gates, zact, batch, seq, tq, tk):
    t = batch * seq
    nq = seq // tq
    n_all = seq // tk
    gw = HEADS_PER_GROUP * HEAD_DIM
    flags = tile_flags.reshape(batch * NSA_KV * nq, n_all)
    tiles = jnp.argsort(1 - flags, axis=-1, stable=True).astype(jnp.int32).reshape(-1)
    counts = jnp.sum(flags, axis=-1).astype(jnp.int32)
    row = lambda b, g, i, *_: (b * nq + i, g)
    ks_blk = (NSA_HEADS * HEAD_DIM) // HEAD_DIM
    kw_blk = ks_blk + NSA_KV
    grid_spec = pltpu.PrefetchScalarGridSpec(
        num_scalar_prefetch=2,
        grid=(batch, NSA_KV, nq),
        in_specs=[pl.BlockSpec((tq, gw), row),
                  pl.BlockSpec((seq, HEAD_DIM), lambda b, g, i, *_: (b, ks_blk + g)),
                  pl.BlockSpec((seq, HEAD_DIM), lambda b, g, i, *_: (b, g)),
                  pl.BlockSpec((seq, HEAD_DIM), lambda b, g, i, *_: (b, kw_blk + g)),
                  pl.BlockSpec((seq, HEAD_DIM), lambda b, g, i, *_: (b, NSA_KV + g)),
                  pl.BlockSpec((seq, LANES), lambda b, g, i, *_: (0, 0)),
                  pl.BlockSpec((seq, LANES), lambda b, g, i, *_: (0, 0)),
                  pl.BlockSpec((tq, LANES), row),
                  pl.BlockSpec((tq, gw), row),
                  pl.BlockSpec((tq, LANES), row),
                  pl.BlockSpec((tq, gw), row)],
        out_specs=pl.BlockSpec((tq, gw), row),
        scratch_shapes=[pltpu.VMEM((HEADS_PER_GROUP * tq, tk), F32)] * 2,
    )
    return pl.pallas_call(
        functools.partial(_nsa_main_kernel, tq=tq, tk=tk, seq=seq),
        grid_spec=grid_spec,
        out_shape=jax.ShapeDtypeStruct((t, NSA_HEADS * HEAD_DIM), BF16),
        compiler_params=_params(("parallel", "parallel", "arbitrary")),
        name="nsa_select_window",
    )(tiles, counts, qkn, qkn, vals, qkn, vals, _key_aux_blocks(seq), _key_aux_position(seq), selb, ocmp, gates,
      zact)


def _sortable_key(x):
    bits = pltpu.bitcast(x, jnp.int32)
    bits = jnp.where(bits == INT_MIN, 0, bits)
    return jnp.where(bits < 0, bits ^ 0x7FFFFFFF, bits)


def _dsa_index_kernel(qi_ref, ki_ref, wi_ref, eye_ref, tri_ref, bias_ref, key_scr, *, tq, tk, seq, topk):
    i = pl.program_id(1)
    t0 = i * tq
    n_all = seq // tk
    n_kt = _div_pow2(t0 + tq + tk - 1, tk)
    t = t0 + lax.broadcasted_iota(jnp.int32, (1, tq), 1)
    wrow = wi_ref[...] * (IDX_HEADS ** -0.5 * IDX_DIM ** -0.5)
    hb = 4

    def key_ids(kt):
        return kt * tk + lax.broadcasted_iota(jnp.int32, (tk, 1), 0)

    def score_tile(kt):
        j0 = pl.multiple_of(kt * tk, tk)
        ki = ki_ref[pl.ds(j0, tk), :]
        score = jnp.zeros((tk, tq), F32)
        for h0 in range(0, IDX_HEADS, hb):
            x = _dot_nt(ki, qi_ref[0, h0:h0 + hb].reshape(hb * tq, IDX_DIM))
            for h in range(hb):
                score = score + jnp.maximum(x[:, h * tq:(h + 1) * tq], 0.0) * wrow[h0 + h:h0 + h + 1, :]
        key_scr[kt] = jnp.where(key_ids(kt) <= t, _sortable_key(score), INT_MIN)

    n_pairs = lax.shift_right_logical(n_kt + 1, 1)

    def score_step(u, _):
        score_tile(2 * u)
        score_tile(2 * u + 1)
        return 0

    lax.fori_loop(0, n_pairs, score_step, 0)

    def count(pred):
        def step(u, c):
            for kt in (2 * u, 2 * u + 1):
                hit = jnp.where(pred(key_scr[kt]), 1, 0)
                c = c + jnp.sum(hit.reshape(tk // 8, 8, tq), axis=0)
            return c
        c = lax.fori_loop(0, n_pairs, step, jnp.zeros((8, tq), jnp.int32))
        return jnp.sum(c, axis=0, keepdims=True)

    v = jnp.where(count(lambda k: k >= 0) >= topk, 0, INT_MIN)

    def bit_step(b, v):
        cand = v | (jnp.int32(1) << (30 - b))
        return jnp.where(count(lambda k: k >= cand) >= topk, cand, v)

    v = lax.fori_loop(0, 31, bit_step, v)
    def emit(kt, keep):
        masked = jnp.where(keep & (key_ids(kt) <= t), 0.0, NEG).astype(BF16)
        bias_ref[0, 0, kt] = _dot_nt(eye_ref[...], masked).astype(bias_ref.dtype)

    def emit_no_ties(_):
        def step(kt, c):
            emit(kt, key_scr[kt] >= v)
            return c
        return lax.fori_loop(0, n_kt, step, 0)

    def emit_with_ties(_):
        need = (topk - count(lambda k: k > v)).astype(F32)

        def step(kt, carry):
            key = key_scr[kt]
            eq = key == v
            eqf = jnp.where(eq, 1.0, 0.0)
            before = carry + _dot(tri_ref[...], eqf.astype(BF16)) - eqf
            emit(kt, (key > v) | (eq & (before < need)))
            return carry + jnp.sum(eqf, axis=0, keepdims=True)

        lax.fori_loop(0, n_kt, step, jnp.zeros((1, tq), F32))
        return 0

    over = (count(lambda k: k >= v) > topk) & (v > INT_MIN)
    lax.cond(jnp.max(jnp.where(over, 1, 0)) > 0, emit_with_ties, emit_no_ties, 0)

    def fill_step(kt, _):
        bias_ref[0, 0, kt] = jnp.full((tq, tk), NEG, bias_ref.dtype)
        return 0

    lax.fori_loop(n_kt, n_all, fill_step, 0)


def _dsa_index(qi_heads, ki, wi_t, batch, seq, tq, tk):
    nq = seq // tq
    n_all = seq // tk
    topk = min(DSA_TOPK_MAX, seq // 4)
    tri = jnp.asarray(np.tril(np.ones((tk, tk), np.float32)), dtype=BF16)
    eye = jnp.asarray(np.eye(tq, dtype=np.float32), dtype=BF16)
    return pl.pallas_call(
        functools.partial(_dsa_index_kernel, tq=tq, tk=tk, seq=seq, topk=topk),
        grid=(batch, nq),
        in_specs=[pl.BlockSpec((1, IDX_HEADS, tq, IDX_DIM), lambda b, i: (b, 0, i, 0)),
                  pl.BlockSpec((seq, IDX_DIM), lambda b, i: (b, 0)),
                  pl.BlockSpec((IDX_HEADS, tq), lambda b, i: (0, b * nq + i)),
                  pl.BlockSpec((tq, tq), lambda b, i: (0, 0)),
                  pl.BlockSpec((tk, tk), lambda b, i: (0, 0))],
        out_specs=pl.BlockSpec((1, 1, n_all, tq, tk), lambda b, i: (b, i, 0, 0, 0)),
        out_shape=jax.ShapeDtypeStruct((batch, nq, n_all, tq, tk), BF16),
        scratch_shapes=[pltpu.VMEM((n_all, tk, tq), jnp.int32)],
        compiler_params=_params(("parallel", "arbitrary")),
        name="dsa_index_topk",
    )(qi_heads, ki, wi_t, eye, tri)


def _dsa_attn_kernel(q_ref, k_ref, v_ref, kaux_ref, bias_ref, z_ref, o_ref, s_a, s_b, *, tq, tk, seq):
    g = pl.program_id(1)
    i = pl.program_id(2)
    t0 = i * tq
    rows = HEADS_PER_GROUP * tq
    gsc = _group_slope_scale(g)
    slopes = [(2.0 ** -(j + 1)) * gsc for j in range(HEADS_PER_GROUP)]
    lane = lax.broadcasted_iota(jnp.int32, (tq, LANES), 1)
    q_aug = _query_aug(q_ref, [jnp.where(lane < 2, sl, 0.0) for sl in slopes])

    def scores(kt):
        j0 = pl.multiple_of(kt * tk, tk)
        return _dot_nt(q_aug, jnp.concatenate([k_ref[pl.ds(j0, tk), :], kaux_ref[pl.ds(j0, tk), :]], axis=1))

    def consume(kt, s, carry):
        j0 = pl.multiple_of(kt * tk, tk)
        bias = bias_ref[0, 0, kt].astype(F32)
        s = s + jnp.concatenate([bias] * HEADS_PER_GROUP, axis=0)
        return _flash_step(s, v_ref[pl.ds(j0, tk), :], carry)

    n_kt = _div_pow2(t0 + tq + tk - 1, tk)
    _, l, acc = _pipelined_tiles(n_kt, seq // tk, scores, consume, s_a, s_b, _flash_init(rows))
    o = acc / l
    for j in range(HEADS_PER_GROUP):
        sl = slice(j * HEAD_DIM, (j + 1) * HEAD_DIM)
        o_ref[:, sl] = (o[j * tq:(j + 1) * tq] * z_ref[:, sl]).astype(o_ref.dtype)


def _dsa_attn(qkn, vals, bias, zact, batch, seq, tq, tk):
    t = batch * seq
    nq = seq // tq
    n_all = seq // tk
    gw = HEADS_PER_GROUP * HEAD_DIM
    row = lambda b, g, i: (b * nq + i, g)
    q_blk = (NSA_HEADS + 2 * NSA_KV) * HEAD_DIM // gw
    k_blk = (NSA_HEADS + 2 * NSA_KV + DSA_HEADS)
    return pl.pallas_call(
        functools.partial(_dsa_attn_kernel, tq=tq, tk=tk, seq=seq),
        grid=(batch, DSA_KV, nq),
        in_specs=[pl.BlockSpec((tq, gw), lambda b, g, i: (b * nq + i, q_blk + g)),
                  pl.BlockSpec((seq, HEAD_DIM), lambda b, g, i: (b, k_blk + g)),
                  pl.BlockSpec((seq, HEAD_DIM), lambda b, g, i: (b, 2 * NSA_KV + g)),
                  pl.BlockSpec((seq, LANES), lambda b, g, i: (0, 0)),
                  pl.BlockSpec((1, 1, n_all, tq, tk), lambda b, g, i: (b, i, 0, 0, 0)),
                  pl.BlockSpec((tq, gw), lambda b, g, i: (b * nq + i, NSA_KV + g))],
        out_specs=pl.BlockSpec((tq, gw), row),
        out_shape=jax.ShapeDtypeStruct((t, DSA_HEADS * HEAD_DIM), BF16),
        scratch_shapes=[pltpu.VMEM((HEADS_PER_GROUP * tq, tk), F32)] * 2,
        compiler_params=_params(("parallel", "parallel", "arbitrary")),
        name="dsa_attention",
    )(qkn, qkn, vals, _key_aux_position(seq), bias, zact)


def _merge_kernel(on_ref, od_ref, sn_ref, sd_ref, x_ref, wun_ref, wud_ref, wout_ref, o_ref):
    y = sn_ref[...] * _dot(on_ref[...], wun_ref[...]) + sd_ref[...] * _dot(od_ref[...], wud_ref[...])
    o_ref[...] = x_ref[...] + _dot(y.astype(BF16), wout_ref[...])


def _merge(o_nsa, o_dsa, gate_act, x, w_up_nsa, w_up_dsa, w_out, tm):
    t, d = x.shape
    const = lambda i: (0, 0)
    return pl.pallas_call(
        _merge_kernel,
        grid=(t // tm,),
        in_specs=[pl.BlockSpec((tm, o_nsa.shape[1]), lambda i: (i, 0)),
                  pl.BlockSpec((tm, o_dsa.shape[1]), lambda i: (i, 0)),
                  pl.BlockSpec((tm, d), lambda i: (i, 0)),
                  pl.BlockSpec((tm, d), lambda i: (i, 1)),
                  pl.BlockSpec((tm, d), lambda i: (i, 0)),
                  pl.BlockSpec(w_up_nsa.shape, const), pl.BlockSpec(w_up_dsa.shape, const),
                  pl.BlockSpec(w_out.shape, const)],
        out_specs=pl.BlockSpec((tm, d), lambda i: (i, 0)),
        out_shape=jax.ShapeDtypeStruct((t, d), F32),
        compiler_params=_params(("parallel",)),
        name="merge_out_proj",
    )(o_nsa, o_dsa, gate_act, gate_act, x, w_up_nsa, w_up_dsa, w_out)


def _ple_kernel(x_ref, p_ref, g_ref, wg_ref, wp_ref, o_ref):
    x = x_ref[...]
    ms = jnp.mean(x * x, axis=-1, keepdims=True)
    r = (x * lax.rsqrt(ms + EPS) * g_ref[...]).astype(BF16)
    gate = _sigmoid(_dot(r, wg_ref[...]))
    o_ref[...] = x + _dot(p_ref[...].astype(BF16), wp_ref[...]) * gate


def _ple(x, p, g, w_gate, w_proj, tm):
    t, d = x.shape
    const = lambda i: (0, 0)
    return pl.pallas_call(
        _ple_kernel,
        grid=(t // tm,),
        in_specs=[pl.BlockSpec((tm, d), lambda i: (i, 0)),
                  pl.BlockSpec((tm, p.shape[1]), lambda i: (i, 0)),
                  pl.BlockSpec((1, d), const),
                  pl.BlockSpec(w_gate.shape, const), pl.BlockSpec(w_proj.shape, const)],
        out_specs=pl.BlockSpec((tm, d), lambda i: (i, 0)),
        out_shape=jax.ShapeDtypeStruct((t, d), F32),
        compiler_params=_params(("parallel",)),
        name="ple_gate",
    )(x, p, g.reshape(1, d), w_gate, w_proj)


def _split_in_proj(w):
    widths = [NSA_HEADS * HEAD_DIM] + [NSA_KV * HEAD_DIM] * 6 + [NSA_HEADS * 3, NSA_HEADS * HEAD_DIM,
              DSA_HEADS * HEAD_DIM, DSA_KV * HEAD_DIM, DSA_KV * HEAD_DIM, IDX_HEADS * IDX_DIM, IDX_DIM,
              IDX_HEADS, DSA_HEADS * HEAD_DIM, D_MODEL, D_MODEL]
    offs = np.concatenate([[0], np.cumsum(widths)])
    assert offs[-1] == w.shape[1]
    (nq, nkc, nvc, nks, nvs, nkw, nvw, ng, nz, dq, dk, dv, iq, ik, iw, dz, mgn, mgd) = [
        w[:, offs[k]:offs[k + 1]] for k in range(len(widths))]
    pad = jnp.zeros((w.shape[0], LANES - IDX_DIM - IDX_HEADS - NSA_HEADS * 3), w.dtype)
    return dict(
        normed=jnp.concatenate([nq, nks, nkw, dq, dk], axis=1).astype(BF16),
        vals=jnp.concatenate([nvs, nvw, dv], axis=1).astype(BF16),
        iq=iq.astype(BF16),
        raw=jnp.concatenate([nkc, nvc, ik, iw, ng, pad], axis=1).astype(BF16),
        zact=jnp.concatenate([nz, dz], axis=1).astype(BF16),
        merge=jnp.concatenate([mgn, mgd], axis=1).astype(BF16),
    )


def _layer(x, p, norm_g, w_in, nsa_q_g, nsa_kc_g, nsa_ks_g, nsa_kw_g, cmp_pe_k, cmp_w1_k, cmp_w2_k,
           cmp_pe_v, cmp_w1_v, cmp_w2_v, dsa_q_g, dsa_k_g, w_up_nsa, w_up_dsa, w_out, ple_norm_g,
           w_ple_gate, w_ple_proj):
    batch, seq, d = x.shape
    t = batch * seq
    assert seq % LANES == 0 and seq >= WINDOW + LANES
    x2 = x.reshape(t, d)
    tm = min(1024, t)

    w = _split_in_proj(w_in)
    h = _rmsnorm(x2, norm_g, min(512, t))

    gains = jnp.concatenate([jnp.tile(nsa_q_g * ATTN_SCALE, NSA_HEADS), jnp.tile(nsa_ks_g, NSA_KV),
                             jnp.tile(nsa_kw_g, NSA_KV), jnp.tile(dsa_q_g * ATTN_SCALE, DSA_HEADS),
                             jnp.tile(dsa_k_g, DSA_KV)])
    qkn = _proj(h, w["normed"], tm, w["normed"].shape[1] // 2, BF16, gain=gains)
    vals = _proj(h, w["vals"], tm, w["vals"].shape[1], BF16)
    qi_heads = _proj_heads(h, w["iq"], min(tm, seq), batch, seq)
    kc_raw, vc_raw, misc = _proj_split(h, w["raw"], tm, (NSA_KV * HEAD_DIM, NSA_KV * HEAD_DIM, LANES))
    zact = _proj(h, w["zact"], tm, 1024, F32, act="silu")
    gate_act = _proj(h, w["merge"], tm, 1024, F32, act="sigmoid")

    ki = misc[:, :IDX_DIM].astype(BF16)
    wi = misc[:, IDX_DIM:IDX_DIM + IDX_HEADS]
    ng = misc[:, IDX_DIM + IDX_HEADS:IDX_DIM + IDX_HEADS + NSA_HEADS * 3]
    per_group = HEADS_PER_GROUP * 3
    gates = jnp.pad(ng.reshape(t, NSA_KV, per_group), ((0, 0), (0, 0), (0, LANES - per_group))).reshape(t, NSA_KV * LANES)

    kc = _compress(kc_raw, cmp_pe_k, cmp_w1_k, cmp_w2_k, nsa_kc_g, batch, seq)
    vc = _compress(vc_raw, cmp_pe_v, cmp_w1_v, cmp_w2_v, None, batch, seq)

    tq = 128
    tk = min(512, seq)
    ocmp, selb, tile_flags = _nsa_cmp(qkn, kc, vc, batch, seq, min(256, seq), tk)
    o_nsa = _nsa_main(qkn, vals, selb, tile_flags, ocmp, gates, zact, batch, seq, tq, tk)

    bias = _dsa_index(qi_heads, ki, wi.T, batch, seq, tq, tk)
    o_dsa = _dsa_attn(qkn, vals, bias, zact, batch, seq, tq, tk)

    x1 = _merge(o_nsa, o_dsa, gate_act, x2, w_up_nsa.astype(BF16), w_up_dsa.astype(BF16), w_out.astype(BF16),
                min(256, t))
    x3 = _ple(x1, p.reshape(t, PLE_DIM), ple_norm_g, w_ple_gate.astype(BF16), w_ple_proj.astype(BF16), min(256, t))
    return x3.reshape(batch, seq, d)


def kernel(x, p, norm_g, w_in, nsa_q_g, nsa_kc_g, nsa_ks_g, nsa_kw_g, cmp_pe_k, cmp_w1_k, cmp_w2_k, cmp_pe_v, cmp_w1_v, cmp_w2_v, dsa_q_g, dsa_k_g, w_up_nsa, w_up_dsa, w_out, ple_norm_g, w_ple_gate, w_ple_proj):
    depth = w_in.shape[0]
    for i in range(depth):
        x = _layer(x, p[i], norm_g[i], w_in[i], nsa_q_g[i], nsa_kc_g[i], nsa_ks_g[i], nsa_kw_g[i],
                   cmp_pe_k[i], cmp_w1_k[i], cmp_w2_k[i], cmp_pe_v[i], cmp_w1_v[i], cmp_w2_v[i],
                   dsa_q_g[i], dsa_k_g[i], w_up_nsa[i], w_up_dsa[i], w_out[i], ple_norm_g[i],
                   w_ple_gate[i], w_ple_proj[i])
    return x
```

```python
import functools

import numpy as np
import jax
import jax.numpy as jnp
from jax import lax
from jax.experimental import pallas as pl
from jax.experimental.pallas import tpu as pltpu

D_MODEL = 2048
HEAD_DIM = 128
NSA_HEADS = 8
NSA_KV = 2
DSA_HEADS = 8
DSA_KV = 2
HEADS_PER_GROUP = 4
CMP_LEN = 32
CMP_STRIDE = 16
CMP_HIDDEN = 256
SEL_BLOCK = 64
SEL_TOPN = 16
WINDOW = 512
IDX_HEADS = 16
IDX_DIM = 64
DSA_TOPK_MAX = 256
PLE_DIM = 256
EPS = 1e-6
NEG = -1e30
FORCE_SCORE = 1e4
ATTN_SCALE = HEAD_DIM ** -0.5
LANES = 128
INT_MIN = -(2 ** 31)
RADIX_CHECKS = (24, 27)

VMEM_LIMIT_BYTES = 56 * 1024 * 1024

F32 = jnp.float32
BF16 = jnp.bfloat16


def _params(semantics):
    return pltpu.CompilerParams(dimension_semantics=semantics, vmem_limit_bytes=VMEM_LIMIT_BYTES)


def _sigmoid(x):
    return 1.0 / (1.0 + jnp.exp(-x))


def _dot(a, b):
    return jnp.dot(a, b, preferred_element_type=F32)


def _dot_nt(a, b):
    return lax.dot_general(a, b, (((1,), (1,)), ((), ())), preferred_element_type=F32)


def _log2(n):
    assert n > 0 and n & (n - 1) == 0, n
    return n.bit_length() - 1


def _div_pow2(x, n):
    return lax.shift_right_logical(x, jnp.int32(_log2(n)))


def _group_slope_scale(g):
    return jnp.where(g == 0, 1.0, 2.0 ** -HEADS_PER_GROUP).astype(F32)


def _rmsnorm_kernel(x_ref, g_ref, o_ref):
    x = x_ref[...]
    ms = jnp.mean(x * x, axis=-1, keepdims=True)
    o_ref[...] = (x * lax.rsqrt(ms + EPS) * g_ref[...]).astype(o_ref.dtype)


def _rmsnorm(x, g, tm):
    t, d = x.shape
    return pl.pallas_call(
        _rmsnorm_kernel,
        grid=(t // tm,),
        in_specs=[pl.BlockSpec((tm, d), lambda i: (i, 0)), pl.BlockSpec((1, d), lambda i: (0, 0))],
        out_specs=pl.BlockSpec((tm, d), lambda i: (i, 0)),
        out_shape=jax.ShapeDtypeStruct((t, d), BF16),
        compiler_params=_params(("parallel",)),
        name="in_rmsnorm",
    )(x, g.reshape(1, d))


def _proj_headnorm_kernel(h_ref, w_ref, g_ref, o_ref):
    y = _dot(h_ref[...], w_ref[...])
    for hd in range(y.shape[1] // HEAD_DIM):
        sl = slice(hd * HEAD_DIM, (hd + 1) * HEAD_DIM)
        yh = y[:, sl]
        ms = jnp.mean(yh * yh, axis=-1, keepdims=True)
        o_ref[:, sl] = (yh * lax.rsqrt(ms + EPS) * g_ref[:, sl]).astype(o_ref.dtype)


def _proj_act_kernel(h_ref, w_ref, o_ref, *, act):
    y = _dot(h_ref[...], w_ref[...])
    if act == "silu":
        y = y * _sigmoid(y)
    elif act == "sigmoid":
        y = _sigmoid(y)
    o_ref[...] = y.astype(o_ref.dtype)


def _proj_split_kernel(h_ref, w_ref, *o_refs):
    y = _dot(h_ref[...], w_ref[...])
    off = 0
    for o_ref in o_refs:
        n = o_ref.shape[1]
        o_ref[...] = y[:, off:off + n].astype(o_ref.dtype)
        off += n


def _proj(h, w, tm, tn, out_dtype, act=None, gain=None):
    t, d = h.shape
    n = w.shape[1]
    in_specs = [pl.BlockSpec((tm, d), lambda i, j: (i, 0)), pl.BlockSpec((d, tn), lambda i, j: (0, j))]
    args = [h, w]
    if gain is not None:
        body = _proj_headnorm_kernel
        in_specs.append(pl.BlockSpec((1, tn), lambda i, j: (0, j)))
        args.append(gain.reshape(1, n))
    else:
        body = functools.partial(_proj_act_kernel, act=act)
    return pl.pallas_call(
        body,
        grid=(t // tm, n // tn),
        in_specs=in_specs,
        out_specs=pl.BlockSpec((tm, tn), lambda i, j: (i, j)),
        out_shape=jax.ShapeDtypeStruct((t, n), out_dtype),
        compiler_params=_params(("parallel", "arbitrary")),
        name="in_proj_" + ("headnorm" if gain is not None else str(act)),
    )(*args)


def _proj_heads_kernel(h_ref, w_ref, o_ref):
    y = _dot(h_ref[...], w_ref[...])
    for hd in range(o_ref.shape[1]):
        o_ref[0, hd] = y[:, hd * IDX_DIM:(hd + 1) * IDX_DIM].astype(o_ref.dtype)


def _proj_heads(h, w, tm, batch, seq):
    t, d = h.shape
    per_batch = seq // tm
    return pl.pallas_call(
        _proj_heads_kernel,
        grid=(t // tm,),
        in_specs=[pl.BlockSpec((tm, d), lambda i: (i, 0)), pl.BlockSpec(w.shape, lambda i: (0, 0))],
        out_specs=pl.BlockSpec((1, IDX_HEADS, tm, IDX_DIM), lambda i: (i // per_batch, 0, i % per_batch, 0)),
        out_shape=jax.ShapeDtypeStruct((batch, IDX_HEADS, seq, IDX_DIM), BF16),
        compiler_params=_params(("parallel",)),
        name="in_proj_heads",
    )(h, w)


def _proj_split(h, w, tm, widths):
    t, d = h.shape
    n = w.shape[1]
    return pl.pallas_call(
        _proj_split_kernel,
        grid=(t // tm,),
        in_specs=[pl.BlockSpec((tm, d), lambda i: (i, 0)), pl.BlockSpec((d, n), lambda i: (0, 0))],
        out_specs=[pl.BlockSpec((tm, wd), lambda i: (i, 0)) for wd in widths],
        out_shape=[jax.ShapeDtypeStruct((t, wd), F32) for wd in widths],
        compiler_params=_params(("parallel",)),
        name="in_proj_split",
    )(h, w)


def _compress_kernel(x_ref, pea_ref, peb_ref, wa_ref, wb_ref, w2_ref, g_ref, o_ref, *, normalize):
    x = x_ref[...]
    nch = x.shape[0]
    ha = _dot((x + pea_ref[...]).astype(BF16), wa_ref[...])
    hb = _dot((x + peb_ref[...]).astype(BF16), wb_ref[...])
    hid = ha + pltpu.roll(hb, nch - 1, axis=0)
    act = (hid * _sigmoid(hid)).astype(BF16)
    for g in range(NSA_KV):
        o = _dot(act[:, g * CMP_HIDDEN:(g + 1) * CMP_HIDDEN], w2_ref[...])
        if normalize:
            ms = jnp.mean(o * o, axis=-1, keepdims=True)
            o = o * lax.rsqrt(ms + EPS) * g_ref[...]
        o_ref[0, :, g * HEAD_DIM:(g + 1) * HEAD_DIM] = o.astype(o_ref.dtype)


def _expand_compress_params(pe, w1):
    half = CMP_LEN // 2
    w = w1.reshape(CMP_LEN, HEAD_DIM, CMP_HIDDEN)
    eye = jnp.eye(NSA_KV, dtype=w1.dtype)

    def expand_w(wh):
        return jnp.einsum("ldj,hg->lhdgj", wh, eye).reshape(half * NSA_KV * HEAD_DIM, NSA_KV * CMP_HIDDEN)

    def expand_pe(ph):
        return jnp.broadcast_to(ph[:, None, :], (half, NSA_KV, HEAD_DIM)).reshape(1, half * NSA_KV * HEAD_DIM)

    return (expand_pe(pe[:half]), expand_pe(pe[half:]),
            expand_w(w[:half]).astype(BF16), expand_w(w[half:]).astype(BF16))


def _compress(raw, pe, w1, w2, gain, batch, seq):
    nch = seq // CMP_STRIDE
    width = CMP_STRIDE * NSA_KV * HEAD_DIM
    x = raw.reshape(batch * nch, width)
    pea, peb, wa, wb = _expand_compress_params(pe, w1)
    normalize = gain is not None
    g = (gain if normalize else jnp.ones((HEAD_DIM,), F32)).reshape(1, HEAD_DIM)
    const = lambda b: (0, 0)
    return pl.pallas_call(
        functools.partial(_compress_kernel, normalize=normalize),
        grid=(batch,),
        in_specs=[pl.BlockSpec((nch, width), lambda b: (b, 0)),
                  pl.BlockSpec((1, width), const), pl.BlockSpec((1, width), const),
                  pl.BlockSpec(wa.shape, const), pl.BlockSpec(wb.shape, const),
                  pl.BlockSpec((CMP_HIDDEN, HEAD_DIM), const), pl.BlockSpec((1, HEAD_DIM), const)],
        out_specs=pl.BlockSpec((1, nch, NSA_KV * HEAD_DIM), lambda b: (b, 0, 0)),
        out_shape=jax.ShapeDtypeStruct((batch, nch, NSA_KV * HEAD_DIM), BF16),
        compiler_params=_params(("parallel",)),
        name="nsa_compress",
    )(x, pea, peb, wa, wb, w2.astype(BF16), g)


def _cmp_to_sel_matrix(nch, n_sel_pad):
    n_cmp = nch - 1
    c0 = np.arange(nch)[:, None] * CMP_STRIDE
    s0 = np.arange(n_sel_pad)[None, :] * SEL_BLOCK
    ov = np.clip(np.minimum(c0 + CMP_LEN, s0 + SEL_BLOCK) - np.maximum(c0, s0), 0, None) / CMP_LEN
    ov[n_cmp:] = 0.0
    return jnp.asarray(ov.T, dtype=BF16)


def _nsa_cmp_kernel(q_ref, kc_ref, vc_ref, mt_ref, eye_ref, tile_of_ref, ocmp_ref, selb_ref, flags_ref,
                    *, tq, nch, n_sel):
    g = pl.program_id(1)
    i = pl.program_id(2)
    t = i * tq + lax.broadcasted_iota(jnp.int32, (tq, 1), 0)
    c_end = lax.broadcasted_iota(jnp.int32, (1, nch), 1) * CMP_STRIDE + (CMP_LEN - 1)
    vis = c_end <= t
    distc = t.astype(F32) - (c_end.astype(F32) - (CMP_LEN - 1) / 2.0)
    live = (t >= CMP_LEN - 1).astype(F32)
    kc = kc_ref[0]
    vc = vc_ref[0]
    gsc = _group_slope_scale(g)
    psum = jnp.zeros((tq, nch), F32)
    for j in range(HEADS_PER_GROUP):
        slope = (2.0 ** -(j + 1)) * gsc
        sl = slice(j * HEAD_DIM, (j + 1) * HEAD_DIM)
        s = _dot_nt(q_ref[:, sl], kc) - slope * distc
        s = jnp.where(vis, s, NEG)
        e = jnp.exp(s - jnp.max(s, axis=-1, keepdims=True))
        p = e * (live / jnp.sum(e, axis=-1, keepdims=True))
        psum = psum + p
        ocmp_ref[:, sl] = _dot(p.astype(BF16), vc)

    imp = _dot_nt(mt_ref[...], psum.astype(BF16))
    blk = lax.broadcasted_iota(jnp.int32, (LANES, 1), 0)
    blk_t = _div_pow2(i * tq + lax.broadcasted_iota(jnp.int32, (1, tq), 1), SEL_BLOCK)
    valid = blk <= blk_t
    forced = (blk == 0) | (blk == blk_t) | (blk == blk_t - 1)
    vals = jnp.where(valid & jnp.logical_not(forced) & (blk < n_sel), imp, -1.0)
    sel = valid & forced
    for _ in range(SEL_TOPN - 3):
        mx = jnp.max(vals, axis=0, keepdims=True)
        first = jnp.min(jnp.where(vals == mx, blk, LANES), axis=0, keepdims=True)
        pick = blk == first
        sel = sel | pick
        vals = jnp.where(pick, -3.0, vals)
    sel = sel & valid
    masked = jnp.where(sel, 0.0, NEG).astype(BF16)
    selb_ref[...] = _dot_nt(eye_ref[...], masked).astype(selb_ref.dtype)

    t_lane = i * tq + lax.broadcasted_iota(jnp.int32, (1, tq), 1)
    early = blk < (SEGMENT // SEL_BLOCK) * _div_pow2(t_lane, SEGMENT)
    per_query = _dot(tile_of_ref[...], jnp.where(sel & early, 1.0, 0.0).astype(BF16))
    out_lane = lax.broadcasted_iota(jnp.int32, flags_ref.shape[1:], 1)
    flags = jnp.zeros(flags_ref.shape[1:], jnp.int32)
    for h in range(tq // SEGMENT):
        hit = jnp.max(per_query[:, h * SEGMENT:(h + 1) * SEGMENT], axis=1, keepdims=True) > 0.0
        flags = jnp.where((out_lane == h) & hit, 1, flags)
    flags_ref[0] = flags


SEGMENT = 128
TILE_ROWS = 16


def _nsa_cmp(qkn, kc, vc, batch, seq, tq, tk):
    t = batch * seq
    nch = seq // CMP_STRIDE
    n_sel = seq // SEL_BLOCK
    n_all = seq // tk
    assert n_sel <= LANES and n_all <= TILE_ROWS and tq % SEGMENT == 0
    nq = seq // tq
    gw = HEADS_PER_GROUP * HEAD_DIM
    mt = _cmp_to_sel_matrix(nch, LANES)
    eye = jnp.asarray(np.eye(tq, dtype=np.float32), dtype=BF16)
    tile_of = jnp.asarray(np.arange(LANES)[None, :] // (tk // SEL_BLOCK) == np.arange(TILE_ROWS)[:, None], dtype=BF16)
    row = lambda b, g, i: (b * nq + i, g)
    ocmp, selb, flags = pl.pallas_call(
        functools.partial(_nsa_cmp_kernel, tq=tq, nch=nch, n_sel=n_sel),
        grid=(batch, NSA_KV, nq),
        in_specs=[pl.BlockSpec((tq, gw), row),
                  pl.BlockSpec((1, nch, HEAD_DIM), lambda b, g, i: (b, 0, g)),
                  pl.BlockSpec((1, nch, HEAD_DIM), lambda b, g, i: (b, 0, g)),
                  pl.BlockSpec((LANES, nch), lambda b, g, i: (0, 0)),
                  pl.BlockSpec((tq, tq), lambda b, g, i: (0, 0)),
                  pl.BlockSpec((TILE_ROWS, LANES), lambda b, g, i: (0, 0))],
        out_specs=[pl.BlockSpec((tq, gw), row), pl.BlockSpec((tq, LANES), row),
                   pl.BlockSpec((1, TILE_ROWS, LANES), lambda b, g, i: ((b * NSA_KV + g) * nq + i, 0, 0))],
        out_shape=[jax.ShapeDtypeStruct((t, NSA_HEADS * HEAD_DIM), F32),
                   jax.ShapeDtypeStruct((t, NSA_KV * LANES), BF16),
                   jax.ShapeDtypeStruct((batch * NSA_KV * nq, TILE_ROWS, LANES), jnp.int32)],
        compiler_params=_params(("parallel", "parallel", "parallel")),
        name="nsa_cmp_select",
    )(qkn, kc, vc, mt, eye, tile_of)
    segs = tq // SEGMENT
    tile_flags = flags[:, :n_all, :segs].transpose(0, 2, 1).reshape(batch, NSA_KV, nq * segs, n_all)
    return ocmp, selb, tile_flags


def _stack_rows(t0, tq):
    r = lax.broadcasted_iota(jnp.int32, (HEADS_PER_GROUP * tq, 1), 0)
    return t0 + (r & (tq - 1))


def _key_aux_blocks(seq):
    j = np.arange(seq)
    aux = (j[:, None] // SEL_BLOCK == np.arange(LANES)[None, :]).astype(np.float32)
    aux[:, 0] = j % SEL_BLOCK
    return jnp.asarray(aux, dtype=BF16)


_POS_SPLIT = 256


def _key_aux_position(seq):
    assert seq <= _POS_SPLIT * _POS_SPLIT
    j = np.arange(seq)
    aux = np.zeros((seq, LANES), np.float32)
    aux[:, 0] = j % _POS_SPLIT
    aux[:, 1] = j - j % _POS_SPLIT
    return jnp.asarray(aux, dtype=BF16)


def _query_aug(q_ref, aux_cols):
    parts = [jnp.concatenate([q_ref[:, j * HEAD_DIM:(j + 1) * HEAD_DIM], aux_cols[j].astype(BF16)], axis=1)
             for j in range(HEADS_PER_GROUP)]
    return jnp.concatenate(parts, axis=0)


def _flash_step(s, v, carry):
    m, l, acc = carry
    m_new = jnp.maximum(m, jnp.max(s, axis=-1, keepdims=True))
    alpha = jnp.exp(m - m_new)
    p = jnp.exp(s - m_new).astype(BF16)
    pv = _dot(p, jnp.concatenate([v, jnp.ones_like(v)], axis=1))
    l = alpha * l + pv[:, HEAD_DIM:HEAD_DIM + 1]
    acc = alpha * acc + pv[:, :HEAD_DIM]
    return m_new, l, acc


def _flash_init(rows):
    return (jnp.full((rows, 1), NEG, F32), jnp.zeros((rows, 1), F32), jnp.zeros((rows, HEAD_DIM), F32))


def _pipelined_tiles(n, n_all, scores_fn, consume_fn, s_a, s_b, carry):
    s_a[...] = scores_fn(0)

    def pairs(k, count, carry):
        for c in range(count):
            s_b[...] = scores_fn(k + 2 * c + 1)
            carry = consume_fn(k + 2 * c, s_a[...], carry)
            s_a[...] = scores_fn(jnp.minimum(k + 2 * c + 2, n_all - 1))
            carry = consume_fn(k + 2 * c + 1, s_b[...], carry)
        return carry

    quads = lax.shift_right_logical(n, 2)
    carry = lax.fori_loop(0, quads, lambda u, c: pairs(4 * u, 2, c), carry)
    done = 4 * quads
    carry = lax.cond((n & 2) == 2, lambda c: pairs(done, 1, c), lambda c: c, carry)
    return lax.cond((n & 1) == 1, lambda c: consume_fn(n - 1, s_a[...], c), lambda c: c, carry)


def _nsa_main_kernel(tiles_ref, count_ref, q_ref, ks_ref, vs_ref, kw_ref, vw_ref, kaux_ref, kpos_ref, selb_ref,
                     ocmp_ref, gate_ref, z_ref, o_ref, s_a, s_b, *, tq, tk, seq):
    b = pl.program_id(0)
    g = pl.program_id(1)
    i = pl.program_id(2)
    n_all = seq // tk
    t0 = pl.multiple_of(i * tq, tq)
    rows = HEADS_PER_GROUP * tq
    gsc = _group_slope_scale(g)
    t_rows = _stack_rows(t0, tq)
    slopes = [(2.0 ** -(j + 1)) * gsc for j in range(HEADS_PER_GROUP)]

    lane = lax.broadcasted_iota(jnp.int32, (tq, LANES), 1)
    chosen = (selb_ref[...].astype(F32) > -1.0) & (lane < _div_pow2(t0, SEL_BLOCK))
    block_start = (lane * SEL_BLOCK).astype(F32)
    q_aug = _query_aug(q_ref, [jnp.where(lane == 0, sl, jnp.where(chosen, sl * block_start, NEG)) for sl in slopes])

    slot = (b * NSA_KV + g) * (seq // tq) + i
    base = slot * n_all

    def scores(u):
        j0 = pl.multiple_of(tiles_ref[base + u] * tk, tk)
        return _dot_nt(q_aug, jnp.concatenate([ks_ref[pl.ds(j0, tk), :], kaux_ref[pl.ds(j0, tk), :]], axis=1))

    def consume(u, s, carry):
        j0 = pl.multiple_of(tiles_ref[base + u] * tk, tk)
        return _flash_step(s, vs_ref[pl.ds(j0, tk), :], carry)

    carry = _pipelined_tiles(count_ref[slot], n_all, scores, consume, s_a, s_b, _flash_init(rows))

    q_pos = _query_aug(q_ref, [jnp.where(lane < 2, sl, 0.0) for sl in slopes])

    j_own = t0 + lax.broadcasted_iota(jnp.int32, (1, tq), 1)
    s = _dot_nt(q_pos, jnp.concatenate([ks_ref[pl.ds(t0, tq), :], kpos_ref[pl.ds(t0, tq), :]], axis=1))
    s = jnp.where(j_own <= t_rows, s, NEG)
    _, l_s, acc_s = _flash_step(s, vs_ref[pl.ds(t0, tq), :], carry)
    o_slc = acc_s / l_s

    wk = WINDOW + tq
    start = pl.multiple_of(jnp.maximum(t0 - WINDOW, 0), tq)
    j_win = start + lax.broadcasted_iota(jnp.int32, (1, wk), 1)
    s = _dot_nt(q_pos, jnp.concatenate([kw_ref[pl.ds(start, wk), :], kpos_ref[pl.ds(start, wk), :]], axis=1))
    s = jnp.where((j_win <= t_rows) & (j_win > t_rows - WINDOW), s, NEG)
    e = jnp.exp(s - jnp.max(s, axis=-1, keepdims=True))
    o_win = _dot(e.astype(BF16), vw_ref[pl.ds(start, wk), :]) / jnp.sum(e, axis=-1, keepdims=True)

    for j in range(HEADS_PER_GROUP):
        sl = slice(j * HEAD_DIM, (j + 1) * HEAD_DIM)
        rs = slice(j * tq, (j + 1) * tq)
        gates = [_sigmoid(gate_ref[:, 3 * j + c:3 * j + c + 1]) for c in range(3)]
        o = gates[0] * ocmp_ref[:, sl] + gates[1] * o_slc[rs] + gates[2] * o_win[rs]
        o_ref[:, sl] = (o * z_ref[:, sl]).astype(o_ref.dtype)


def _nsa_main(qkn, vals, selb, tile_flags, ocmp, gates, zact, batch, seq, tq, tk):
    t = batch * seq
    nq = seq // tq
    n_all = seq // tk
    gw = HEADS_PER_GROUP * HEAD_DIM
    flags = tile_flags.reshape(batch * NSA_KV * nq, n_all)
    tiles = jnp.argsort(1 - flags, axis=-1, stable=True).astype(jnp.int32).reshape(-1)
    counts = jnp.sum(flags, axis=-1).astype(jnp.int32)
    row = lambda b, g, i, *_: (b * nq + i, g)
    ks_blk = (NSA_HEADS * HEAD_DIM) // HEAD_DIM
    kw_blk = ks_blk + NSA_KV
    grid_spec = pltpu.PrefetchScalarGridSpec(
        num_scalar_prefetch=2,
        grid=(batch, NSA_KV, nq),
        in_specs=[pl.BlockSpec((tq, gw), row),
                  pl.BlockSpec((seq, HEAD_DIM), lambda b, g, i, *_: (b, ks_blk + g)),
                  pl.BlockSpec((seq, HEAD_DIM), lambda b, g, i, *_: (b, g)),
                  pl.BlockSpec((seq, HEAD_DIM), lambda b, g, i, *_: (b, kw_blk + g)),
                  pl.BlockSpec((seq, HEAD_DIM), lambda b, g, i, *_: (b, NSA_KV + g)),
                  pl.BlockSpec((seq, LANES), lambda b, g, i, *_: (0, 0)),
                  pl.BlockSpec((seq, LANES), lambda b, g, i, *_: (0, 0)),
                  pl.BlockSpec((tq, LANES), row),
                  pl.BlockSpec((tq, gw), row),
                  pl.BlockSpec((tq, LANES), row),
                  pl.BlockSpec((tq, gw), row)],
        out_specs=pl.BlockSpec((tq, gw), row),
        scratch_shapes=[pltpu.VMEM((HEADS_PER_GROUP * tq, tk), F32)] * 2,
    )
    return pl.pallas_call(
        functools.partial(_nsa_main_kernel, tq=tq, tk=tk, seq=seq),
        grid_spec=grid_spec,
        out_shape=jax.ShapeDtypeStruct((t, NSA_HEADS * HEAD_DIM), BF16),
        compiler_params=_params(("parallel", "parallel", "arbitrary")),
        name="nsa_select_window",
    )(tiles, counts, qkn, qkn, vals, qkn, vals, _key_aux_blocks(seq), _key_aux_position(seq), selb, ocmp, gates,
      zact)


def _sortable_key(x):
    bits = pltpu.bitcast(x, jnp.int32)
    bits = jnp.where(bits == INT_MIN, 0, bits)
    return jnp.where(bits < 0, bits ^ 0x7FFFFFFF, bits)


def _dsa_index_kernel(qi_ref, ki_ref, wi_ref, eye_ref, tri_ref, bias_ref, key_scr, *, tq, tk, seq, topk):
    i = pl.program_id(1)
    t0 = i * tq
    n_all = seq // tk
    n_kt = _div_pow2(t0 + tq + tk - 1, tk)
    t = t0 + lax.broadcasted_iota(jnp.int32, (1, tq), 1)
    wrow = wi_ref[...] * (IDX_HEADS ** -0.5 * IDX_DIM ** -0.5)
    hb = 4

    def key_ids(kt):
        return kt * tk + lax.broadcasted_iota(jnp.int32, (tk, 1), 0)

    def score_tile(kt):
        j0 = pl.multiple_of(kt * tk, tk)
        ki = ki_ref[pl.ds(j0, tk), :]
        score = jnp.zeros((tk, tq), F32)
        for h0 in range(0, IDX_HEADS, hb):
            x = _dot_nt(ki, qi_ref[0, h0:h0 + hb].reshape(hb * tq, IDX_DIM))
            for h in range(hb):
                score = score + jnp.maximum(x[:, h * tq:(h + 1) * tq], 0.0) * wrow[h0 + h:h0 + h + 1, :]
        key_scr[kt] = jnp.where(key_ids(kt) <= t, _sortable_key(score), INT_MIN)

    n_pairs = lax.shift_right_logical(n_kt + 1, 1)

    def score_step(u, _):
        score_tile(2 * u)
        score_tile(2 * u + 1)
        return 0

    lax.fori_loop(0, n_pairs, score_step, 0)

    def count(pred):
        def step(u, c):
            for kt in (2 * u, 2 * u + 1):
                hit = jnp.where(pred(key_scr[kt]), 1, 0)
                c = c + jnp.sum(hit.reshape(tk // 8, 8, tq), axis=0)
            return c
        c = lax.fori_loop(0, n_pairs, step, jnp.zeros((8, tq), jnp.int32))
        return jnp.sum(c, axis=0, keepdims=True)

    everything = jnp.int32(seq + tk)
    c0 = count(lambda k: k >= 0)
    state = (jnp.where(c0 >= topk, 0, INT_MIN), jnp.where(c0 >= topk, c0, everything))

    def bit_step(b, state):
        v, n_ge = state
        cand = v | (jnp.int32(1) << (30 - b))
        c = count(lambda k: k >= cand)
        return jnp.where(c >= topk, cand, v), jnp.where(c >= topk, c, n_ge)

    def settled(state):
        done = (state[1] == topk) | (t < topk)
        return jnp.min(jnp.where(done, 1, 0)) > 0

    state = lax.fori_loop(0, RADIX_CHECKS[0], bit_step, state)
    for lo, hi in zip(RADIX_CHECKS, RADIX_CHECKS[1:] + (31,)):
        state = lax.cond(settled(state), lambda s: s, lambda s, lo=lo, hi=hi: lax.fori_loop(lo, hi, bit_step, s), state)
    v, n_ge = state
    def emit(kt, keep):
        masked = jnp.where(keep & (key_ids(kt) <= t), 0.0, NEG).astype(BF16)
        bias_ref[0, 0, kt] = _dot_nt(eye_ref[...], masked).astype(bias_ref.dtype)

    def emit_no_ties(_):
        def step(kt, c):
            emit(kt, key_scr[kt] >= v)
            return c
        return lax.fori_loop(0, n_kt, step, 0)

    def emit_with_ties(_):
        need = (topk - count(lambda k: k > v)).astype(F32)

        def step(kt, carry):
            key = key_scr[kt]
            eq = key == v
            eqf = jnp.where(eq, 1.0, 0.0)
            before = carry + _dot(tri_ref[...], eqf.astype(BF16)) - eqf
            emit(kt, (key > v) | (eq & (before < need)))
            return carry + jnp.sum(eqf, axis=0, keepdims=True)

        lax.fori_loop(0, n_kt, step, jnp.zeros((1, tq), F32))
        return 0

    over = (n_ge > topk) & (v > INT_MIN)
    lax.cond(jnp.max(jnp.where(over, 1, 0)) > 0, emit_with_ties, emit_no_ties, 0)

    def fill_step(kt, _):
        bias_ref[0, 0, kt] = jnp.full((tq, tk), NEG, bias_ref.dtype)
        return 0

    lax.fori_loop(n_kt, n_all, fill_step, 0)


def _dsa_index(qi_heads, ki, wi_t, batch, seq, tq, tk):
    nq = seq // tq
    n_all = seq // tk
    topk = min(DSA_TOPK_MAX, seq // 4)
    tri = jnp.asarray(np.tril(np.ones((tk, tk), np.float32)), dtype=BF16)
    eye = jnp.asarray(np.eye(tq, dtype=np.float32), dtype=BF16)
    return pl.pallas_call(
        functools.partial(_dsa_index_kernel, tq=tq, tk=tk, seq=seq, topk=topk),
        grid=(batch, nq),
        in_specs=[pl.BlockSpec((1, IDX_HEADS, tq, IDX_DIM), lambda b, i: (b, 0, i, 0)),
                  pl.BlockSpec((seq, IDX_DIM), lambda b, i: (b, 0)),
                  pl.BlockSpec((IDX_HEADS, tq), lambda b, i: (0, b * nq + i)),
                  pl.BlockSpec((tq, tq), lambda b, i: (0, 0)),
                  pl.BlockSpec((tk, tk), lambda b, i: (0, 0))],
        out_specs=pl.BlockSpec((1, 1, n_all, tq, tk), lambda b, i: (b, i, 0, 0, 0)),
        out_shape=jax.ShapeDtypeStruct((batch, nq, n_all, tq, tk), BF16),
        scratch_shapes=[pltpu.VMEM((n_all, tk, tq), jnp.int32)],
        compiler_params=_params(("parallel", "arbitrary")),
        name="dsa_index_topk",
    )(qi_heads, ki, wi_t, eye, tri)


def _dsa_attn_kernel(q_ref, k_ref, v_ref, kaux_ref, bias_ref, z_ref, o_ref, s_a, s_b, *, tq, tk, seq):
    g = pl.program_id(1)
    i = pl.program_id(2)
    t0 = i * tq
    rows = HEADS_PER_GROUP * tq
    gsc = _group_slope_scale(g)
    slopes = [(2.0 ** -(j + 1)) * gsc for j in range(HEADS_PER_GROUP)]
    lane = lax.broadcasted_iota(jnp.int32, (tq, LANES), 1)
    q_aug = _query_aug(q_ref, [jnp.where(lane < 2, sl, 0.0) for sl in slopes])

    def scores(kt):
        j0 = pl.multiple_of(kt * tk, tk)
        return _dot_nt(q_aug, jnp.concatenate([k_ref[pl.ds(j0, tk), :], kaux_ref[pl.ds(j0, tk), :]], axis=1))

    def consume(kt, s, carry):
        j0 = pl.multiple_of(kt * tk, tk)
        bias = bias_ref[0, 0, kt].astype(F32)
        s = s + jnp.concatenate([bias] * HEADS_PER_GROUP, axis=0)
        return _flash_step(s, v_ref[pl.ds(j0, tk), :], carry)

    n_kt = _div_pow2(t0 + tq + tk - 1, tk)
    _, l, acc = _pipelined_tiles(n_kt, seq // tk, scores, consume, s_a, s_b, _flash_init(rows))
    o = acc / l
    for j in range(HEADS_PER_GROUP):
        sl = slice(j * HEAD_DIM, (j + 1) * HEAD_DIM)
        o_ref[:, sl] = (o[j * tq:(j + 1) * tq] * z_ref[:, sl]).astype(o_ref.dtype)


def _dsa_attn(qkn, vals, bias, zact, batch, seq, tq, tk):
    t = batch * seq
    nq = seq // tq
    n_all = seq // tk
    gw = HEADS_PER_GROUP * HEAD_DIM
    row = lambda b, g, i: (b * nq + i, g)
    q_blk = (NSA_HEADS + 2 * NSA_KV) * HEAD_DIM // gw
    k_blk = (NSA_HEADS + 2 * NSA_KV + DSA_HEADS)
    return pl.pallas_call(
        functools.partial(_dsa_attn_kernel, tq=tq, tk=tk, seq=seq),
        grid=(batch, DSA_KV, nq),
        in_specs=[pl.BlockSpec((tq, gw), lambda b, g, i: (b * nq + i, q_blk + g)),
                  pl.BlockSpec((seq, HEAD_DIM), lambda b, g, i: (b, k_blk + g)),
                  pl.BlockSpec((seq, HEAD_DIM), lambda b, g, i: (b, 2 * NSA_KV + g)),
                  pl.BlockSpec((seq, LANES), lambda b, g, i: (0, 0)),
                  pl.BlockSpec((1, 1, n_all, tq, tk), lambda b, g, i: (b, i, 0, 0, 0)),
                  pl.BlockSpec((tq, gw), lambda b, g, i: (b * nq + i, NSA_KV + g))],
        out_specs=pl.BlockSpec((tq, gw), row),
        out_shape=jax.ShapeDtypeStruct((t, DSA_HEADS * HEAD_DIM), BF16),
        scratch_shapes=[pltpu.VMEM((HEADS_PER_GROUP * tq, tk), F32)] * 2,
        compiler_params=_params(("parallel", "parallel", "arbitrary")),
        name="dsa_attention",
    )(qkn, qkn, vals, _key_aux_position(seq), bias, zact)


def _merge_kernel(on_ref, od_ref, sn_ref, sd_ref, x_ref, wun_ref, wud_ref, wout_ref, o_ref):
    y = sn_ref[...] * _dot(on_ref[...], wun_ref[...]) + sd_ref[...] * _dot(od_ref[...], wud_ref[...])
    o_ref[...] = x_ref[...] + _dot(y.astype(BF16), wout_ref[...])


def _merge(o_nsa, o_dsa, gate_act, x, w_up_nsa, w_up_dsa, w_out, tm):
    t, d = x.shape
    const = lambda i: (0, 0)
    return pl.pallas_call(
        _merge_kernel,
        grid=(t // tm,),
        in_specs=[pl.BlockSpec((tm, o_nsa.shape[1]), lambda i: (i, 0)),
                  pl.BlockSpec((tm, o_dsa.shape[1]), lambda i: (i, 0)),
                  pl.BlockSpec((tm, d), lambda i: (i, 0)),
                  pl.BlockSpec((tm, d), lambda i: (i, 1)),
                  pl.BlockSpec((tm, d), lambda i: (i, 0)),
                  pl.BlockSpec(w_up_nsa.shape, const), pl.BlockSpec(w_up_dsa.shape, const),
                  pl.BlockSpec(w_out.shape, const)],
        out_specs=pl.BlockSpec((tm, d), lambda i: (i, 0)),
        out_shape=jax.ShapeDtypeStruct((t, d), F32),
        compiler_params=_params(("parallel",)),
        name="merge_out_proj",
    )(o_nsa, o_dsa, gate_act, gate_act, x, w_up_nsa, w_up_dsa, w_out)


def _ple_kernel(x_ref, p_ref, g_ref, wg_ref, wp_ref, o_ref):
    x = x_ref[...]
    ms = jnp.mean(x * x, axis=-1, keepdims=True)
    r = (x * lax.rsqrt(ms + EPS) * g_ref[...]).astype(BF16)
    gate = _sigmoid(_dot(r, wg_ref[...]))
    o_ref[...] = x + _dot(p_ref[...].astype(BF16), wp_ref[...]) * gate


def _ple(x, p, g, w_gate, w_proj, tm):
    t, d = x.shape
    const = lambda i: (0, 0)
    return pl.pallas_call(
        _ple_kernel,
        grid=(t // tm,),
        in_specs=[pl.BlockSpec((tm, d), lambda i: (i, 0)),
                  pl.BlockSpec((tm, p.shape[1]), lambda i: (i, 0)),
                  pl.BlockSpec((1, d), const),
                  pl.BlockSpec(w_gate.shape, const), pl.BlockSpec(w_proj.shape, const)],
        out_specs=pl.BlockSpec((tm, d), lambda i: (i, 0)),
        out_shape=jax.ShapeDtypeStruct((t, d), F32),
        compiler_params=_params(("parallel",)),
        name="ple_gate",
    )(x, p, g.reshape(1, d), w_gate, w_proj)


def _split_in_proj(w):
    widths = [NSA_HEADS * HEAD_DIM] + [NSA_KV * HEAD_DIM] * 6 + [NSA_HEADS * 3, NSA_HEADS * HEAD_DIM,
              DSA_HEADS * HEAD_DIM, DSA_KV * HEAD_DIM, DSA_KV * HEAD_DIM, IDX_HEADS * IDX_DIM, IDX_DIM,
              IDX_HEADS, DSA_HEADS * HEAD_DIM, D_MODEL, D_MODEL]
    offs = np.concatenate([[0], np.cumsum(widths)])
    assert offs[-1] == w.shape[1]
    (nq, nkc, nvc, nks, nvs, nkw, nvw, ng, nz, dq, dk, dv, iq, ik, iw, dz, mgn, mgd) = [
        w[:, offs[k]:offs[k + 1]] for k in range(len(widths))]
    pad = jnp.zeros((w.shape[0], LANES - IDX_DIM - IDX_HEADS - NSA_HEADS * 3), w.dtype)
    return dict(
        normed=jnp.concatenate([nq, nks, nkw, dq, dk], axis=1).astype(BF16),
        vals=jnp.concatenate([nvs, nvw, dv], axis=1).astype(BF16),
        iq=iq.astype(BF16),
        raw=jnp.concatenate([nkc, nvc, ik, iw, ng, pad], axis=1).astype(BF16),
        zact=jnp.concatenate([nz, dz], axis=1).astype(BF16),
        merge=jnp.concatenate([mgn, mgd], axis=1).astype(BF16),
    )


def _layer(x, p, norm_g, w_in, nsa_q_g, nsa_kc_g, nsa_ks_g, nsa_kw_g, cmp_pe_k, cmp_w1_k, cmp_w2_k,
           cmp_pe_v, cmp_w1_v, cmp_w2_v, dsa_q_g, dsa_k_g, w_up_nsa, w_up_dsa, w_out, ple_norm_g,
           w_ple_gate, w_ple_proj):
    batch, seq, d = x.shape
    t = batch * seq
    assert seq % LANES == 0 and seq >= WINDOW + LANES
    x2 = x.reshape(t, d)
    tm = min(1024, t)

    w = _split_in_proj(w_in)
    h = _rmsnorm(x2, norm_g, min(512, t))

    gains = jnp.concatenate([jnp.tile(nsa_q_g * ATTN_SCALE, NSA_HEADS), jnp.tile(nsa_ks_g, NSA_KV),
                             jnp.tile(nsa_kw_g, NSA_KV), jnp.tile(dsa_q_g * ATTN_SCALE, DSA_HEADS),
                             jnp.tile(dsa_k_g, DSA_KV)])
    qkn = _proj(h, w["normed"], tm, w["normed"].shape[1] // 2, BF16, gain=gains)
    vals = _proj(h, w["vals"], tm, w["vals"].shape[1], BF16)
    qi_heads = _proj_heads(h, w["iq"], min(tm, seq), batch, seq)
    kc_raw, vc_raw, misc = _proj_split(h, w["raw"], tm, (NSA_KV * HEAD_DIM, NSA_KV * HEAD_DIM, LANES))
    zact = _proj(h, w["zact"], tm, 1024, F32, act="silu")
    gate_act = _proj(h, w["merge"], tm, 1024, F32, act="sigmoid")

    ki = misc[:, :IDX_DIM].astype(BF16)
    wi = misc[:, IDX_DIM:IDX_DIM + IDX_HEADS]
    ng = misc[:, IDX_DIM + IDX_HEADS:IDX_DIM + IDX_HEADS + NSA_HEADS * 3]
    per_group = HEADS_PER_GROUP * 3
    gates = jnp.pad(ng.reshape(t, NSA_KV, per_group), ((0, 0), (0, 0), (0, LANES - per_group))).reshape(t, NSA_KV * LANES)

    kc = _compress(kc_raw, cmp_pe_k, cmp_w1_k, cmp_w2_k, nsa_kc_g, batch, seq)
    vc = _compress(vc_raw, cmp_pe_v, cmp_w1_v, cmp_w2_v, None, batch, seq)

    tq = 128
    tk = min(512, seq)
    ocmp, selb, tile_flags = _nsa_cmp(qkn, kc, vc, batch, seq, min(256, seq), tk)
    o_nsa = _nsa_main(qkn, vals, selb, tile_flags, ocmp, gates, zact, batch, seq, tq, tk)

    bias = _dsa_index(qi_heads, ki, wi.T, batch, seq, tq, tk)
    o_dsa = _dsa_attn(qkn, vals, bias, zact, batch, seq, tq, tk)

    x1 = _merge(o_nsa, o_dsa, gate_act, x2, w_up_nsa.astype(BF16), w_up_dsa.astype(BF16), w_out.astype(BF16),
                min(256, t))
    x3 = _ple(x1, p.reshape(t, PLE_DIM), ple_norm_g, w_ple_gate.astype(BF16), w_ple_proj.astype(BF16), min(256, t))
    return x3.reshape(batch, seq, d)


def kernel(x, p, norm_g, w_in, nsa_q_g, nsa_kc_g, nsa_ks_g, nsa_kw_g, cmp_pe_k, cmp_w1_k, cmp_w2_k, cmp_pe_v, cmp_w1_v, cmp_w2_v, dsa_q_g, dsa_k_g, w_up_nsa, w_up_dsa, w_out, ple_norm_g, w_ple_gate, w_ple_proj):
    depth = w_in.shape[0]
    for i in range(depth):
        x = _layer(x, p[i], norm_g[i], w_in[i], nsa_q_g[i], nsa_kc_g[i], nsa_ks_g[i], nsa_kw_g[i],
                   cmp_pe_k[i], cmp_w1_k[i], cmp_w2_k[i], cmp_pe_v[i], cmp_w1_v[i], cmp_w2_v[i],
                   dsa_q_g[i], dsa_k_g[i], w_up_nsa[i], w_up_dsa[i], w_out[i], ple_norm_g[i],
                   w_ple_gate[i], w_ple_proj[i])
    return x
```

```python
import functools

import numpy as np
import jax
import jax.numpy as jnp
from jax import lax
from jax.experimental import pallas as pl
from jax.experimental.pallas import tpu as pltpu

D_MODEL = 2048
HEAD_DIM = 128
NSA_HEADS = 8
NSA_KV = 2
DSA_HEADS = 8
DSA_KV = 2
HEADS_PER_GROUP = 4
CMP_LEN = 32
CMP_STRIDE = 16
CMP_HIDDEN = 256
SEL_BLOCK = 64
SEL_TOPN = 16
WINDOW = 512
IDX_HEADS = 16
IDX_DIM = 64
DSA_TOPK_MAX = 256
PLE_DIM = 256
EPS = 1e-6
NEG = -1e30
FORCE_SCORE = 1e4
ATTN_SCALE = HEAD_DIM ** -0.5
LANES = 128
INT_MIN = -(2 ** 31)
RADIX_CHECKS = (24, 27)

VMEM_LIMIT_BYTES = 56 * 1024 * 1024

F32 = jnp.float32
BF16 = jnp.bfloat16


def _params(semantics):
    return pltpu.CompilerParams(dimension_semantics=semantics, vmem_limit_bytes=VMEM_LIMIT_BYTES)


def _sigmoid(x):
    return 1.0 / (1.0 + jnp.exp(-x))


def _dot(a, b):
    return jnp.dot(a, b, preferred_element_type=F32)


def _dot_nt(a, b):
    return lax.dot_general(a, b, (((1,), (1,)), ((), ())), preferred_element_type=F32)


def _log2(n):
    assert n > 0 and n & (n - 1) == 0, n
    return n.bit_length() - 1


def _div_pow2(x, n):
    return lax.shift_right_logical(x, jnp.int32(_log2(n)))


def _group_slope_scale(g):
    return jnp.where(g == 0, 1.0, 2.0 ** -HEADS_PER_GROUP).astype(F32)


def _rmsnorm_kernel(x_ref, g_ref, o_ref):
    x = x_ref[...]
    ms = jnp.mean(x * x, axis=-1, keepdims=True)
    o_ref[...] = (x * lax.rsqrt(ms + EPS) * g_ref[...]).astype(o_ref.dtype)


def _rmsnorm(x, g, tm):
    t, d = x.shape
    return pl.pallas_call(
        _rmsnorm_kernel,
        grid=(t // tm,),
        in_specs=[pl.BlockSpec((tm, d), lambda i: (i, 0)), pl.BlockSpec((1, d), lambda i: (0, 0))],
        out_specs=pl.BlockSpec((tm, d), lambda i: (i, 0)),
        out_shape=jax.ShapeDtypeStruct((t, d), BF16),
        compiler_params=_params(("parallel",)),
        name="in_rmsnorm",
    )(x, g.reshape(1, d))


def _proj_headnorm_kernel(h_ref, w_ref, g_ref, o_ref):
    y = _dot(h_ref[...], w_ref[...])
    for hd in range(y.shape[1] // HEAD_DIM):
        sl = slice(hd * HEAD_DIM, (hd + 1) * HEAD_DIM)
        yh = y[:, sl]
        ms = jnp.mean(yh * yh, axis=-1, keepdims=True)
        o_ref[:, sl] = (yh * lax.rsqrt(ms + EPS) * g_ref[:, sl]).astype(o_ref.dtype)


def _proj_act_kernel(h_ref, w_ref, o_ref, *, act):
    y = _dot(h_ref[...], w_ref[...])
    if act == "silu":
        y = y * _sigmoid(y)
    elif act == "sigmoid":
        y = _sigmoid(y)
    o_ref[...] = y.astype(o_ref.dtype)


def _proj_split_kernel(h_ref, w_ref, *o_refs):
    y = _dot(h_ref[...], w_ref[...])
    off = 0
    for o_ref in o_refs:
        n = o_ref.shape[1]
        o_ref[...] = y[:, off:off + n].astype(o_ref.dtype)
        off += n


def _proj(h, w, tm, tn, out_dtype, act=None, gain=None):
    t, d = h.shape
    n = w.shape[1]
    in_specs = [pl.BlockSpec((tm, d), lambda i, j: (i, 0)), pl.BlockSpec((d, tn), lambda i, j: (0, j))]
    args = [h, w]
    if gain is not None:
        body = _proj_headnorm_kernel
        in_specs.append(pl.BlockSpec((1, tn), lambda i, j: (0, j)))
        args.append(gain.reshape(1, n))
    else:
        body = functools.partial(_proj_act_kernel, act=act)
    return pl.pallas_call(
        body,
        grid=(t // tm, n // tn),
        in_specs=in_specs,
        out_specs=pl.BlockSpec((tm, tn), lambda i, j: (i, j)),
        out_shape=jax.ShapeDtypeStruct((t, n), out_dtype),
        compiler_params=_params(("parallel", "arbitrary")),
        name="in_proj_" + ("headnorm" if gain is not None else str(act)),
    )(*args)


def _proj_heads_kernel(h_ref, w_ref, o_ref):
    y = _dot(h_ref[...], w_ref[...])
    for hd in range(o_ref.shape[1]):
        o_ref[0, hd] = y[:, hd * IDX_DIM:(hd + 1) * IDX_DIM].astype(o_ref.dtype)


def _proj_heads(h, w, tm, batch, seq):
    t, d = h.shape
    per_batch = seq // tm
    return pl.pallas_call(
        _proj_heads_kernel,
        grid=(t // tm,),
        in_specs=[pl.BlockSpec((tm, d), lambda i: (i, 0)), pl.BlockSpec(w.shape, lambda i: (0, 0))],
        out_specs=pl.BlockSpec((1, IDX_HEADS, tm, IDX_DIM), lambda i: (i // per_batch, 0, i % per_batch, 0)),
        out_shape=jax.ShapeDtypeStruct((batch, IDX_HEADS, seq, IDX_DIM), BF16),
        compiler_params=_params(("parallel",)),
        name="in_proj_heads",
    )(h, w)


def _proj_split(h, w, tm, widths):
    t, d = h.shape
    n = w.shape[1]
    return pl.pallas_call(
        _proj_split_kernel,
        grid=(t // tm,),
        in_specs=[pl.BlockSpec((tm, d), lambda i: (i, 0)), pl.BlockSpec((d, n), lambda i: (0, 0))],
        out_specs=[pl.BlockSpec((tm, wd), lambda i: (i, 0)) for wd in widths],
        out_shape=[jax.ShapeDtypeStruct((t, wd), F32) for wd in widths],
        compiler_params=_params(("parallel",)),
        name="in_proj_split",
    )(h, w)


def _compress_kernel(x_ref, pea_ref, peb_ref, wa_ref, wb_ref, w2_ref, g_ref, o_ref, *, normalize):
    x = x_ref[...]
    nch = x.shape[0]
    ha = _dot((x + pea_ref[...]).astype(BF16), wa_ref[...])
    hb = _dot((x + peb_ref[...]).astype(BF16), wb_ref[...])
    hid = ha + pltpu.roll(hb, nch - 1, axis=0)
    act = (hid * _sigmoid(hid)).astype(BF16)
    for g in range(NSA_KV):
        o = _dot(act[:, g * CMP_HIDDEN:(g + 1) * CMP_HIDDEN], w2_ref[...])
        if normalize:
            ms = jnp.mean(o * o, axis=-1, keepdims=True)
            o = o * lax.rsqrt(ms + EPS) * g_ref[...]
        o_ref[0, :, g * HEAD_DIM:(g + 1) * HEAD_DIM] = o.astype(o_ref.dtype)


def _expand_compress_params(pe, w1):
    half = CMP_LEN // 2
    w = w1.reshape(CMP_LEN, HEAD_DIM, CMP_HIDDEN)
    eye = jnp.eye(NSA_KV, dtype=w1.dtype)

    def expand_w(wh):
        return jnp.einsum("ldj,hg->lhdgj", wh, eye).reshape(half * NSA_KV * HEAD_DIM, NSA_KV * CMP_HIDDEN)

    def expand_pe(ph):
        return jnp.broadcast_to(ph[:, None, :], (half, NSA_KV, HEAD_DIM)).reshape(1, half * NSA_KV * HEAD_DIM)

    return (expand_pe(pe[:half]), expand_pe(pe[half:]),
            expand_w(w[:half]).astype(BF16), expand_w(w[half:]).astype(BF16))


def _compress(raw, pe, w1, w2, gain, batch, seq):
    nch = seq // CMP_STRIDE
    width = CMP_STRIDE * NSA_KV * HEAD_DIM
    x = raw.reshape(batch * nch, width)
    pea, peb, wa, wb = _expand_compress_params(pe, w1)
    normalize = gain is not None
    g = (gain if normalize else jnp.ones((HEAD_DIM,), F32)).reshape(1, HEAD_DIM)
    const = lambda b: (0, 0)
    return pl.pallas_call(
        functools.partial(_compress_kernel, normalize=normalize),
        grid=(batch,),
        in_specs=[pl.BlockSpec((nch, width), lambda b: (b, 0)),
                  pl.BlockSpec((1, width), const), pl.BlockSpec((1, width), const),
                  pl.BlockSpec(wa.shape, const), pl.BlockSpec(wb.shape, const),
                  pl.BlockSpec((CMP_HIDDEN, HEAD_DIM), const), pl.BlockSpec((1, HEAD_DIM), const)],
        out_specs=pl.BlockSpec((1, nch, NSA_KV * HEAD_DIM), lambda b: (b, 0, 0)),
        out_shape=jax.ShapeDtypeStruct((batch, nch, NSA_KV * HEAD_DIM), BF16),
        compiler_params=_params(("parallel",)),
        name="nsa_compress",
    )(x, pea, peb, wa, wb, w2.astype(BF16), g)


def _cmp_to_sel_matrix(nch, n_sel_pad):
    n_cmp = nch - 1
    c0 = np.arange(nch)[:, None] * CMP_STRIDE
    s0 = np.arange(n_sel_pad)[None, :] * SEL_BLOCK
    ov = np.clip(np.minimum(c0 + CMP_LEN, s0 + SEL_BLOCK) - np.maximum(c0, s0), 0, None) / CMP_LEN
    ov[n_cmp:] = 0.0
    return jnp.asarray(ov.T, dtype=BF16)


def _nsa_cmp_kernel(q_ref, kc_ref, vc_ref, mt_ref, eye_ref, tile_of_ref, ocmp_ref, selb_ref, flags_ref,
                    *, tq, nch, n_sel):
    g = pl.program_id(1)
    i = pl.program_id(2)
    t = i * tq + lax.broadcasted_iota(jnp.int32, (tq, 1), 0)
    c_end = lax.broadcasted_iota(jnp.int32, (1, nch), 1) * CMP_STRIDE + (CMP_LEN - 1)
    vis = c_end <= t
    distc = t.astype(F32) - (c_end.astype(F32) - (CMP_LEN - 1) / 2.0)
    live = (t >= CMP_LEN - 1).astype(F32)
    kc = kc_ref[0]
    vc = vc_ref[0]
    gsc = _group_slope_scale(g)
    psum = jnp.zeros((tq, nch), F32)
    for j in range(HEADS_PER_GROUP):
        slope = (2.0 ** -(j + 1)) * gsc
        sl = slice(j * HEAD_DIM, (j + 1) * HEAD_DIM)
        s = _dot_nt(q_ref[:, sl], kc) - slope * distc
        s = jnp.where(vis, s, NEG)
        e = jnp.exp(s - jnp.max(s, axis=-1, keepdims=True))
        p = e * (live / jnp.sum(e, axis=-1, keepdims=True))
        psum = psum + p
        ocmp_ref[:, sl] = _dot(p.astype(BF16), vc)

    imp = _dot_nt(mt_ref[...], psum.astype(BF16))
    blk = lax.broadcasted_iota(jnp.int32, (LANES, 1), 0)
    blk_t = _div_pow2(i * tq + lax.broadcasted_iota(jnp.int32, (1, tq), 1), SEL_BLOCK)
    valid = blk <= blk_t
    forced = (blk == 0) | (blk == blk_t) | (blk == blk_t - 1)
    vals = jnp.where(valid & jnp.logical_not(forced) & (blk < n_sel), imp, -1.0)
    sel = valid & forced
    for _ in range(SEL_TOPN - 3):
        mx = jnp.max(vals, axis=0, keepdims=True)
        first = jnp.min(jnp.where(vals == mx, blk, LANES), axis=0, keepdims=True)
        pick = blk == first
        sel = sel | pick
        vals = jnp.where(pick, -3.0, vals)
    sel = sel & valid
    masked = jnp.where(sel, 0.0, NEG).astype(BF16)
    selb_ref[...] = _dot_nt(eye_ref[...], masked).astype(selb_ref.dtype)

    t_lane = i * tq + lax.broadcasted_iota(jnp.int32, (1, tq), 1)
    early = blk < (SEGMENT // SEL_BLOCK) * _div_pow2(t_lane, SEGMENT)
    per_query = _dot(tile_of_ref[...], jnp.where(sel & early, 1.0, 0.0).astype(BF16))
    out_lane = lax.broadcasted_iota(jnp.int32, flags_ref.shape[1:], 1)
    flags = jnp.zeros(flags_ref.shape[1:], jnp.int32)
    for h in range(tq // SEGMENT):
        hit = jnp.max(per_query[:, h * SEGMENT:(h + 1) * SEGMENT], axis=1, keepdims=True) > 0.0
        flags = jnp.where((out_lane == h) & hit, 1, flags)
    flags_ref[0] = flags


SEGMENT = 256
TILE_ROWS = 16


def _nsa_cmp(qkn, kc, vc, batch, seq, tq, tk):
    t = batch * seq
    nch = seq // CMP_STRIDE
    n_sel = seq // SEL_BLOCK
    n_all = seq // tk
    assert n_sel <= LANES and n_all <= TILE_ROWS and tq % SEGMENT == 0
    nq = seq // tq
    gw = HEADS_PER_GROUP * HEAD_DIM
    mt = _cmp_to_sel_matrix(nch, LANES)
    eye = jnp.asarray(np.eye(tq, dtype=np.float32), dtype=BF16)
    tile_of = jnp.asarray(np.arange(LANES)[None, :] // (tk // SEL_BLOCK) == np.arange(TILE_ROWS)[:, None], dtype=BF16)
    row = lambda b, g, i: (b * nq + i, g)
    ocmp, selb, flags = pl.pallas_call(
        functools.partial(_nsa_cmp_kernel, tq=tq, nch=nch, n_sel=n_sel),
        grid=(batch, NSA_KV, nq),
        in_specs=[pl.BlockSpec((tq, gw), row),
                  pl.BlockSpec((1, nch, HEAD_DIM), lambda b, g, i: (b, 0, g)),
                  pl.BlockSpec((1, nch, HEAD_DIM), lambda b, g, i: (b, 0, g)),
                  pl.BlockSpec((LANES, nch), lambda b, g, i: (0, 0)),
                  pl.BlockSpec((tq, tq), lambda b, g, i: (0, 0)),
                  pl.BlockSpec((TILE_ROWS, LANES), lambda b, g, i: (0, 0))],
        out_specs=[pl.BlockSpec((tq, gw), row), pl.BlockSpec((tq, LANES), row),
                   pl.BlockSpec((1, TILE_ROWS, LANES), lambda b, g, i: ((b * NSA_KV + g) * nq + i, 0, 0))],
        out_shape=[jax.ShapeDtypeStruct((t, NSA_HEADS * HEAD_DIM), F32),
                   jax.ShapeDtypeStruct((t, NSA_KV * LANES), BF16),
                   jax.ShapeDtypeStruct((batch * NSA_KV * nq, TILE_ROWS, LANES), jnp.int32)],
        compiler_params=_params(("parallel", "parallel", "parallel")),
        name="nsa_cmp_select",
    )(qkn, kc, vc, mt, eye, tile_of)
    segs = tq // SEGMENT
    tile_flags = flags[:, :n_all, :segs].transpose(0, 2, 1).reshape(batch, NSA_KV, nq * segs, n_all)
    return ocmp, selb, tile_flags


def _stack_rows(t0, tq):
    r = lax.broadcasted_iota(jnp.int32, (HEADS_PER_GROUP * tq, 1), 0)
    return t0 + (r & (tq - 1))


def _key_aux_blocks(seq):
    j = np.arange(seq)
    aux = (j[:, None] // SEL_BLOCK == np.arange(LANES)[None, :]).astype(np.float32)
    aux[:, 0] = j % SEL_BLOCK
    return jnp.asarray(aux, dtype=BF16)


_POS_SPLIT = 256


def _key_aux_position(seq):
    assert seq <= _POS_SPLIT * _POS_SPLIT
    j = np.arange(seq)
    aux = np.zeros((seq, LANES), np.float32)
    aux[:, 0] = j % _POS_SPLIT
    aux[:, 1] = j - j % _POS_SPLIT
    return jnp.asarray(aux, dtype=BF16)


def _query_aug(q_ref, aux_cols):
    parts = [jnp.concatenate([q_ref[:, j * HEAD_DIM:(j + 1) * HEAD_DIM], aux_cols[j].astype(BF16)], axis=1)
             for j in range(HEADS_PER_GROUP)]
    return jnp.concatenate(parts, axis=0)


def _flash_step(s, v, carry):
    m, l, acc = carry
    m_new = jnp.maximum(m, jnp.max(s, axis=-1, keepdims=True))
    alpha = jnp.exp(m - m_new)
    p = jnp.exp(s - m_new).astype(BF16)
    pv = _dot(p, jnp.concatenate([v, jnp.ones_like(v)], axis=1))
    l = alpha * l + pv[:, HEAD_DIM:HEAD_DIM + 1]
    acc = alpha * acc + pv[:, :HEAD_DIM]
    return m_new, l, acc


def _flash_init(rows):
    return (jnp.full((rows, 1), NEG, F32), jnp.zeros((rows, 1), F32), jnp.zeros((rows, HEAD_DIM), F32))


def _pipelined_tiles(n, n_all, scores_fn, consume_fn, s_a, s_b, carry):
    s_a[...] = scores_fn(0)

    def pairs(k, count, carry):
        for c in range(count):
            s_b[...] = scores_fn(k + 2 * c + 1)
            carry = consume_fn(k + 2 * c, s_a[...], carry)
            s_a[...] = scores_fn(jnp.minimum(k + 2 * c + 2, n_all - 1))
            carry = consume_fn(k + 2 * c + 1, s_b[...], carry)
        return carry

    quads = lax.shift_right_logical(n, 2)
    carry = lax.fori_loop(0, quads, lambda u, c: pairs(4 * u, 2, c), carry)
    done = 4 * quads
    carry = lax.cond((n & 2) == 2, lambda c: pairs(done, 1, c), lambda c: c, carry)
    return lax.cond((n & 1) == 1, lambda c: consume_fn(n - 1, s_a[...], c), lambda c: c, carry)


def _nsa_main_kernel(tiles_ref, count_ref, q_ref, ks_ref, vs_ref, kw_ref, vw_ref, kaux_ref, kpos_ref, selb_ref,
                     ocmp_ref, gate_ref, z_ref, o_ref, s_a, s_b, *, tq, tk, seq):
    b = pl.program_id(0)
    g = pl.program_id(1)
    i = pl.program_id(2)
    n_all = seq // tk
    t0 = pl.multiple_of(i * tq, tq)
    rows = HEADS_PER_GROUP * tq
    gsc = _group_slope_scale(g)
    t_rows = _stack_rows(t0, tq)
    slopes = [(2.0 ** -(j + 1)) * gsc for j in range(HEADS_PER_GROUP)]

    lane = lax.broadcasted_iota(jnp.int32, (tq, LANES), 1)
    chosen = selb_ref[...].astype(F32) > -1.0
    own_first = _div_pow2(t0, SEL_BLOCK)
    block_start = (lane * SEL_BLOCK).astype(F32)

    def masked_queries(blocks):
        return _query_aug(q_ref, [jnp.where(lane == 0, sl, jnp.where(chosen & blocks, sl * block_start, NEG))
                                  for sl in slopes])

    q_aug = masked_queries(lane < own_first)

    slot = (b * NSA_KV + g) * (seq // tq) + i
    base = slot * n_all

    def scores(u):
        j0 = pl.multiple_of(tiles_ref[base + u] * tk, tk)
        return _dot_nt(q_aug, jnp.concatenate([ks_ref[pl.ds(j0, tk), :], kaux_ref[pl.ds(j0, tk), :]], axis=1))

    def consume(u, s, carry):
        j0 = pl.multiple_of(tiles_ref[base + u] * tk, tk)
        return _flash_step(s, vs_ref[pl.ds(j0, tk), :], carry)

    carry = _pipelined_tiles(count_ref[slot], n_all, scores, consume, s_a, s_b, _flash_init(rows))

    j_own = t0 + lax.broadcasted_iota(jnp.int32, (1, tq), 1)
    s = _dot_nt(masked_queries(lane >= own_first),
                jnp.concatenate([ks_ref[pl.ds(t0, tq), :], kaux_ref[pl.ds(t0, tq), :]], axis=1))
    s = jnp.where(j_own <= t_rows, s, NEG)
    _, l_s, acc_s = _flash_step(s, vs_ref[pl.ds(t0, tq), :], carry)
    o_slc = acc_s / l_s

    q_pos = _query_aug(q_ref, [jnp.where(lane < 2, sl, 0.0) for sl in slopes])
    wk = WINDOW + tq
    start = pl.multiple_of(jnp.maximum(t0 - WINDOW, 0), LANES)
    j_win = start + lax.broadcasted_iota(jnp.int32, (1, wk), 1)
    s = _dot_nt(q_pos, jnp.concatenate([kw_ref[pl.ds(start, wk), :], kpos_ref[pl.ds(start, wk), :]], axis=1))
    s = jnp.where((j_win <= t_rows) & (j_win > t_rows - WINDOW), s, NEG)
    e = jnp.exp(s - jnp.max(s, axis=-1, keepdims=True))
    o_win = _dot(e.astype(BF16), vw_ref[pl.ds(start, wk), :]) / jnp.sum(e, axis=-1, keepdims=True)

    for j in range(HEADS_PER_GROUP):
        sl = slice(j * HEAD_DIM, (j + 1) * HEAD_DIM)
        rs = slice(j * tq, (j + 1) * tq)
        gates = [_sigmoid(gate_ref[:, 3 * j + c:3 * j + c + 1]) for c in range(3)]
        o = gates[0] * ocmp_ref[:, sl] + gates[1] * o_slc[rs] + gates[2] * o_win[rs]
        o_ref[:, sl] = (o * z_ref[:, sl]).astype(o_ref.dtype)


def _nsa_main(qkn, vals, selb, tile_flags, ocmp, gates, zact, batch, seq, tq, tk):
    t = batch * seq
    nq = seq // tq
    n_all = seq // tk
    gw = HEADS_PER_GROUP * HEAD_DIM
    flags = tile_flags.reshape(batch * NSA_KV * nq, n_all)
    tiles = jnp.argsort(1 - flags, axis=-1, stable=True).astype(jnp.int32).reshape(-1)
    counts = jnp.sum(flags, axis=-1).astype(jnp.int32)
    row = lambda b, g, i, *_: (b * nq + i, g)
    ks_blk = (NSA_HEADS * HEAD_DIM) // HEAD_DIM
    kw_blk = ks_blk + NSA_KV
    grid_spec = pltpu.PrefetchScalarGridSpec(
        num_scalar_prefetch=2,
        grid=(batch, NSA_KV, nq),
        in_specs=[pl.BlockSpec((tq, gw), row),
                  pl.BlockSpec((seq, HEAD_DIM), lambda b, g, i, *_: (b, ks_blk + g)),
                  pl.BlockSpec((seq, HEAD_DIM), lambda b, g, i, *_: (b, g)),
                  pl.BlockSpec((seq, HEAD_DIM), lambda b, g, i, *_: (b, kw_blk + g)),
                  pl.BlockSpec((seq, HEAD_DIM), lambda b, g, i, *_: (b, NSA_KV + g)),
                  pl.BlockSpec((seq, LANES), lambda b, g, i, *_: (0, 0)),
                  pl.BlockSpec((seq, LANES), lambda b, g, i, *_: (0, 0)),
                  pl.BlockSpec((tq, LANES), row),
                  pl.BlockSpec((tq, gw), row),
                  pl.BlockSpec((tq, LANES), row),
                  pl.BlockSpec((tq, gw), row)],
        out_specs=pl.BlockSpec((tq, gw), row),
        scratch_shapes=[pltpu.VMEM((HEADS_PER_GROUP * tq, tk), F32)] * 2,
    )
    return pl.pallas_call(
        functools.partial(_nsa_main_kernel, tq=tq, tk=tk, seq=seq),
        grid_spec=grid_spec,
        out_shape=jax.ShapeDtypeStruct((t, NSA_HEADS * HEAD_DIM), BF16),
        compiler_params=_params(("parallel", "parallel", "arbitrary")),
        name="nsa_select_window",
    )(tiles, counts, qkn, qkn, vals, qkn, vals, _key_aux_blocks(seq), _key_aux_position(seq), selb, ocmp, gates,
      zact)


def _sortable_key(x):
    bits = pltpu.bitcast(x, jnp.int32)
    bits = jnp.where(bits == INT_MIN, 0, bits)
    return jnp.where(bits < 0, bits ^ 0x7FFFFFFF, bits)


def _dsa_index_kernel(qi_ref, ki_ref, wi_ref, eye_ref, tri_ref, bias_ref, key_scr, *, tq, tk, seq, topk):
    i = pl.program_id(1)
    t0 = i * tq
    n_all = seq // tk
    n_kt = _div_pow2(t0 + tq + tk - 1, tk)
    t = t0 + lax.broadcasted_iota(jnp.int32, (1, tq), 1)
    wrow = wi_ref[...] * (IDX_HEADS ** -0.5 * IDX_DIM ** -0.5)
    hb = 4

    def key_ids(kt):
        return kt * tk + lax.broadcasted_iota(jnp.int32, (tk, 1), 0)

    def score_tile(kt):
        j0 = pl.multiple_of(kt * tk, tk)
        ki = ki_ref[pl.ds(j0, tk), :]
        score = jnp.zeros((tk, tq), F32)
        for h0 in range(0, IDX_HEADS, hb):
            x = _dot_nt(ki, qi_ref[0, h0:h0 + hb].reshape(hb * tq, IDX_DIM))
            for h in range(hb):
                score = score + jnp.maximum(x[:, h * tq:(h + 1) * tq], 0.0) * wrow[h0 + h:h0 + h + 1, :]
        key_scr[kt] = jnp.where(key_ids(kt) <= t, _sortable_key(score), INT_MIN)

    n_pairs = lax.shift_right_logical(n_kt + 1, 1)

    def score_step(u, _):
        score_tile(2 * u)
        score_tile(2 * u + 1)
        return 0

    lax.fori_loop(0, n_pairs, score_step, 0)

    def count(pred):
        def step(u, c):
            for kt in (2 * u, 2 * u + 1):
                hit = jnp.where(pred(key_scr[kt]), 1, 0)
                c = c + jnp.sum(hit.reshape(tk // 8, 8, tq), axis=0)
            return c
        c = lax.fori_loop(0, n_pairs, step, jnp.zeros((8, tq), jnp.int32))
        return jnp.sum(c, axis=0, keepdims=True)

    everything = jnp.int32(seq + tk)
    c0 = count(lambda k: k >= 0)
    state = (jnp.where(c0 >= topk, 0, INT_MIN), jnp.where(c0 >= topk, c0, everything))

    def bit_step(b, state):
        v, n_ge = state
        cand = v | (jnp.int32(1) << (30 - b))
        c = count(lambda k: k >= cand)
        return jnp.where(c >= topk, cand, v), jnp.where(c >= topk, c, n_ge)

    def settled(state):
        done = (state[1] == topk) | (t < topk)
        return jnp.min(jnp.where(done, 1, 0)) > 0

    state = lax.fori_loop(0, RADIX_CHECKS[0], bit_step, state)
    for lo, hi in zip(RADIX_CHECKS, RADIX_CHECKS[1:] + (31,)):
        state = lax.cond(settled(state), lambda s: s, lambda s, lo=lo, hi=hi: lax.fori_loop(lo, hi, bit_step, s), state)
    v, n_ge = state
    def emit(kt, keep):
        masked = jnp.where(keep & (key_ids(kt) <= t), 0.0, NEG).astype(BF16)
        bias_ref[0, 0, kt] = _dot_nt(eye_ref[...], masked).astype(bias_ref.dtype)

    def fill_step(kt, _):
        bias_ref[0, 0, kt] = jnp.full((tq, tk), NEG, bias_ref.dtype)
        return 0

    def emit_no_ties(_):
        def step(u, c):
            for kt in (2 * u, 2 * u + 1):
                emit(kt, key_scr[kt] >= v)
            return c
        return lax.fori_loop(0, n_pairs, step, 0)

    def emit_with_ties(_):
        need = (topk - count(lambda k: k > v)).astype(F32)

        def step(kt, carry):
            key = key_scr[kt]
            eq = key == v
            eqf = jnp.where(eq, 1.0, 0.0)
            before = carry + _dot(tri_ref[...], eqf.astype(BF16)) - eqf
            emit(kt, (key > v) | (eq & (before < need)))
            return carry + jnp.sum(eqf, axis=0, keepdims=True)

        lax.fori_loop(0, n_kt, step, jnp.zeros((1, tq), F32))
        return lax.fori_loop(n_kt, 2 * n_pairs, fill_step, 0)

    over = (n_ge > topk) & (v > INT_MIN)
    lax.cond(jnp.max(jnp.where(over, 1, 0)) > 0, emit_with_ties, emit_no_ties, 0)
    lax.fori_loop(2 * n_pairs, n_all, fill_step, 0)


def _dsa_index(qi_heads, ki, wi_t, batch, seq, tq, tk):
    nq = seq // tq
    n_all = seq // tk
    topk = min(DSA_TOPK_MAX, seq // 4)
    tri = jnp.asarray(np.tril(np.ones((tk, tk), np.float32)), dtype=BF16)
    eye = jnp.asarray(np.eye(tq, dtype=np.float32), dtype=BF16)
    return pl.pallas_call(
        functools.partial(_dsa_index_kernel, tq=tq, tk=tk, seq=seq, topk=topk),
        grid=(batch, nq),
        in_specs=[pl.BlockSpec((1, IDX_HEADS, tq, IDX_DIM), lambda b, i: (b, 0, i, 0)),
                  pl.BlockSpec((seq, IDX_DIM), lambda b, i: (b, 0)),
                  pl.BlockSpec((IDX_HEADS, tq), lambda b, i: (0, b * nq + i)),
                  pl.BlockSpec((tq, tq), lambda b, i: (0, 0)),
                  pl.BlockSpec((tk, tk), lambda b, i: (0, 0))],
        out_specs=pl.BlockSpec((1, 1, n_all, tq, tk), lambda b, i: (b, i, 0, 0, 0)),
        out_shape=jax.ShapeDtypeStruct((batch, nq, n_all, tq, tk), BF16),
        scratch_shapes=[pltpu.VMEM((n_all, tk, tq), jnp.int32)],
        compiler_params=_params(("parallel", "arbitrary")),
        name="dsa_index_topk",
    )(qi_heads, ki, wi_t, eye, tri)


def _dsa_attn_kernel(q_ref, k_ref, v_ref, kaux_ref, bias_ref, z_ref, o_ref, s_a, s_b, *, tq, tk, seq):
    g = pl.program_id(1)
    i = pl.program_id(2)
    t0 = i * tq
    rows = HEADS_PER_GROUP * tq
    gsc = _group_slope_scale(g)
    slopes = [(2.0 ** -(j + 1)) * gsc for j in range(HEADS_PER_GROUP)]
    lane = lax.broadcasted_iota(jnp.int32, (tq, LANES), 1)
    q_aug = _query_aug(q_ref, [jnp.where(lane < 2, sl, 0.0) for sl in slopes])

    def scores(kt):
        j0 = pl.multiple_of(kt * tk, tk)
        return _dot_nt(q_aug, jnp.concatenate([k_ref[pl.ds(j0, tk), :], kaux_ref[pl.ds(j0, tk), :]], axis=1))

    def consume(kt, s, carry):
        j0 = pl.multiple_of(kt * tk, tk)
        bias = [bias_ref[0, part, kt].astype(F32) for part in range(bias_ref.shape[1])]
        s = s + jnp.concatenate(bias * HEADS_PER_GROUP, axis=0)
        return _flash_step(s, v_ref[pl.ds(j0, tk), :], carry)

    n_kt = _div_pow2(t0 + tq + tk - 1, tk)
    _, l, acc = _pipelined_tiles(n_kt, seq // tk, scores, consume, s_a, s_b, _flash_init(rows))
    o = acc / l
    for j in range(HEADS_PER_GROUP):
        sl = slice(j * HEAD_DIM, (j + 1) * HEAD_DIM)
        o_ref[:, sl] = (o[j * tq:(j + 1) * tq] * z_ref[:, sl]).astype(o_ref.dtype)


def _dsa_attn(qkn, vals, bias, zact, batch, seq, tq, tk):
    t = batch * seq
    nq = seq // tq
    n_all = seq // tk
    gw = HEADS_PER_GROUP * HEAD_DIM
    row = lambda b, g, i: (b * nq + i, g)
    q_blk = (NSA_HEADS + 2 * NSA_KV) * HEAD_DIM // gw
    k_blk = (NSA_HEADS + 2 * NSA_KV + DSA_HEADS)
    return pl.pallas_call(
        functools.partial(_dsa_attn_kernel, tq=tq, tk=tk, seq=seq),
        grid=(batch, DSA_KV, nq),
        in_specs=[pl.BlockSpec((tq, gw), lambda b, g, i: (b * nq + i, q_blk + g)),
                  pl.BlockSpec((seq, HEAD_DIM), lambda b, g, i: (b, k_blk + g)),
                  pl.BlockSpec((seq, HEAD_DIM), lambda b, g, i: (b, 2 * NSA_KV + g)),
                  pl.BlockSpec((seq, LANES), lambda b, g, i: (0, 0)),
                  pl.BlockSpec((1, tq // bias.shape[3]) + bias.shape[2:], lambda b, g, i: (b, i, 0, 0, 0)),
                  pl.BlockSpec((tq, gw), lambda b, g, i: (b * nq + i, NSA_KV + g))],
        out_specs=pl.BlockSpec((tq, gw), row),
        out_shape=jax.ShapeDtypeStruct((t, DSA_HEADS * HEAD_DIM), BF16),
        scratch_shapes=[pltpu.VMEM((HEADS_PER_GROUP * tq, tk), F32)] * 2,
        compiler_params=_params(("parallel", "parallel", "arbitrary")),
        name="dsa_attention",
    )(qkn, qkn, vals, _key_aux_position(seq), bias, zact)


def _merge_kernel(on_ref, od_ref, sn_ref, sd_ref, x_ref, wun_ref, wud_ref, wout_ref, o_ref):
    y = sn_ref[...] * _dot(on_ref[...], wun_ref[...]) + sd_ref[...] * _dot(od_ref[...], wud_ref[...])
    o_ref[...] = x_ref[...] + _dot(y.astype(BF16), wout_ref[...])


def _merge(o_nsa, o_dsa, gate_act, x, w_up_nsa, w_up_dsa, w_out, tm):
    t, d = x.shape
    const = lambda i: (0, 0)
    return pl.pallas_call(
        _merge_kernel,
        grid=(t // tm,),
        in_specs=[pl.BlockSpec((tm, o_nsa.shape[1]), lambda i: (i, 0)),
                  pl.BlockSpec((tm, o_dsa.shape[1]), lambda i: (i, 0)),
                  pl.BlockSpec((tm, d), lambda i: (i, 0)),
                  pl.BlockSpec((tm, d), lambda i: (i, 1)),
                  pl.BlockSpec((tm, d), lambda i: (i, 0)),
                  pl.BlockSpec(w_up_nsa.shape, const), pl.BlockSpec(w_up_dsa.shape, const),
                  pl.BlockSpec(w_out.shape, const)],
        out_specs=pl.BlockSpec((tm, d), lambda i: (i, 0)),
        out_shape=jax.ShapeDtypeStruct((t, d), F32),
        compiler_params=_params(("parallel",)),
        name="merge_out_proj",
    )(o_nsa, o_dsa, gate_act, gate_act, x, w_up_nsa, w_up_dsa, w_out)


def _ple_kernel(x_ref, p_ref, g_ref, wg_ref, wp_ref, o_ref):
    x = x_ref[...]
    ms = jnp.mean(x * x, axis=-1, keepdims=True)
    r = (x * lax.rsqrt(ms + EPS) * g_ref[...]).astype(BF16)
    gate = _sigmoid(_dot(r, wg_ref[...]))
    o_ref[...] = x + _dot(p_ref[...].astype(BF16), wp_ref[...]) * gate


def _ple(x, p, g, w_gate, w_proj, tm):
    t, d = x.shape
    const = lambda i: (0, 0)
    return pl.pallas_call(
        _ple_kernel,
        grid=(t // tm,),
        in_specs=[pl.BlockSpec((tm, d), lambda i: (i, 0)),
                  pl.BlockSpec((tm, p.shape[1]), lambda i: (i, 0)),
                  pl.BlockSpec((1, d), const),
                  pl.BlockSpec(w_gate.shape, const), pl.BlockSpec(w_proj.shape, const)],
        out_specs=pl.BlockSpec((tm, d), lambda i: (i, 0)),
        out_shape=jax.ShapeDtypeStruct((t, d), F32),
        compiler_params=_params(("parallel",)),
        name="ple_gate",
    )(x, p, g.reshape(1, d), w_gate, w_proj)


def _split_in_proj(w):
    widths = [NSA_HEADS * HEAD_DIM] + [NSA_KV * HEAD_DIM] * 6 + [NSA_HEADS * 3, NSA_HEADS * HEAD_DIM,
              DSA_HEADS * HEAD_DIM, DSA_KV * HEAD_DIM, DSA_KV * HEAD_DIM, IDX_HEADS * IDX_DIM, IDX_DIM,
              IDX_HEADS, DSA_HEADS * HEAD_DIM, D_MODEL, D_MODEL]
    offs = np.concatenate([[0], np.cumsum(widths)])
    assert offs[-1] == w.shape[1]
    (nq, nkc, nvc, nks, nvs, nkw, nvw, ng, nz, dq, dk, dv, iq, ik, iw, dz, mgn, mgd) = [
        w[:, offs[k]:offs[k + 1]] for k in range(len(widths))]
    pad = jnp.zeros((w.shape[0], LANES - IDX_DIM - IDX_HEADS - NSA_HEADS * 3), w.dtype)
    return dict(
        normed=jnp.concatenate([nq, nks, nkw, dq, dk], axis=1).astype(BF16),
        vals=jnp.concatenate([nvs, nvw, dv], axis=1).astype(BF16),
        iq=iq.astype(BF16),
        raw=jnp.concatenate([nkc, nvc, ik, iw, ng, pad], axis=1).astype(BF16),
        zact=jnp.concatenate([nz, dz], axis=1).astype(BF16),
        merge=jnp.concatenate([mgn, mgd], axis=1).astype(BF16),
    )


def _layer(x, p, norm_g, w_in, nsa_q_g, nsa_kc_g, nsa_ks_g, nsa_kw_g, cmp_pe_k, cmp_w1_k, cmp_w2_k,
           cmp_pe_v, cmp_w1_v, cmp_w2_v, dsa_q_g, dsa_k_g, w_up_nsa, w_up_dsa, w_out, ple_norm_g,
           w_ple_gate, w_ple_proj):
    batch, seq, d = x.shape
    t = batch * seq
    assert seq % SEGMENT == 0 and seq >= WINDOW + SEGMENT
    x2 = x.reshape(t, d)
    tm = min(1024, t)

    w = _split_in_proj(w_in)
    h = _rmsnorm(x2, norm_g, min(512, t))

    gains = jnp.concatenate([jnp.tile(nsa_q_g * ATTN_SCALE, NSA_HEADS), jnp.tile(nsa_ks_g, NSA_KV),
                             jnp.tile(nsa_kw_g, NSA_KV), jnp.tile(dsa_q_g * ATTN_SCALE, DSA_HEADS),
                             jnp.tile(dsa_k_g, DSA_KV)])
    qkn = _proj(h, w["normed"], tm, w["normed"].shape[1] // 2, BF16, gain=gains)
    vals = _proj(h, w["vals"], tm, w["vals"].shape[1], BF16)
    qi_heads = _proj_heads(h, w["iq"], min(tm, seq), batch, seq)
    kc_raw, vc_raw, misc = _proj_split(h, w["raw"], tm, (NSA_KV * HEAD_DIM, NSA_KV * HEAD_DIM, LANES))
    zact = _proj(h, w["zact"], tm, 1024, F32, act="silu")
    gate_act = _proj(h, w["merge"], tm, 1024, F32, act="sigmoid")

    ki = misc[:, :IDX_DIM].astype(BF16)
    wi = misc[:, IDX_DIM:IDX_DIM + IDX_HEADS]
    ng = misc[:, IDX_DIM + IDX_HEADS:IDX_DIM + IDX_HEADS + NSA_HEADS * 3]
    per_group = HEADS_PER_GROUP * 3
    gates = jnp.pad(ng.reshape(t, NSA_KV, per_group), ((0, 0), (0, 0), (0, LANES - per_group))).reshape(t, NSA_KV * LANES)

    kc = _compress(kc_raw, cmp_pe_k, cmp_w1_k, cmp_w2_k, nsa_kc_g, batch, seq)
    vc = _compress(vc_raw, cmp_pe_v, cmp_w1_v, cmp_w2_v, None, batch, seq)

    tq = 128
    tk = min(512, seq)
    ocmp, selb, tile_flags = _nsa_cmp(qkn, kc, vc, batch, seq, min(256, seq), tk)
    o_nsa = _nsa_main(qkn, vals, selb, tile_flags, ocmp, gates, zact, batch, seq, SEGMENT, tk)

    bias = _dsa_index(qi_heads, ki, wi.T, batch, seq, tq, tk)
    o_dsa = _dsa_attn(qkn, vals, bias, zact, batch, seq, 2 * tq, tk)

    x1 = _merge(o_nsa, o_dsa, gate_act, x2, w_up_nsa.astype(BF16), w_up_dsa.astype(BF16), w_out.astype(BF16),
                min(256, t))
    x3 = _ple(x1, p.reshape(t, PLE_DIM), ple_norm_g, w_ple_gate.astype(BF16), w_ple_proj.astype(BF16), min(256, t))
    return x3.reshape(batch, seq, d)


def kernel(x, p, norm_g, w_in, nsa_q_g, nsa_kc_g, nsa_ks_g, nsa_kw_g, cmp_pe_k, cmp_w1_k, cmp_w2_k, cmp_pe_v, cmp_w1_v, cmp_w2_v, dsa_q_g, dsa_k_g, w_up_nsa, w_up_dsa, w_out, ple_norm_g, w_ple_gate, w_ple_proj):
    depth = w_in.shape[0]
    for i in range(depth):
        x = _layer(x, p[i], norm_g[i], w_in[i], nsa_q_g[i], nsa_kc_g[i], nsa_ks_g[i], nsa_kw_g[i],
                   cmp_pe_k[i], cmp_w1_k[i], cmp_w2_k[i], cmp_pe_v[i], cmp_w1_v[i], cmp_w2_v[i],
                   dsa_q_g[i], dsa_k_g[i], w_up_nsa[i], w_up_dsa[i], w_out[i], ple_norm_g[i],
                   w_ple_gate[i], w_ple_proj[i])
    return x
```

```python
import functools
from typing import NamedTuple

import numpy as np
import jax
import jax.numpy as jnp
from jax import lax
from jax.experimental import pallas as pl
from jax.experimental.pallas import tpu as pltpu

D_MODEL = 2048
HEAD_DIM = 128
NSA_HEADS = 8
NSA_KV = 2
DSA_HEADS = 8
DSA_KV = 2
HEADS_PER_GROUP = 4
CMP_LEN = 32
CMP_STRIDE = 16
CMP_HIDDEN = 256
SEL_BLOCK = 64
SEL_TOPN = 16
WINDOW = 512
IDX_HEADS = 16
IDX_DIM = 64
DSA_TOPK_MAX = 256
PLE_DIM = 256
EPS = 1e-6
NEG = -1e30
ATTN_SCALE = HEAD_DIM ** -0.5
LANES = 128
INT_MIN = -(2 ** 31)
RADIX_CHECKS = (24, 27)
BF16_EXACT_INT = 256

VMEM_LIMIT_BYTES = 56 * 1024 * 1024

F32 = jnp.float32
BF16 = jnp.bfloat16

SEGMENT = 256
INDEX_QUERIES = 128
TILE_ROWS = 16


class _Tiles(NamedTuple):
    norm_rows: int
    proj_rows: int
    proj_cols: int
    out_rows: int
    cmp_queries: int
    keys: int

    @classmethod
    def for_shape(cls, batch, seq):
        t = batch * seq
        return cls(norm_rows=min(512, t), proj_rows=min(1024, t), proj_cols=1024, out_rows=min(256, t),
                   cmp_queries=min(256, seq), keys=min(512, seq))


def _params(semantics):
    return pltpu.CompilerParams(dimension_semantics=semantics, vmem_limit_bytes=VMEM_LIMIT_BYTES)


def _sigmoid(x):
    return 1.0 / (1.0 + jnp.exp(-x))


def _dot(a, b):
    return jnp.dot(a, b, preferred_element_type=F32)


def _dot_nt(a, b):
    return lax.dot_general(a, b, (((1,), (1,)), ((), ())), preferred_element_type=F32)


def _log2(n):
    assert n > 0 and n & (n - 1) == 0, n
    return n.bit_length() - 1


def _div_pow2(x, n):
    return lax.shift_right_logical(x, jnp.int32(_log2(n)))


def _group_slope_scale(g):
    return jnp.where(g == 0, 1.0, 2.0 ** -HEADS_PER_GROUP).astype(F32)


def _rmsnorm_kernel(x_ref, g_ref, o_ref):
    x = x_ref[...]
    ms = jnp.mean(x * x, axis=-1, keepdims=True)
    o_ref[...] = (x * lax.rsqrt(ms + EPS) * g_ref[...]).astype(o_ref.dtype)


def _rmsnorm(x, g, tm):
    t, d = x.shape
    return pl.pallas_call(
        _rmsnorm_kernel,
        grid=(t // tm,),
        in_specs=[pl.BlockSpec((tm, d), lambda i: (i, 0)), pl.BlockSpec((1, d), lambda i: (0, 0))],
        out_specs=pl.BlockSpec((tm, d), lambda i: (i, 0)),
        out_shape=jax.ShapeDtypeStruct((t, d), BF16),
        compiler_params=_params(("parallel",)),
        name="in_rmsnorm",
    )(x, g.reshape(1, d))


def _proj_headnorm_kernel(h_ref, w_ref, g_ref, o_ref):
    y = _dot(h_ref[...], w_ref[...])
    for hd in range(y.shape[1] // HEAD_DIM):
        sl = slice(hd * HEAD_DIM, (hd + 1) * HEAD_DIM)
        yh = y[:, sl]
        ms = jnp.mean(yh * yh, axis=-1, keepdims=True)
        o_ref[:, sl] = (yh * lax.rsqrt(ms + EPS) * g_ref[:, sl]).astype(o_ref.dtype)


def _proj_act_kernel(h_ref, w_ref, o_ref, *, act):
    y = _dot(h_ref[...], w_ref[...])
    if act == "silu":
        y = y * _sigmoid(y)
    elif act == "sigmoid":
        y = _sigmoid(y)
    o_ref[...] = y.astype(o_ref.dtype)


def _proj_split_kernel(h_ref, w_ref, *o_refs):
    y = _dot(h_ref[...], w_ref[...])
    off = 0
    for o_ref in o_refs:
        n = o_ref.shape[1]
        o_ref[...] = y[:, off:off + n].astype(o_ref.dtype)
        off += n


def _proj(h, w, tm, tn, out_dtype, act=None, gain=None):
    t, d = h.shape
    n = w.shape[1]
    in_specs = [pl.BlockSpec((tm, d), lambda i, j: (i, 0)), pl.BlockSpec((d, tn), lambda i, j: (0, j))]
    args = [h, w]
    if gain is not None:
        body = _proj_headnorm_kernel
        in_specs.append(pl.BlockSpec((1, tn), lambda i, j: (0, j)))
        args.append(gain.reshape(1, n))
    else:
        body = functools.partial(_proj_act_kernel, act=act)
    return pl.pallas_call(
        body,
        grid=(t // tm, n // tn),
        in_specs=in_specs,
        out_specs=pl.BlockSpec((tm, tn), lambda i, j: (i, j)),
        out_shape=jax.ShapeDtypeStruct((t, n), out_dtype),
        compiler_params=_params(("parallel", "arbitrary")),
        name="in_proj_" + ("headnorm" if gain is not None else str(act)),
    )(*args)


def _proj_heads_kernel(h_ref, w_ref, o_ref):
    y = _dot(h_ref[...], w_ref[...])
    for hd in range(o_ref.shape[1]):
        o_ref[0, hd] = y[:, hd * IDX_DIM:(hd + 1) * IDX_DIM].astype(o_ref.dtype)


def _proj_heads(h, w, tm, batch, seq):
    t, d = h.shape
    per_batch = seq // tm
    return pl.pallas_call(
        _proj_heads_kernel,
        grid=(t // tm,),
        in_specs=[pl.BlockSpec((tm, d), lambda i: (i, 0)), pl.BlockSpec(w.shape, lambda i: (0, 0))],
        out_specs=pl.BlockSpec((1, IDX_HEADS, tm, IDX_DIM), lambda i: (i // per_batch, 0, i % per_batch, 0)),
        out_shape=jax.ShapeDtypeStruct((batch, IDX_HEADS, seq, IDX_DIM), BF16),
        compiler_params=_params(("parallel",)),
        name="in_proj_heads",
    )(h, w)


def _proj_split(h, w, tm, widths):
    t, d = h.shape
    n = w.shape[1]
    return pl.pallas_call(
        _proj_split_kernel,
        grid=(t // tm,),
        in_specs=[pl.BlockSpec((tm, d), lambda i: (i, 0)), pl.BlockSpec((d, n), lambda i: (0, 0))],
        out_specs=[pl.BlockSpec((tm, wd), lambda i: (i, 0)) for wd in widths],
        out_shape=[jax.ShapeDtypeStruct((t, wd), F32) for wd in widths],
        compiler_params=_params(("parallel",)),
        name="in_proj_split",
    )(h, w)


def _compress_kernel(x_ref, pea_ref, peb_ref, wa_ref, wb_ref, w2_ref, g_ref, o_ref, *, normalize):
    x = x_ref[...]
    nch = x.shape[0]
    ha = _dot((x + pea_ref[...]).astype(BF16), wa_ref[...])
    hb = _dot((x + peb_ref[...]).astype(BF16), wb_ref[...])
    hid = ha + pltpu.roll(hb, nch - 1, axis=0)
    act = (hid * _sigmoid(hid)).astype(BF16)
    for g in range(NSA_KV):
        o = _dot(act[:, g * CMP_HIDDEN:(g + 1) * CMP_HIDDEN], w2_ref[...])
        if normalize:
            ms = jnp.mean(o * o, axis=-1, keepdims=True)
            o = o * lax.rsqrt(ms + EPS) * g_ref[...]
        o_ref[0, :, g * HEAD_DIM:(g + 1) * HEAD_DIM] = o.astype(o_ref.dtype)


def _expand_compress_params(pe, w1):
    half = CMP_LEN // 2
    w = w1.reshape(CMP_LEN, HEAD_DIM, CMP_HIDDEN)
    eye = jnp.eye(NSA_KV, dtype=w1.dtype)

    def expand_w(wh):
        return jnp.einsum("ldj,hg->lhdgj", wh, eye).reshape(half * NSA_KV * HEAD_DIM, NSA_KV * CMP_HIDDEN)

    def expand_pe(ph):
        return jnp.broadcast_to(ph[:, None, :], (half, NSA_KV, HEAD_DIM)).reshape(1, half * NSA_KV * HEAD_DIM)

    return (expand_pe(pe[:half]), expand_pe(pe[half:]),
            expand_w(w[:half]).astype(BF16), expand_w(w[half:]).astype(BF16))


def _compress(raw, pe, w1, w2, gain, batch, seq):
    nch = seq // CMP_STRIDE
    width = CMP_STRIDE * NSA_KV * HEAD_DIM
    x = raw.reshape(batch * nch, width)
    pea, peb, wa, wb = _expand_compress_params(pe, w1)
    normalize = gain is not None
    g = (gain if normalize else jnp.ones((HEAD_DIM,), F32)).reshape(1, HEAD_DIM)
    const = lambda b: (0, 0)
    return pl.pallas_call(
        functools.partial(_compress_kernel, normalize=normalize),
        grid=(batch,),
        in_specs=[pl.BlockSpec((nch, width), lambda b: (b, 0)),
                  pl.BlockSpec((1, width), const), pl.BlockSpec((1, width), const),
                  pl.BlockSpec(wa.shape, const), pl.BlockSpec(wb.shape, const),
                  pl.BlockSpec((CMP_HIDDEN, HEAD_DIM), const), pl.BlockSpec((1, HEAD_DIM), const)],
        out_specs=pl.BlockSpec((1, nch, NSA_KV * HEAD_DIM), lambda b: (b, 0, 0)),
        out_shape=jax.ShapeDtypeStruct((batch, nch, NSA_KV * HEAD_DIM), BF16),
        compiler_params=_params(("parallel",)),
        name="nsa_compress",
    )(x, pea, peb, wa, wb, w2.astype(BF16), g)


def _cmp_to_sel_matrix(nch, n_sel_pad):
    n_cmp = nch - 1
    c0 = np.arange(nch)[:, None] * CMP_STRIDE
    s0 = np.arange(n_sel_pad)[None, :] * SEL_BLOCK
    ov = np.clip(np.minimum(c0 + CMP_LEN, s0 + SEL_BLOCK) - np.maximum(c0, s0), 0, None) / CMP_LEN
    ov[n_cmp:] = 0.0
    return jnp.asarray(ov.T, dtype=BF16)


def _nsa_cmp_kernel(q_ref, kc_ref, vc_ref, mt_ref, eye_ref, tile_of_ref, ocmp_ref, selb_ref, flags_ref,
                    *, tq, nch, n_sel):
    g = pl.program_id(1)
    i = pl.program_id(2)
    t = i * tq + lax.broadcasted_iota(jnp.int32, (tq, 1), 0)
    live = (t >= CMP_LEN - 1).astype(F32)
    gsc = _group_slope_scale(g)

    def attend(width):
        c_end = lax.broadcasted_iota(jnp.int32, (1, width), 1) * CMP_STRIDE + (CMP_LEN - 1)
        vis = c_end <= t
        distc = t.astype(F32) - (c_end.astype(F32) - (CMP_LEN - 1) / 2.0)
        kc = kc_ref[0, :width, :]
        vc = vc_ref[0, :width, :]
        psum = jnp.zeros((tq, width), F32)
        for j in range(HEADS_PER_GROUP):
            slope = (2.0 ** -(j + 1)) * gsc
            sl = slice(j * HEAD_DIM, (j + 1) * HEAD_DIM)
            s = _dot_nt(q_ref[:, sl], kc) - slope * distc
            s = jnp.where(vis, s, NEG)
            e = jnp.exp(s - jnp.max(s, axis=-1, keepdims=True))
            p = e * (live / jnp.sum(e, axis=-1, keepdims=True))
            psum = psum + p
            ocmp_ref[:, sl] = _dot(p.astype(BF16), vc)
        return _dot_nt(mt_ref[:, :width], psum.astype(BF16))

    widths = sorted({min(nch, LANES * (k + 1)) for k in range(pl.cdiv(nch, LANES))})
    visible = ((i + 1) * tq - CMP_LEN) // CMP_STRIDE + 1
    imp = lax.switch(jnp.minimum((visible - 1) // LANES, len(widths) - 1),
                     [functools.partial(attend, w) for w in widths])
    blk = lax.broadcasted_iota(jnp.int32, (LANES, 1), 0)
    blk_t = _div_pow2(i * tq + lax.broadcasted_iota(jnp.int32, (1, tq), 1), SEL_BLOCK)
    valid = blk <= blk_t
    forced = (blk == 0) | (blk == blk_t) | (blk == blk_t - 1)
    vals = jnp.where(valid & jnp.logical_not(forced) & (blk < n_sel), imp, -1.0)
    sel = valid & forced
    for _ in range(SEL_TOPN - 3):
        mx = jnp.max(vals, axis=0, keepdims=True)
        first = jnp.min(jnp.where(vals == mx, blk, LANES), axis=0, keepdims=True)
        pick = blk == first
        sel = sel | pick
        vals = jnp.where(pick, -3.0, vals)
    sel = sel & valid
    masked = jnp.where(sel, 0.0, NEG).astype(BF16)
    selb_ref[...] = _dot_nt(eye_ref[...], masked).astype(selb_ref.dtype)

    t_lane = i * tq + lax.broadcasted_iota(jnp.int32, (1, tq), 1)
    early = blk < (SEGMENT // SEL_BLOCK) * _div_pow2(t_lane, SEGMENT)
    per_query = _dot(tile_of_ref[...], jnp.where(sel & early, 1.0, 0.0).astype(BF16))
    out_lane = lax.broadcasted_iota(jnp.int32, flags_ref.shape[1:], 1)
    flags = jnp.zeros(flags_ref.shape[1:], jnp.int32)
    for h in range(tq // SEGMENT):
        hit = jnp.max(per_query[:, h * SEGMENT:(h + 1) * SEGMENT], axis=1, keepdims=True) > 0.0
        flags = jnp.where((out_lane == h) & hit, 1, flags)
    flags_ref[0] = flags


def _nsa_cmp(qkn, kc, vc, batch, seq, tq, tk):
    t = batch * seq
    nch = seq // CMP_STRIDE
    n_sel = seq // SEL_BLOCK
    n_all = seq // tk
    assert n_sel <= LANES and n_all <= TILE_ROWS and tq % SEGMENT == 0
    nq = seq // tq
    gw = HEADS_PER_GROUP * HEAD_DIM
    mt = _cmp_to_sel_matrix(nch, LANES)
    eye = jnp.asarray(np.eye(tq, dtype=np.float32), dtype=BF16)
    tile_of = jnp.asarray(np.arange(LANES)[None, :] // (tk // SEL_BLOCK) == np.arange(TILE_ROWS)[:, None], dtype=BF16)
    row = lambda b, g, i: (b * nq + i, g)
    ocmp, selb, flags = pl.pallas_call(
        functools.partial(_nsa_cmp_kernel, tq=tq, nch=nch, n_sel=n_sel),
        grid=(batch, NSA_KV, nq),
        in_specs=[pl.BlockSpec((tq, gw), row),
                  pl.BlockSpec((1, nch, HEAD_DIM), lambda b, g, i: (b, 0, g)),
                  pl.BlockSpec((1, nch, HEAD_DIM), lambda b, g, i: (b, 0, g)),
                  pl.BlockSpec((LANES, nch), lambda b, g, i: (0, 0)),
                  pl.BlockSpec((tq, tq), lambda b, g, i: (0, 0)),
                  pl.BlockSpec((TILE_ROWS, LANES), lambda b, g, i: (0, 0))],
        out_specs=[pl.BlockSpec((tq, gw), row), pl.BlockSpec((tq, LANES), row),
                   pl.BlockSpec((1, TILE_ROWS, LANES), lambda b, g, i: ((b * NSA_KV + g) * nq + i, 0, 0))],
        out_shape=[jax.ShapeDtypeStruct((t, NSA_HEADS * HEAD_DIM), F32),
                   jax.ShapeDtypeStruct((t, NSA_KV * LANES), BF16),
                   jax.ShapeDtypeStruct((batch * NSA_KV * nq, TILE_ROWS, LANES), jnp.int32)],
        compiler_params=_params(("parallel", "parallel", "parallel")),
        name="nsa_cmp_select",
    )(qkn, kc, vc, mt, eye, tile_of)
    segs = tq // SEGMENT
    tile_flags = flags[:, :n_all, :segs].transpose(0, 2, 1).reshape(batch, NSA_KV, nq * segs, n_all)
    return ocmp, selb, tile_flags


def _stack_rows(t0, tq):
    r = lax.broadcasted_iota(jnp.int32, (HEADS_PER_GROUP * tq, 1), 0)
    return t0 + (r & (tq - 1))


def _key_aux_blocks(seq):
    j = np.arange(seq)
    aux = (j[:, None] // SEL_BLOCK == np.arange(LANES)[None, :]).astype(np.float32)
    aux[:, 0] = j % SEL_BLOCK
    return jnp.asarray(aux, dtype=BF16)


def _key_aux_position(seq):
    assert seq <= BF16_EXACT_INT * BF16_EXACT_INT
    j = np.arange(seq)
    aux = np.zeros((seq, LANES), np.float32)
    aux[:, 0] = j % BF16_EXACT_INT
    aux[:, 1] = j - j % BF16_EXACT_INT
    return jnp.asarray(aux, dtype=BF16)


def _query_aug(q_ref, aux_cols):
    parts = [jnp.concatenate([q_ref[:, j * HEAD_DIM:(j + 1) * HEAD_DIM], aux_cols[j].astype(BF16)], axis=1)
             for j in range(HEADS_PER_GROUP)]
    return jnp.concatenate(parts, axis=0)


def _flash_step(s, v, carry):
    m, l, acc = carry
    m_new = jnp.maximum(m, jnp.max(s, axis=-1, keepdims=True))
    alpha = jnp.exp(m - m_new)
    p = jnp.exp(s - m_new).astype(BF16)
    pv = _dot(p, jnp.concatenate([v, jnp.ones_like(v)], axis=1))
    l = alpha * l + pv[:, HEAD_DIM:HEAD_DIM + 1]
    acc = alpha * acc + pv[:, :HEAD_DIM]
    return m_new, l, acc


def _flash_init(rows):
    return (jnp.full((rows, 1), NEG, F32), jnp.zeros((rows, 1), F32), jnp.zeros((rows, HEAD_DIM), F32))


def _pipelined_tiles(n, n_all, scores_fn, consume_fn, s_a, s_b, carry):
    s_a[...] = scores_fn(0)

    def pairs(k, count, carry):
        for c in range(count):
            s_b[...] = scores_fn(k + 2 * c + 1)
            carry = consume_fn(k + 2 * c, s_a[...], carry)
            s_a[...] = scores_fn(jnp.minimum(k + 2 * c + 2, n_all - 1))
            carry = consume_fn(k + 2 * c + 1, s_b[...], carry)
        return carry

    quads = lax.shift_right_logical(n, 2)
    carry = lax.fori_loop(0, quads, lambda u, c: pairs(4 * u, 2, c), carry)
    done = 4 * quads
    carry = lax.cond((n & 2) == 2, lambda c: pairs(done, 1, c), lambda c: c, carry)
    return lax.cond((n & 1) == 1, lambda c: consume_fn(n - 1, s_a[...], c), lambda c: c, carry)


def _nsa_main_kernel(tiles_ref, count_ref, q_ref, ks_ref, vs_ref, kw_ref, vw_ref, kaux_ref, kpos_ref, selb_ref,
                     ocmp_ref, gate_ref, z_ref, o_ref, s_a, s_b, *, tq, tk, seq):
    b = pl.program_id(0)
    g = pl.program_id(1)
    i = pl.program_id(2)
    n_all = seq // tk
    t0 = pl.multiple_of(i * tq, tq)
    rows = HEADS_PER_GROUP * tq
    gsc = _group_slope_scale(g)
    t_rows = _stack_rows(t0, tq)
    slopes = [(2.0 ** -(j + 1)) * gsc for j in range(HEADS_PER_GROUP)]

    lane = lax.broadcasted_iota(jnp.int32, (tq, LANES), 1)
    chosen = selb_ref[...].astype(F32) > -1.0
    own_first = _div_pow2(t0, SEL_BLOCK)
    block_start = (lane * SEL_BLOCK).astype(F32)

    def masked_queries(blocks):
        return _query_aug(q_ref, [jnp.where(lane == 0, sl, jnp.where(chosen & blocks, sl * block_start, NEG))
                                  for sl in slopes])

    q_aug = masked_queries(lane < own_first)

    slot = (b * NSA_KV + g) * (seq // tq) + i
    base = slot * n_all

    def scores(u):
        j0 = pl.multiple_of(tiles_ref[base + u] * tk, tk)
        return _dot_nt(q_aug, jnp.concatenate([ks_ref[pl.ds(j0, tk), :], kaux_ref[pl.ds(j0, tk), :]], axis=1))

    def consume(u, s, carry):
        j0 = pl.multiple_of(tiles_ref[base + u] * tk, tk)
        return _flash_step(s, vs_ref[pl.ds(j0, tk), :], carry)

    carry = _pipelined_tiles(count_ref[slot], n_all, scores, consume, s_a, s_b, _flash_init(rows))

    j_own = t0 + lax.broadcasted_iota(jnp.int32, (1, tq), 1)
    s = _dot_nt(masked_queries(lane >= own_first),
                jnp.concatenate([ks_ref[pl.ds(t0, tq), :], kaux_ref[pl.ds(t0, tq), :]], axis=1))
    s = jnp.where(j_own <= t_rows, s, NEG)
    _, l_s, acc_s = _flash_step(s, vs_ref[pl.ds(t0, tq), :], carry)
    o_slc = acc_s / l_s

    q_pos = _query_aug(q_ref, [jnp.where(lane < 2, sl, 0.0) for sl in slopes])
    wk = WINDOW + tq
    start = pl.multiple_of(jnp.maximum(t0 - WINDOW, 0), LANES)
    j_win = start + lax.broadcasted_iota(jnp.int32, (1, wk), 1)
    s = _dot_nt(q_pos, jnp.concatenate([kw_ref[pl.ds(start, wk), :], kpos_ref[pl.ds(start, wk), :]], axis=1))
    s = jnp.where((j_win <= t_rows) & (j_win > t_rows - WINDOW), s, NEG)
    e = jnp.exp(s - jnp.max(s, axis=-1, keepdims=True))
    o_win = _dot(e.astype(BF16), vw_ref[pl.ds(start, wk), :]) / jnp.sum(e, axis=-1, keepdims=True)

    for j in range(HEADS_PER_GROUP):
        sl = slice(j * HEAD_DIM, (j + 1) * HEAD_DIM)
        rs = slice(j * tq, (j + 1) * tq)
        gates = [_sigmoid(gate_ref[:, 3 * j + c:3 * j + c + 1]) for c in range(3)]
        o = gates[0] * ocmp_ref[:, sl] + gates[1] * o_slc[rs] + gates[2] * o_win[rs]
        o_ref[:, sl] = (o * z_ref[:, sl]).astype(o_ref.dtype)


def _nsa_main(qkn, vals, selb, tile_flags, ocmp, gates, zact, batch, seq, tq, tk):
    t = batch * seq
    nq = seq // tq
    n_all = seq // tk
    gw = HEADS_PER_GROUP * HEAD_DIM
    flags = tile_flags.reshape(batch * NSA_KV * nq, n_all)
    tiles = jnp.argsort(1 - flags, axis=-1, stable=True).astype(jnp.int32).reshape(-1)
    counts = jnp.sum(flags, axis=-1).astype(jnp.int32)
    row = lambda b, g, i, *_: (b * nq + i, g)
    ks_blk = (NSA_HEADS * HEAD_DIM) // HEAD_DIM
    kw_blk = ks_blk + NSA_KV
    grid_spec = pltpu.PrefetchScalarGridSpec(
        num_scalar_prefetch=2,
        grid=(batch, NSA_KV, nq),
        in_specs=[pl.BlockSpec((tq, gw), row),
                  pl.BlockSpec((seq, HEAD_DIM), lambda b, g, i, *_: (b, ks_blk + g)),
                  pl.BlockSpec((seq, HEAD_DIM), lambda b, g, i, *_: (b, g)),
                  pl.BlockSpec((seq, HEAD_DIM), lambda b, g, i, *_: (b, kw_blk + g)),
                  pl.BlockSpec((seq, HEAD_DIM), lambda b, g, i, *_: (b, NSA_KV + g)),
                  pl.BlockSpec((seq, LANES), lambda b, g, i, *_: (0, 0)),
                  pl.BlockSpec((seq, LANES), lambda b, g, i, *_: (0, 0)),
                  pl.BlockSpec((tq, LANES), row),
                  pl.BlockSpec((tq, gw), row),
                  pl.BlockSpec((tq, LANES), row),
                  pl.BlockSpec((tq, gw), row)],
        out_specs=pl.BlockSpec((tq, gw), row),
        scratch_shapes=[pltpu.VMEM((HEADS_PER_GROUP * tq, tk), F32)] * 2,
    )
    return pl.pallas_call(
        functools.partial(_nsa_main_kernel, tq=tq, tk=tk, seq=seq),
        grid_spec=grid_spec,
        out_shape=jax.ShapeDtypeStruct((t, NSA_HEADS * HEAD_DIM), BF16),
        compiler_params=_params(("parallel", "parallel", "arbitrary")),
        name="nsa_select_window",
    )(tiles, counts, qkn, qkn, vals, qkn, vals, _key_aux_blocks(seq), _key_aux_position(seq), selb, ocmp, gates,
      zact)


def _threshold_value(code):
    return pltpu.bitcast(jnp.where(code < 0, code ^ 0x7FFFFFFF, code), F32)


def _dsa_index_kernel(qi_ref, ki_ref, wi_ref, eye_ref, tri_ref, bias_ref, score_scr, *, tq, tk, seq, topk):
    i = pl.program_id(1)
    t0 = i * tq
    n_all = seq // tk
    n_kt = _div_pow2(t0 + tq + tk - 1, tk)
    t = t0 + lax.broadcasted_iota(jnp.int32, (1, tq), 1)
    wrow = wi_ref[...] * (IDX_HEADS ** -0.5 * IDX_DIM ** -0.5)
    hb = 4

    def key_ids(kt):
        return kt * tk + lax.broadcasted_iota(jnp.int32, (tk, 1), 0)

    def score_tile(kt):
        j0 = pl.multiple_of(kt * tk, tk)
        ki = ki_ref[pl.ds(j0, tk), :]
        score = jnp.zeros((tk, tq), F32)
        for h0 in range(0, IDX_HEADS, hb):
            x = _dot_nt(ki, qi_ref[0, h0:h0 + hb].reshape(hb * tq, IDX_DIM))
            for h in range(hb):
                score = score + jnp.maximum(x[:, h * tq:(h + 1) * tq], 0.0) * wrow[h0 + h:h0 + h + 1, :]
        score_scr[kt] = jnp.where(key_ids(kt) <= t, score, -jnp.inf)

    n_pairs = lax.shift_right_logical(n_kt + 1, 1)

    def score_step(u, _):
        score_tile(2 * u)
        score_tile(2 * u + 1)
        return 0

    lax.fori_loop(0, n_pairs, score_step, 0)

    def count(pred):
        def step(u, c):
            for kt in (2 * u, 2 * u + 1):
                hit = jnp.where(pred(score_scr[kt]), 1, 0)
                c = c + jnp.sum(hit.reshape(tk // 8, 8, tq), axis=0)
            return c
        c = lax.fori_loop(0, n_pairs, step, jnp.zeros((8, tq), jnp.int32))
        return jnp.sum(c, axis=0, keepdims=True)

    everything = jnp.int32(seq + tk)
    c0 = count(lambda s: s >= 0.0)
    state = (jnp.where(c0 >= topk, 0, INT_MIN), jnp.where(c0 >= topk, c0, everything))

    def bit_step(b, state):
        v, n_ge = state
        cand = v | (jnp.int32(1) << (30 - b))
        threshold = _threshold_value(cand)
        c = count(lambda s: s >= threshold)
        return jnp.where(c >= topk, cand, v), jnp.where(c >= topk, c, n_ge)

    def settled(state):
        done = (state[1] == topk) | (t < topk)
        return jnp.min(jnp.where(done, 1, 0)) > 0

    state = lax.fori_loop(0, RADIX_CHECKS[0], bit_step, state)
    for lo, hi in zip(RADIX_CHECKS, RADIX_CHECKS[1:] + (31,)):
        state = lax.cond(settled(state), lambda s: s, lambda s, lo=lo, hi=hi: lax.fori_loop(lo, hi, bit_step, s), state)
    v, n_ge = state
    threshold = _threshold_value(v)
    take_all = v == INT_MIN

    def emit(kt, keep):
        masked = jnp.where(keep & (key_ids(kt) <= t), 0.0, NEG).astype(BF16)
        bias_ref[0, 0, kt] = _dot_nt(eye_ref[...], masked).astype(bias_ref.dtype)

    def fill_step(kt, _):
        bias_ref[0, 0, kt] = jnp.full((tq, tk), NEG, bias_ref.dtype)
        return 0

    def emit_no_ties(_):
        def step(u, c):
            for kt in (2 * u, 2 * u + 1):
                emit(kt, (score_scr[kt] >= threshold) | take_all)
            return c
        return lax.fori_loop(0, n_pairs, step, 0)

    def emit_with_ties(_):
        need = (topk - count(lambda s: s > threshold)).astype(F32)

        def step(kt, carry):
            score = score_scr[kt]
            eq = score == threshold
            eqf = jnp.where(eq, 1.0, 0.0)
            before = carry + _dot(tri_ref[...], eqf.astype(BF16)) - eqf
            emit(kt, (score > threshold) | (eq & (before < need)) | take_all)
            return carry + jnp.sum(eqf, axis=0, keepdims=True)

        lax.fori_loop(0, n_kt, step, jnp.zeros((1, tq), F32))
        return lax.fori_loop(n_kt, 2 * n_pairs, fill_step, 0)

    over = (n_ge > topk) & jnp.logical_not(take_all)
    lax.cond(jnp.max(jnp.where(over, 1, 0)) > 0, emit_with_ties, emit_no_ties, 0)
    lax.fori_loop(2 * n_pairs, n_all, fill_step, 0)


def _dsa_index(qi_heads, ki, wi_t, batch, seq, tq, tk):
    nq = seq // tq
    n_all = seq // tk
    topk = min(DSA_TOPK_MAX, seq // 4)
    tri = jnp.asarray(np.tril(np.ones((tk, tk), np.float32)), dtype=BF16)
    eye = jnp.asarray(np.eye(tq, dtype=np.float32), dtype=BF16)
    return pl.pallas_call(
        functools.partial(_dsa_index_kernel, tq=tq, tk=tk, seq=seq, topk=topk),
        grid=(batch, nq),
        in_specs=[pl.BlockSpec((1, IDX_HEADS, tq, IDX_DIM), lambda b, i: (b, 0, i, 0)),
                  pl.BlockSpec((seq, IDX_DIM), lambda b, i: (b, 0)),
                  pl.BlockSpec((IDX_HEADS, tq), lambda b, i: (0, b * nq + i)),
                  pl.BlockSpec((tq, tq), lambda b, i: (0, 0)),
                  pl.BlockSpec((tk, tk), lambda b, i: (0, 0))],
        out_specs=pl.BlockSpec((1, 1, n_all, tq, tk), lambda b, i: (b, i, 0, 0, 0)),
        out_shape=jax.ShapeDtypeStruct((batch, nq, n_all, tq, tk), BF16),
        scratch_shapes=[pltpu.VMEM((n_all, tk, tq), F32)],
        compiler_params=_params(("parallel", "arbitrary")),
        name="dsa_index_topk",
    )(qi_heads, ki, wi_t, eye, tri)


def _dsa_attn_kernel(q_ref, k_ref, v_ref, kaux_ref, bias_ref, z_ref, o_ref, s_a, s_b, *, tq, tk, seq):
    g = pl.program_id(1)
    i = pl.program_id(2)
    t0 = i * tq
    rows = HEADS_PER_GROUP * tq
    gsc = _group_slope_scale(g)
    slopes = [(2.0 ** -(j + 1)) * gsc for j in range(HEADS_PER_GROUP)]
    lane = lax.broadcasted_iota(jnp.int32, (tq, LANES), 1)
    q_aug = _query_aug(q_ref, [jnp.where(lane < 2, sl, 0.0) for sl in slopes])

    def scores(kt):
        j0 = pl.multiple_of(kt * tk, tk)
        return _dot_nt(q_aug, jnp.concatenate([k_ref[pl.ds(j0, tk), :], kaux_ref[pl.ds(j0, tk), :]], axis=1))

    def consume(kt, s, carry):
        j0 = pl.multiple_of(kt * tk, tk)
        bias = [bias_ref[0, part, kt].astype(F32) for part in range(bias_ref.shape[1])]
        s = s + jnp.concatenate(bias * HEADS_PER_GROUP, axis=0)
        return _flash_step(s, v_ref[pl.ds(j0, tk), :], carry)

    n_kt = _div_pow2(t0 + tq + tk - 1, tk)
    _, l, acc = _pipelined_tiles(n_kt, seq // tk, scores, consume, s_a, s_b, _flash_init(rows))
    o = acc / l
    for j in range(HEADS_PER_GROUP):
        sl = slice(j * HEAD_DIM, (j + 1) * HEAD_DIM)
        o_ref[:, sl] = (o[j * tq:(j + 1) * tq] * z_ref[:, sl]).astype(o_ref.dtype)


def _dsa_attn(qkn, vals, bias, zact, batch, seq, tq, tk):
    t = batch * seq
    nq = seq // tq
    n_all = seq // tk
    gw = HEADS_PER_GROUP * HEAD_DIM
    row = lambda b, g, i: (b * nq + i, g)
    q_blk = (NSA_HEADS + 2 * NSA_KV) * HEAD_DIM // gw
    k_blk = (NSA_HEADS + 2 * NSA_KV + DSA_HEADS)
    return pl.pallas_call(
        functools.partial(_dsa_attn_kernel, tq=tq, tk=tk, seq=seq),
        grid=(batch, DSA_KV, nq),
        in_specs=[pl.BlockSpec((tq, gw), lambda b, g, i: (b * nq + i, q_blk + g)),
                  pl.BlockSpec((seq, HEAD_DIM), lambda b, g, i: (b, k_blk + g)),
                  pl.BlockSpec((seq, HEAD_DIM), lambda b, g, i: (b, 2 * NSA_KV + g)),
                  pl.BlockSpec((seq, LANES), lambda b, g, i: (0, 0)),
                  pl.BlockSpec((1, tq // bias.shape[3]) + bias.shape[2:], lambda b, g, i: (b, i, 0, 0, 0)),
                  pl.BlockSpec((tq, gw), lambda b, g, i: (b * nq + i, NSA_KV + g))],
        out_specs=pl.BlockSpec((tq, gw), row),
        out_shape=jax.ShapeDtypeStruct((t, DSA_HEADS * HEAD_DIM), BF16),
        scratch_shapes=[pltpu.VMEM((HEADS_PER_GROUP * tq, tk), F32)] * 2,
        compiler_params=_params(("parallel", "parallel", "arbitrary")),
        name="dsa_attention",
    )(qkn, qkn, vals, _key_aux_position(seq), bias, zact)


def _merge_kernel(on_ref, od_ref, sn_ref, sd_ref, x_ref, wun_ref, wud_ref, wout_ref, o_ref):
    y = sn_ref[...] * _dot(on_ref[...], wun_ref[...]) + sd_ref[...] * _dot(od_ref[...], wud_ref[...])
    o_ref[...] = x_ref[...] + _dot(y.astype(BF16), wout_ref[...])


def _merge(o_nsa, o_dsa, gate_act, x, w_up_nsa, w_up_dsa, w_out, tm):
    t, d = x.shape
    const = lambda i: (0, 0)
    return pl.pallas_call(
        _merge_kernel,
        grid=(t // tm,),
        in_specs=[pl.BlockSpec((tm, o_nsa.shape[1]), lambda i: (i, 0)),
                  pl.BlockSpec((tm, o_dsa.shape[1]), lambda i: (i, 0)),
                  pl.BlockSpec((tm, d), lambda i: (i, 0)),
                  pl.BlockSpec((tm, d), lambda i: (i, 1)),
                  pl.BlockSpec((tm, d), lambda i: (i, 0)),
                  pl.BlockSpec(w_up_nsa.shape, const), pl.BlockSpec(w_up_dsa.shape, const),
                  pl.BlockSpec(w_out.shape, const)],
        out_specs=pl.BlockSpec((tm, d), lambda i: (i, 0)),
        out_shape=jax.ShapeDtypeStruct((t, d), F32),
        compiler_params=_params(("parallel",)),
        name="merge_out_proj",
    )(o_nsa, o_dsa, gate_act, gate_act, x, w_up_nsa, w_up_dsa, w_out)


def _ple_kernel(x_ref, p_ref, g_ref, wg_ref, wp_ref, o_ref):
    x = x_ref[...]
    ms = jnp.mean(x * x, axis=-1, keepdims=True)
    r = (x * lax.rsqrt(ms + EPS) * g_ref[...]).astype(BF16)
    gate = _sigmoid(_dot(r, wg_ref[...]))
    o_ref[...] = x + _dot(p_ref[...].astype(BF16), wp_ref[...]) * gate


def _ple(x, p, g, w_gate, w_proj, tm):
    t, d = x.shape
    const = lambda i: (0, 0)
    return pl.pallas_call(
        _ple_kernel,
        grid=(t // tm,),
        in_specs=[pl.BlockSpec((tm, d), lambda i: (i, 0)),
                  pl.BlockSpec((tm, p.shape[1]), lambda i: (i, 0)),
                  pl.BlockSpec((1, d), const),
                  pl.BlockSpec(w_gate.shape, const), pl.BlockSpec(w_proj.shape, const)],
        out_specs=pl.BlockSpec((tm, d), lambda i: (i, 0)),
        out_shape=jax.ShapeDtypeStruct((t, d), F32),
        compiler_params=_params(("parallel",)),
        name="ple_gate",
    )(x, p, g.reshape(1, d), w_gate, w_proj)


def _split_in_proj(w):
    widths = [NSA_HEADS * HEAD_DIM] + [NSA_KV * HEAD_DIM] * 6 + [NSA_HEADS * 3, NSA_HEADS * HEAD_DIM,
              DSA_HEADS * HEAD_DIM, DSA_KV * HEAD_DIM, DSA_KV * HEAD_DIM, IDX_HEADS * IDX_DIM, IDX_DIM,
              IDX_HEADS, DSA_HEADS * HEAD_DIM, D_MODEL, D_MODEL]
    offs = np.concatenate([[0], np.cumsum(widths)])
    assert offs[-1] == w.shape[1]
    (nq, nkc, nvc, nks, nvs, nkw, nvw, ng, nz, dq, dk, dv, iq, ik, iw, dz, mgn, mgd) = [
        w[:, offs[k]:offs[k + 1]] for k in range(len(widths))]
    pad = jnp.zeros((w.shape[0], LANES - IDX_DIM - IDX_HEADS - NSA_HEADS * 3), w.dtype)
    return dict(
        normed=jnp.concatenate([nq, nks, nkw, dq, dk], axis=1).astype(BF16),
        vals=jnp.concatenate([nvs, nvw, dv], axis=1).astype(BF16),
        iq=iq.astype(BF16),
        raw=jnp.concatenate([nkc, nvc, ik, iw, ng, pad], axis=1).astype(BF16),
        zact=jnp.concatenate([nz, dz], axis=1).astype(BF16),
        merge=jnp.concatenate([mgn, mgd], axis=1).astype(BF16),
    )


def _layer(x, p, norm_g, w_in, nsa_q_g, nsa_kc_g, nsa_ks_g, nsa_kw_g, cmp_pe_k, cmp_w1_k, cmp_w2_k,
           cmp_pe_v, cmp_w1_v, cmp_w2_v, dsa_q_g, dsa_k_g, w_up_nsa, w_up_dsa, w_out, ple_norm_g,
           w_ple_gate, w_ple_proj):
    batch, seq, d = x.shape
    t = batch * seq
    assert seq % SEGMENT == 0 and seq >= WINDOW + SEGMENT
    x2 = x.reshape(t, d)
    tiles = _Tiles.for_shape(batch, seq)
    tm = tiles.proj_rows

    w = _split_in_proj(w_in)
    h = _rmsnorm(x2, norm_g, tiles.norm_rows)

    gains = jnp.concatenate([jnp.tile(nsa_q_g * ATTN_SCALE, NSA_HEADS), jnp.tile(nsa_ks_g, NSA_KV),
                             jnp.tile(nsa_kw_g, NSA_KV), jnp.tile(dsa_q_g * ATTN_SCALE, DSA_HEADS),
                             jnp.tile(dsa_k_g, DSA_KV)])
    qkn = _proj(h, w["normed"], tm, w["normed"].shape[1] // 2, BF16, gain=gains)
    vals = _proj(h, w["vals"], tm, w["vals"].shape[1], BF16)
    qi_heads = _proj_heads(h, w["iq"], min(tm, seq), batch, seq)
    kc_raw, vc_raw, misc = _proj_split(h, w["raw"], tm, (NSA_KV * HEAD_DIM, NSA_KV * HEAD_DIM, LANES))
    zact = _proj(h, w["zact"], tm, tiles.proj_cols, F32, act="silu")
    gate_act = _proj(h, w["merge"], tm, tiles.proj_cols, F32, act="sigmoid")

    ki = misc[:, :IDX_DIM].astype(BF16)
    wi = misc[:, IDX_DIM:IDX_DIM + IDX_HEADS]
    ng = misc[:, IDX_DIM + IDX_HEADS:IDX_DIM + IDX_HEADS + NSA_HEADS * 3]
    per_group = HEADS_PER_GROUP * 3
    gates = jnp.pad(ng.reshape(t, NSA_KV, per_group), ((0, 0), (0, 0), (0, LANES - per_group))).reshape(t, NSA_KV * LANES)

    kc = _compress(kc_raw, cmp_pe_k, cmp_w1_k, cmp_w2_k, nsa_kc_g, batch, seq)
    vc = _compress(vc_raw, cmp_pe_v, cmp_w1_v, cmp_w2_v, None, batch, seq)

    tk = tiles.keys
    ocmp, selb, tile_flags = _nsa_cmp(qkn, kc, vc, batch, seq, tiles.cmp_queries, tk)
    o_nsa = _nsa_main(qkn, vals, selb, tile_flags, ocmp, gates, zact, batch, seq, SEGMENT, tk)

    bias = _dsa_index(qi_heads, ki, wi.T, batch, seq, INDEX_QUERIES, tk)
    o_dsa = _dsa_attn(qkn, vals, bias, zact, batch, seq, SEGMENT, tk)

    x1 = _merge(o_nsa, o_dsa, gate_act, x2, w_up_nsa.astype(BF16), w_up_dsa.astype(BF16), w_out.astype(BF16),
                tiles.out_rows)
    x3 = _ple(x1, p.reshape(t, PLE_DIM), ple_norm_g, w_ple_gate.astype(BF16), w_ple_proj.astype(BF16),
              tiles.out_rows)
    return x3.reshape(batch, seq, d)


def kernel(x, p, norm_g, w_in, nsa_q_g, nsa_kc_g, nsa_ks_g, nsa_kw_g, cmp_pe_k, cmp_w1_k, cmp_w2_k, cmp_pe_v, cmp_w1_v, cmp_w2_v, dsa_q_g, dsa_k_g, w_up_nsa, w_up_dsa, w_out, ple_norm_g, w_ple_gate, w_ple_proj):
    depth = w_in.shape[0]
    for i in range(depth):
        x = _layer(x, p[i], norm_g[i], w_in[i], nsa_q_g[i], nsa_kc_g[i], nsa_ks_g[i], nsa_kw_g[i],
                   cmp_pe_k[i], cmp_w1_k[i], cmp_w2_k[i], cmp_pe_v[i], cmp_w1_v[i], cmp_w2_v[i],
                   dsa_q_g[i], dsa_k_g[i], w_up_nsa[i], w_up_dsa[i], w_out[i], ple_norm_g[i],
                   w_ple_gate[i], w_ple_proj[i])
    return x
```

```python
import functools
from typing import NamedTuple

import numpy as np
import jax
import jax.numpy as jnp
from jax import lax
from jax.experimental import pallas as pl
from jax.experimental.pallas import tpu as pltpu

D_MODEL = 2048
HEAD_DIM = 128
NSA_HEADS = 8
NSA_KV = 2
DSA_HEADS = 8
DSA_KV = 2
HEADS_PER_GROUP = 4
CMP_LEN = 32
CMP_STRIDE = 16
CMP_HIDDEN = 256
SEL_BLOCK = 64
SEL_TOPN = 16
WINDOW = 512
IDX_HEADS = 16
IDX_DIM = 64
DSA_TOPK_MAX = 256
PLE_DIM = 256
EPS = 1e-6
NEG = -1e30
ATTN_SCALE = HEAD_DIM ** -0.5
LANES = 128
INT_MIN = -(2 ** 31)
RADIX_CHECKS = (24, 27)
BF16_EXACT_INT = 256

VMEM_LIMIT_BYTES = 56 * 1024 * 1024

F32 = jnp.float32
BF16 = jnp.bfloat16

SEGMENT = 256
INDEX_QUERIES = 128
TILE_ROWS = 16


class _Tiles(NamedTuple):
    norm_rows: int
    proj_rows: int
    proj_cols: int
    out_rows: int
    cmp_queries: int
    keys: int

    @classmethod
    def for_shape(cls, batch, seq):
        t = batch * seq
        return cls(norm_rows=min(512, t), proj_rows=min(1024, t), proj_cols=1024, out_rows=min(256, t),
                   cmp_queries=min(256, seq), keys=min(512, seq))


def _params(semantics):
    return pltpu.CompilerParams(dimension_semantics=semantics, vmem_limit_bytes=VMEM_LIMIT_BYTES)


def _sigmoid(x):
    return 1.0 / (1.0 + jnp.exp(-x))


def _dot(a, b):
    return jnp.dot(a, b, preferred_element_type=F32)


def _dot_nt(a, b):
    return lax.dot_general(a, b, (((1,), (1,)), ((), ())), preferred_element_type=F32)


def _log2(n):
    assert n > 0 and n & (n - 1) == 0, n
    return n.bit_length() - 1


def _div_pow2(x, n):
    return lax.shift_right_logical(x, jnp.int32(_log2(n)))


def _group_slope_scale(g):
    return jnp.where(g == 0, 1.0, 2.0 ** -HEADS_PER_GROUP).astype(F32)


def _rmsnorm_kernel(x_ref, g_ref, o_ref):
    x = x_ref[...]
    ms = jnp.mean(x * x, axis=-1, keepdims=True)
    o_ref[...] = (x * lax.rsqrt(ms + EPS) * g_ref[...]).astype(o_ref.dtype)


def _rmsnorm(x, g, tm):
    t, d = x.shape
    return pl.pallas_call(
        _rmsnorm_kernel,
        grid=(t // tm,),
        in_specs=[pl.BlockSpec((tm, d), lambda i: (i, 0)), pl.BlockSpec((1, d), lambda i: (0, 0))],
        out_specs=pl.BlockSpec((tm, d), lambda i: (i, 0)),
        out_shape=jax.ShapeDtypeStruct((t, d), BF16),
        compiler_params=_params(("parallel",)),
        name="in_rmsnorm",
    )(x, g.reshape(1, d))


def _proj_headnorm_kernel(h_ref, w_ref, g_ref, o_ref):
    y = _dot(h_ref[...], w_ref[...])
    for hd in range(y.shape[1] // HEAD_DIM):
        sl = slice(hd * HEAD_DIM, (hd + 1) * HEAD_DIM)
        yh = y[:, sl]
        ms = jnp.mean(yh * yh, axis=-1, keepdims=True)
        o_ref[:, sl] = (yh * lax.rsqrt(ms + EPS) * g_ref[:, sl]).astype(o_ref.dtype)


def _proj_act_kernel(h_ref, w_ref, o_ref, *, act):
    y = _dot(h_ref[...], w_ref[...])
    if act == "silu":
        y = y * _sigmoid(y)
    elif act == "sigmoid":
        y = _sigmoid(y)
    o_ref[...] = y.astype(o_ref.dtype)


def _proj_split_kernel(h_ref, w_ref, *o_refs):
    y = _dot(h_ref[...], w_ref[...])
    off = 0
    for o_ref in o_refs:
        n = o_ref.shape[1]
        o_ref[...] = y[:, off:off + n].astype(o_ref.dtype)
        off += n


def _proj(h, w, tm, tn, out_dtype, act=None, gain=None):
    t, d = h.shape
    n = w.shape[1]
    in_specs = [pl.BlockSpec((tm, d), lambda i, j: (i, 0)), pl.BlockSpec((d, tn), lambda i, j: (0, j))]
    args = [h, w]
    if gain is not None:
        body = _proj_headnorm_kernel
        in_specs.append(pl.BlockSpec((1, tn), lambda i, j: (0, j)))
        args.append(gain.reshape(1, n))
    else:
        body = functools.partial(_proj_act_kernel, act=act)
    return pl.pallas_call(
        body,
        grid=(t // tm, n // tn),
        in_specs=in_specs,
        out_specs=pl.BlockSpec((tm, tn), lambda i, j: (i, j)),
        out_shape=jax.ShapeDtypeStruct((t, n), out_dtype),
        compiler_params=_params(("parallel", "arbitrary")),
        name="in_proj_" + ("headnorm" if gain is not None else str(act)),
    )(*args)


def _proj_heads_kernel(h_ref, w_ref, o_ref):
    y = _dot(h_ref[...], w_ref[...])
    for hd in range(o_ref.shape[1]):
        o_ref[0, hd] = y[:, hd * IDX_DIM:(hd + 1) * IDX_DIM].astype(o_ref.dtype)


def _proj_heads(h, w, tm, batch, seq):
    t, d = h.shape
    per_batch = seq // tm
    return pl.pallas_call(
        _proj_heads_kernel,
        grid=(t // tm,),
        in_specs=[pl.BlockSpec((tm, d), lambda i: (i, 0)), pl.BlockSpec(w.shape, lambda i: (0, 0))],
        out_specs=pl.BlockSpec((1, IDX_HEADS, tm, IDX_DIM), lambda i: (i // per_batch, 0, i % per_batch, 0)),
        out_shape=jax.ShapeDtypeStruct((batch, IDX_HEADS, seq, IDX_DIM), BF16),
        compiler_params=_params(("parallel",)),
        name="in_proj_heads",
    )(h, w)


def _proj_split(h, w, tm, widths):
    t, d = h.shape
    n = w.shape[1]
    return pl.pallas_call(
        _proj_split_kernel,
        grid=(t // tm,),
        in_specs=[pl.BlockSpec((tm, d), lambda i: (i, 0)), pl.BlockSpec((d, n), lambda i: (0, 0))],
        out_specs=[pl.BlockSpec((tm, wd), lambda i: (i, 0)) for wd in widths],
        out_shape=[jax.ShapeDtypeStruct((t, wd), F32) for wd in widths],
        compiler_params=_params(("parallel",)),
        name="in_proj_split",
    )(h, w)


def _compress_kernel(x_ref, pea_ref, peb_ref, wa_ref, wb_ref, w2_ref, g_ref, o_ref, *, normalize):
    x = x_ref[...]
    nch = x.shape[0]
    ha = _dot((x + pea_ref[...]).astype(BF16), wa_ref[...])
    hb = _dot((x + peb_ref[...]).astype(BF16), wb_ref[...])
    hid = ha + pltpu.roll(hb, nch - 1, axis=0)
    act = (hid * _sigmoid(hid)).astype(BF16)
    for g in range(NSA_KV):
        o = _dot(act[:, g * CMP_HIDDEN:(g + 1) * CMP_HIDDEN], w2_ref[...])
        if normalize:
            ms = jnp.mean(o * o, axis=-1, keepdims=True)
            o = o * lax.rsqrt(ms + EPS) * g_ref[...]
        o_ref[0, :, g * HEAD_DIM:(g + 1) * HEAD_DIM] = o.astype(o_ref.dtype)


def _expand_compress_params(pe, w1):
    half = CMP_LEN // 2
    w = w1.reshape(CMP_LEN, HEAD_DIM, CMP_HIDDEN)
    eye = jnp.eye(NSA_KV, dtype=w1.dtype)

    def expand_w(wh):
        return jnp.einsum("ldj,hg->lhdgj", wh, eye).reshape(half * NSA_KV * HEAD_DIM, NSA_KV * CMP_HIDDEN)

    def expand_pe(ph):
        return jnp.broadcast_to(ph[:, None, :], (half, NSA_KV, HEAD_DIM)).reshape(1, half * NSA_KV * HEAD_DIM)

    return (expand_pe(pe[:half]), expand_pe(pe[half:]),
            expand_w(w[:half]).astype(BF16), expand_w(w[half:]).astype(BF16))


def _compress(raw, pe, w1, w2, gain, batch, seq):
    nch = seq // CMP_STRIDE
    width = CMP_STRIDE * NSA_KV * HEAD_DIM
    x = raw.reshape(batch * nch, width)
    pea, peb, wa, wb = _expand_compress_params(pe, w1)
    normalize = gain is not None
    g = (gain if normalize else jnp.ones((HEAD_DIM,), F32)).reshape(1, HEAD_DIM)
    const = lambda b: (0, 0)
    return pl.pallas_call(
        functools.partial(_compress_kernel, normalize=normalize),
        grid=(batch,),
        in_specs=[pl.BlockSpec((nch, width), lambda b: (b, 0)),
                  pl.BlockSpec((1, width), const), pl.BlockSpec((1, width), const),
                  pl.BlockSpec(wa.shape, const), pl.BlockSpec(wb.shape, const),
                  pl.BlockSpec((CMP_HIDDEN, HEAD_DIM), const), pl.BlockSpec((1, HEAD_DIM), const)],
        out_specs=pl.BlockSpec((1, nch, NSA_KV * HEAD_DIM), lambda b: (b, 0, 0)),
        out_shape=jax.ShapeDtypeStruct((batch, nch, NSA_KV * HEAD_DIM), BF16),
        compiler_params=_params(("parallel",)),
        name="nsa_compress",
    )(x, pea, peb, wa, wb, w2.astype(BF16), g)


def _cmp_to_sel_matrix(nch, n_sel_pad):
    n_cmp = nch - 1
    c0 = np.arange(nch)[:, None] * CMP_STRIDE
    s0 = np.arange(n_sel_pad)[None, :] * SEL_BLOCK
    ov = np.clip(np.minimum(c0 + CMP_LEN, s0 + SEL_BLOCK) - np.maximum(c0, s0), 0, None) / CMP_LEN
    ov[n_cmp:] = 0.0
    return jnp.asarray(ov.T, dtype=BF16)


def _nsa_cmp_kernel(q_ref, kc_ref, vc_ref, mt_ref, eye_ref, tile_of_ref, ocmp_ref, selb_ref, flags_ref,
                    *, tq, nch, n_sel):
    g = pl.program_id(1)
    i = pl.program_id(2)
    t = i * tq + lax.broadcasted_iota(jnp.int32, (tq, 1), 0)
    live = (t >= CMP_LEN - 1).astype(F32)
    gsc = _group_slope_scale(g)

    def attend(width):
        c_end = lax.broadcasted_iota(jnp.int32, (1, width), 1) * CMP_STRIDE + (CMP_LEN - 1)
        vis = c_end <= t
        distc = t.astype(F32) - (c_end.astype(F32) - (CMP_LEN - 1) / 2.0)
        kc = kc_ref[0, :width, :]
        vc = vc_ref[0, :width, :]
        psum = jnp.zeros((tq, width), F32)
        for j in range(HEADS_PER_GROUP):
            slope = (2.0 ** -(j + 1)) * gsc
            sl = slice(j * HEAD_DIM, (j + 1) * HEAD_DIM)
            s = _dot_nt(q_ref[:, sl], kc) - slope * distc
            s = jnp.where(vis, s, NEG)
            e = jnp.exp(s - jnp.max(s, axis=-1, keepdims=True))
            p = e * (live / jnp.sum(e, axis=-1, keepdims=True))
            psum = psum + p
            ocmp_ref[:, sl] = _dot(p.astype(BF16), vc)
        return _dot_nt(mt_ref[:, :width], psum.astype(BF16))

    widths = sorted({min(nch, LANES * (k + 1)) for k in range(pl.cdiv(nch, LANES))})
    visible = ((i + 1) * tq - CMP_LEN) // CMP_STRIDE + 1
    imp = lax.switch(jnp.minimum((visible - 1) // LANES, len(widths) - 1),
                     [functools.partial(attend, w) for w in widths])
    blk = lax.broadcasted_iota(jnp.int32, (LANES, 1), 0)
    blk_t = _div_pow2(i * tq + lax.broadcasted_iota(jnp.int32, (1, tq), 1), SEL_BLOCK)
    valid = blk <= blk_t
    forced = (blk == 0) | (blk == blk_t) | (blk == blk_t - 1)
    vals = jnp.where(valid & jnp.logical_not(forced) & (blk < n_sel), imp, -1.0)
    sel = valid & forced
    for _ in range(SEL_TOPN - 3):
        mx = jnp.max(vals, axis=0, keepdims=True)
        first = jnp.min(jnp.where(vals == mx, blk, LANES), axis=0, keepdims=True)
        pick = blk == first
        sel = sel | pick
        vals = jnp.where(pick, -3.0, vals)
    sel = sel & valid
    masked = jnp.where(sel, 0.0, NEG).astype(BF16)
    selb_ref[...] = _dot_nt(eye_ref[...], masked).astype(selb_ref.dtype)

    t_lane = i * tq + lax.broadcasted_iota(jnp.int32, (1, tq), 1)
    early = blk < (SEGMENT // SEL_BLOCK) * _div_pow2(t_lane, SEGMENT)
    per_query = _dot(tile_of_ref[...], jnp.where(sel & early, 1.0, 0.0).astype(BF16))
    out_lane = lax.broadcasted_iota(jnp.int32, flags_ref.shape[1:], 1)
    flags = jnp.zeros(flags_ref.shape[1:], jnp.int32)
    for h in range(tq // SEGMENT):
        hit = jnp.max(per_query[:, h * SEGMENT:(h + 1) * SEGMENT], axis=1, keepdims=True) > 0.0
        flags = jnp.where((out_lane == h) & hit, 1, flags)
    flags_ref[0] = flags


def _nsa_cmp(qkn, kc, vc, batch, seq, tq, tk):
    t = batch * seq
    nch = seq // CMP_STRIDE
    n_sel = seq // SEL_BLOCK
    n_all = seq // tk
    assert n_sel <= LANES and n_all <= TILE_ROWS and tq % SEGMENT == 0
    nq = seq // tq
    gw = HEADS_PER_GROUP * HEAD_DIM
    mt = _cmp_to_sel_matrix(nch, LANES)
    eye = jnp.asarray(np.eye(tq, dtype=np.float32), dtype=BF16)
    tile_of = jnp.asarray(np.arange(LANES)[None, :] // (tk // SEL_BLOCK) == np.arange(TILE_ROWS)[:, None], dtype=BF16)
    row = lambda b, g, i: (b * nq + i, g)
    ocmp, selb, flags = pl.pallas_call(
        functools.partial(_nsa_cmp_kernel, tq=tq, nch=nch, n_sel=n_sel),
        grid=(batch, NSA_KV, nq),
        in_specs=[pl.BlockSpec((tq, gw), row),
                  pl.BlockSpec((1, nch, HEAD_DIM), lambda b, g, i: (b, 0, g)),
                  pl.BlockSpec((1, nch, HEAD_DIM), lambda b, g, i: (b, 0, g)),
                  pl.BlockSpec((LANES, nch), lambda b, g, i: (0, 0)),
                  pl.BlockSpec((tq, tq), lambda b, g, i: (0, 0)),
                  pl.BlockSpec((TILE_ROWS, LANES), lambda b, g, i: (0, 0))],
        out_specs=[pl.BlockSpec((tq, gw), row), pl.BlockSpec((tq, LANES), row),
                   pl.BlockSpec((1, TILE_ROWS, LANES), lambda b, g, i: ((b * NSA_KV + g) * nq + i, 0, 0))],
        out_shape=[jax.ShapeDtypeStruct((t, NSA_HEADS * HEAD_DIM), F32),
                   jax.ShapeDtypeStruct((t, NSA_KV * LANES), BF16),
                   jax.ShapeDtypeStruct((batch * NSA_KV * nq, TILE_ROWS, LANES), jnp.int32)],
        compiler_params=_params(("parallel", "parallel", "parallel")),
        name="nsa_cmp_select",
    )(qkn, kc, vc, mt, eye, tile_of)
    segs = tq // SEGMENT
    tile_flags = flags[:, :n_all, :segs].transpose(0, 2, 1).reshape(batch, NSA_KV, nq * segs, n_all)
    return ocmp, selb, tile_flags


def _stack_rows(t0, tq):
    r = lax.broadcasted_iota(jnp.int32, (HEADS_PER_GROUP * tq, 1), 0)
    return t0 + (r & (tq - 1))


def _key_aux_blocks(seq):
    j = np.arange(seq)
    aux = (j[:, None] // SEL_BLOCK == np.arange(LANES)[None, :]).astype(np.float32)
    aux[:, 0] = j % SEL_BLOCK
    return jnp.asarray(aux, dtype=BF16)


def _key_aux_position(seq):
    assert seq <= BF16_EXACT_INT * BF16_EXACT_INT
    j = np.arange(seq)
    aux = np.zeros((seq, LANES), np.float32)
    aux[:, 0] = j % BF16_EXACT_INT
    aux[:, 1] = j - j % BF16_EXACT_INT
    return jnp.asarray(aux, dtype=BF16)


def _query_aug(q_ref, aux_cols):
    parts = [jnp.concatenate([q_ref[:, j * HEAD_DIM:(j + 1) * HEAD_DIM], aux_cols[j].astype(BF16)], axis=1)
             for j in range(HEADS_PER_GROUP)]
    return jnp.concatenate(parts, axis=0)


def _flash_step(s, v, carry):
    m, l, acc = carry
    m_new = jnp.maximum(m, jnp.max(s, axis=-1, keepdims=True))
    alpha = jnp.exp(m - m_new)
    p = jnp.exp(s - m_new).astype(BF16)
    pv = _dot(p, jnp.concatenate([v, jnp.ones_like(v)], axis=1))
    l = alpha * l + pv[:, HEAD_DIM:HEAD_DIM + 1]
    acc = alpha * acc + pv[:, :HEAD_DIM]
    return m_new, l, acc


def _flash_init(rows):
    return (jnp.full((rows, 1), NEG, F32), jnp.zeros((rows, 1), F32), jnp.zeros((rows, HEAD_DIM), F32))


def _pipelined_tiles(n, n_all, scores_fn, consume_fn, s_a, s_b, carry):
    s_a[...] = scores_fn(0)

    def pairs(k, count, carry):
        for c in range(count):
            s_b[...] = scores_fn(k + 2 * c + 1)
            carry = consume_fn(k + 2 * c, s_a[...], carry)
            s_a[...] = scores_fn(jnp.minimum(k + 2 * c + 2, n_all - 1))
            carry = consume_fn(k + 2 * c + 1, s_b[...], carry)
        return carry

    quads = lax.shift_right_logical(n, 2)
    carry = lax.fori_loop(0, quads, lambda u, c: pairs(4 * u, 2, c), carry)
    done = 4 * quads
    carry = lax.cond((n & 2) == 2, lambda c: pairs(done, 1, c), lambda c: c, carry)
    return lax.cond((n & 1) == 1, lambda c: consume_fn(n - 1, s_a[...], c), lambda c: c, carry)


def _nsa_main_kernel(tiles_ref, count_ref, q_ref, ks_ref, vs_ref, kw_ref, vw_ref, kaux_ref, kpos_ref, selb_ref,
                     ocmp_ref, gate_ref, z_ref, o_ref, s_a, s_b, *, tq, tk, seq):
    b = pl.program_id(0)
    g = pl.program_id(1)
    i = pl.program_id(2)
    n_all = seq // tk
    t0 = pl.multiple_of(i * tq, tq)
    rows = HEADS_PER_GROUP * tq
    gsc = _group_slope_scale(g)
    t_rows = _stack_rows(t0, tq)
    slopes = [(2.0 ** -(j + 1)) * gsc for j in range(HEADS_PER_GROUP)]

    lane = lax.broadcasted_iota(jnp.int32, (tq, LANES), 1)
    chosen = selb_ref[...].astype(F32) > -1.0
    own_first = _div_pow2(t0, SEL_BLOCK)
    block_start = (lane * SEL_BLOCK).astype(F32)

    def masked_queries(blocks):
        return _query_aug(q_ref, [jnp.where(lane == 0, sl, jnp.where(chosen & blocks, sl * block_start, NEG))
                                  for sl in slopes])

    q_aug = masked_queries(lane < own_first)

    slot = (b * NSA_KV + g) * (seq // tq) + i
    base = slot * n_all

    def scores(u):
        j0 = pl.multiple_of(tiles_ref[base + u] * tk, tk)
        return _dot_nt(q_aug, jnp.concatenate([ks_ref[pl.ds(j0, tk), :], kaux_ref[pl.ds(j0, tk), :]], axis=1))

    def consume(u, s, carry):
        j0 = pl.multiple_of(tiles_ref[base + u] * tk, tk)
        return _flash_step(s, vs_ref[pl.ds(j0, tk), :], carry)

    carry = _pipelined_tiles(count_ref[slot], n_all, scores, consume, s_a, s_b, _flash_init(rows))

    j_own = t0 + lax.broadcasted_iota(jnp.int32, (1, tq), 1)
    s = _dot_nt(masked_queries(lane >= own_first),
                jnp.concatenate([ks_ref[pl.ds(t0, tq), :], kaux_ref[pl.ds(t0, tq), :]], axis=1))
    s = jnp.where(j_own <= t_rows, s, NEG)
    _, l_s, acc_s = _flash_step(s, vs_ref[pl.ds(t0, tq), :], carry)
    o_slc = acc_s / l_s

    q_pos = _query_aug(q_ref, [jnp.where(lane < 2, sl, 0.0) for sl in slopes])
    wk = WINDOW + tq
    start = pl.multiple_of(jnp.maximum(t0 - WINDOW, 0), LANES)
    j_win = start + lax.broadcasted_iota(jnp.int32, (1, wk), 1)
    s = _dot_nt(q_pos, jnp.concatenate([kw_ref[pl.ds(start, wk), :], kpos_ref[pl.ds(start, wk), :]], axis=1))
    s = jnp.where((j_win <= t_rows) & (j_win > t_rows - WINDOW), s, NEG)
    e = jnp.exp(s - jnp.max(s, axis=-1, keepdims=True))
    o_win = _dot(e.astype(BF16), vw_ref[pl.ds(start, wk), :]) / jnp.sum(e, axis=-1, keepdims=True)

    for j in range(HEADS_PER_GROUP):
        sl = slice(j * HEAD_DIM, (j + 1) * HEAD_DIM)
        rs = slice(j * tq, (j + 1) * tq)
        gates = [_sigmoid(gate_ref[:, 3 * j + c:3 * j + c + 1]) for c in range(3)]
        o = gates[0] * ocmp_ref[:, sl] + gates[1] * o_slc[rs] + gates[2] * o_win[rs]
        o_ref[:, sl] = (o * z_ref[:, sl]).astype(o_ref.dtype)


def _nsa_main(qkn, vals, selb, tile_flags, ocmp, gates, zact, batch, seq, tq, tk):
    t = batch * seq
    nq = seq // tq
    n_all = seq // tk
    gw = HEADS_PER_GROUP * HEAD_DIM
    flags = tile_flags.reshape(batch * NSA_KV * nq, n_all)
    tiles = jnp.argsort(1 - flags, axis=-1, stable=True).astype(jnp.int32).reshape(-1)
    counts = jnp.sum(flags, axis=-1).astype(jnp.int32)
    row = lambda b, g, i, *_: (b * nq + i, g)
    ks_blk = (NSA_HEADS * HEAD_DIM) // HEAD_DIM
    kw_blk = ks_blk + NSA_KV
    grid_spec = pltpu.PrefetchScalarGridSpec(
        num_scalar_prefetch=2,
        grid=(batch, NSA_KV, nq),
        in_specs=[pl.BlockSpec((tq, gw), row),
                  pl.BlockSpec((seq, HEAD_DIM), lambda b, g, i, *_: (b, ks_blk + g)),
                  pl.BlockSpec((seq, HEAD_DIM), lambda b, g, i, *_: (b, g)),
                  pl.BlockSpec((seq, HEAD_DIM), lambda b, g, i, *_: (b, kw_blk + g)),
                  pl.BlockSpec((seq, HEAD_DIM), lambda b, g, i, *_: (b, NSA_KV + g)),
                  pl.BlockSpec((seq, LANES), lambda b, g, i, *_: (0, 0)),
                  pl.BlockSpec((seq, LANES), lambda b, g, i, *_: (0, 0)),
                  pl.BlockSpec((tq, LANES), row),
                  pl.BlockSpec((tq, gw), row),
                  pl.BlockSpec((tq, LANES), row),
                  pl.BlockSpec((tq, gw), row)],
        out_specs=pl.BlockSpec((tq, gw), row),
        scratch_shapes=[pltpu.VMEM((HEADS_PER_GROUP * tq, tk), F32)] * 2,
    )
    return pl.pallas_call(
        functools.partial(_nsa_main_kernel, tq=tq, tk=tk, seq=seq),
        grid_spec=grid_spec,
        out_shape=jax.ShapeDtypeStruct((t, NSA_HEADS * HEAD_DIM), BF16),
        compiler_params=_params(("parallel", "parallel", "arbitrary")),
        name="nsa_select_window",
    )(tiles, counts, qkn, qkn, vals, qkn, vals, _key_aux_blocks(seq), _key_aux_position(seq), selb, ocmp, gates,
      zact)


def _threshold_value(code):
    return pltpu.bitcast(jnp.where(code < 0, code ^ 0x7FFFFFFF, code), F32)


def _dsa_index_kernel(qi_ref, ki_ref, wi_ref, eye_ref, tri_ref, bias_ref, score_scr, *, tq, tk, seq, topk):
    i = pl.program_id(1)
    t0 = i * tq
    n_all = seq // tk
    n_kt = _div_pow2(t0 + tq + tk - 1, tk)
    t = t0 + lax.broadcasted_iota(jnp.int32, (1, tq), 1)
    wrow = wi_ref[...] * (IDX_HEADS ** -0.5 * IDX_DIM ** -0.5)
    hb = 4

    def key_ids(kt):
        return kt * tk + lax.broadcasted_iota(jnp.int32, (tk, 1), 0)

    def score_tile(kt):
        j0 = pl.multiple_of(kt * tk, tk)
        ki = ki_ref[pl.ds(j0, tk), :]
        score = jnp.zeros((tk, tq), F32)
        for h0 in range(0, IDX_HEADS, hb):
            x = _dot_nt(ki, qi_ref[0, h0:h0 + hb].reshape(hb * tq, IDX_DIM))
            for h in range(hb):
                score = score + jnp.maximum(x[:, h * tq:(h + 1) * tq], 0.0) * wrow[h0 + h:h0 + h + 1, :]
        score = jnp.where(key_ids(kt) <= t, score, -jnp.inf)
        score_scr[kt] = score
        return score

    def hits(pred, score, c):
        return c + jnp.sum(jnp.where(pred(score), 1, 0).reshape(tk // 8, 8, tq), axis=0)

    def non_negative(s):
        return s >= 0.0

    n_pairs = lax.shift_right_logical(n_kt + 1, 1)

    def score_step(u, c):
        for kt in (2 * u, 2 * u + 1):
            c = hits(non_negative, score_tile(kt), c)
        return c

    c0 = jnp.sum(lax.fori_loop(0, n_pairs, score_step, jnp.zeros((8, tq), jnp.int32)), axis=0, keepdims=True)

    def count(pred):
        def step(u, c):
            for kt in (2 * u, 2 * u + 1):
                c = hits(pred, score_scr[kt], c)
            return c
        c = lax.fori_loop(0, n_pairs, step, jnp.zeros((8, tq), jnp.int32))
        return jnp.sum(c, axis=0, keepdims=True)

    everything = jnp.int32(seq + tk)
    state = (jnp.where(c0 >= topk, 0, INT_MIN), jnp.where(c0 >= topk, c0, everything))

    def bit_step(b, state):
        v, n_ge = state
        cand = v | (jnp.int32(1) << (30 - b))
        threshold = _threshold_value(cand)
        c = count(lambda s: s >= threshold)
        return jnp.where(c >= topk, cand, v), jnp.where(c >= topk, c, n_ge)

    def settled(state):
        done = (state[1] == topk) | (t < topk)
        return jnp.min(jnp.where(done, 1, 0)) > 0

    state = lax.fori_loop(0, RADIX_CHECKS[0], bit_step, state)
    for lo, hi in zip(RADIX_CHECKS, RADIX_CHECKS[1:] + (31,)):
        state = lax.cond(settled(state), lambda s: s, lambda s, lo=lo, hi=hi: lax.fori_loop(lo, hi, bit_step, s), state)
    v, n_ge = state
    threshold = _threshold_value(v)
    take_all = v == INT_MIN

    def emit(kt, keep):
        masked = jnp.where(keep & (key_ids(kt) <= t), 0.0, NEG).astype(BF16)
        bias_ref[0, 0, kt] = _dot_nt(eye_ref[...], masked).astype(bias_ref.dtype)

    def fill_step(kt, _):
        bias_ref[0, 0, kt] = jnp.full((tq, tk), NEG, bias_ref.dtype)
        return 0

    def emit_no_ties(_):
        def step(u, c):
            for kt in (2 * u, 2 * u + 1):
                emit(kt, (score_scr[kt] >= threshold) | take_all)
            return c
        return lax.fori_loop(0, n_pairs, step, 0)

    def emit_with_ties(_):
        need = (topk - count(lambda s: s > threshold)).astype(F32)

        def step(kt, carry):
            score = score_scr[kt]
            eq = score == threshold
            eqf = jnp.where(eq, 1.0, 0.0)
            before = carry + _dot(tri_ref[...], eqf.astype(BF16)) - eqf
            emit(kt, (score > threshold) | (eq & (before < need)) | take_all)
            return carry + jnp.sum(eqf, axis=0, keepdims=True)

        lax.fori_loop(0, n_kt, step, jnp.zeros((1, tq), F32))
        return lax.fori_loop(n_kt, 2 * n_pairs, fill_step, 0)

    over = (n_ge > topk) & jnp.logical_not(take_all)
    lax.cond(jnp.max(jnp.where(over, 1, 0)) > 0, emit_with_ties, emit_no_ties, 0)
    lax.fori_loop(2 * n_pairs, n_all, fill_step, 0)


def _dsa_index(qi_heads, ki, wi_t, batch, seq, tq, tk):
    nq = seq // tq
    n_all = seq // tk
    topk = min(DSA_TOPK_MAX, seq // 4)
    tri = jnp.asarray(np.tril(np.ones((tk, tk), np.float32)), dtype=BF16)
    eye = jnp.asarray(np.eye(tq, dtype=np.float32), dtype=BF16)
    return pl.pallas_call(
        functools.partial(_dsa_index_kernel, tq=tq, tk=tk, seq=seq, topk=topk),
        grid=(batch, nq),
        in_specs=[pl.BlockSpec((1, IDX_HEADS, tq, IDX_DIM), lambda b, i: (b, 0, i, 0)),
                  pl.BlockSpec((seq, IDX_DIM), lambda b, i: (b, 0)),
                  pl.BlockSpec((IDX_HEADS, tq), lambda b, i: (0, b * nq + i)),
                  pl.BlockSpec((tq, tq), lambda b, i: (0, 0)),
                  pl.BlockSpec((tk, tk), lambda b, i: (0, 0))],
        out_specs=pl.BlockSpec((1, 1, n_all, tq, tk), lambda b, i: (b, i, 0, 0, 0)),
        out_shape=jax.ShapeDtypeStruct((batch, nq, n_all, tq, tk), BF16),
        scratch_shapes=[pltpu.VMEM((n_all, tk, tq), F32)],
        compiler_params=_params(("parallel", "arbitrary")),
        name="dsa_index_topk",
    )(qi_heads, ki, wi_t, eye, tri)


def _dsa_attn_kernel(q_ref, k_ref, v_ref, kaux_ref, bias_ref, z_ref, o_ref, s_a, s_b, *, tq, tk, seq):
    g = pl.program_id(1)
    i = pl.program_id(2)
    t0 = i * tq
    rows = HEADS_PER_GROUP * tq
    gsc = _group_slope_scale(g)
    slopes = [(2.0 ** -(j + 1)) * gsc for j in range(HEADS_PER_GROUP)]
    lane = lax.broadcasted_iota(jnp.int32, (tq, LANES), 1)
    q_aug = _query_aug(q_ref, [jnp.where(lane < 2, sl, 0.0) for sl in slopes])

    def scores(kt):
        j0 = pl.multiple_of(kt * tk, tk)
        return _dot_nt(q_aug, jnp.concatenate([k_ref[pl.ds(j0, tk), :], kaux_ref[pl.ds(j0, tk), :]], axis=1))

    def consume(kt, s, carry):
        j0 = pl.multiple_of(kt * tk, tk)
        bias = [bias_ref[0, part, kt].astype(F32) for part in range(bias_ref.shape[1])]
        s = s + jnp.concatenate(bias * HEADS_PER_GROUP, axis=0)
        return _flash_step(s, v_ref[pl.ds(j0, tk), :], carry)

    n_kt = _div_pow2(t0 + tq + tk - 1, tk)
    _, l, acc = _pipelined_tiles(n_kt, seq // tk, scores, consume, s_a, s_b, _flash_init(rows))
    o = acc / l
    for j in range(HEADS_PER_GROUP):
        sl = slice(j * HEAD_DIM, (j + 1) * HEAD_DIM)
        o_ref[:, sl] = (o[j * tq:(j + 1) * tq] * z_ref[:, sl]).astype(o_ref.dtype)


def _dsa_attn(qkn, vals, bias, zact, batch, seq, tq, tk):
    t = batch * seq
    nq = seq // tq
    n_all = seq // tk
    gw = HEADS_PER_GROUP * HEAD_DIM
    row = lambda b, g, i: (b * nq + i, g)
    q_blk = (NSA_HEADS + 2 * NSA_KV) * HEAD_DIM // gw
    k_blk = (NSA_HEADS + 2 * NSA_KV + DSA_HEADS)
    return pl.pallas_call(
        functools.partial(_dsa_attn_kernel, tq=tq, tk=tk, seq=seq),
        grid=(batch, DSA_KV, nq),
        in_specs=[pl.BlockSpec((tq, gw), lambda b, g, i: (b * nq + i, q_blk + g)),
                  pl.BlockSpec((seq, HEAD_DIM), lambda b, g, i: (b, k_blk + g)),
                  pl.BlockSpec((seq, HEAD_DIM), lambda b, g, i: (b, 2 * NSA_KV + g)),
                  pl.BlockSpec((seq, LANES), lambda b, g, i: (0, 0)),
                  pl.BlockSpec((1, tq // bias.shape[3]) + bias.shape[2:], lambda b, g, i: (b, i, 0, 0, 0)),
                  pl.BlockSpec((tq, gw), lambda b, g, i: (b * nq + i, NSA_KV + g))],
        out_specs=pl.BlockSpec((tq, gw), row),
        out_shape=jax.ShapeDtypeStruct((t, DSA_HEADS * HEAD_DIM), BF16),
        scratch_shapes=[pltpu.VMEM((HEADS_PER_GROUP * tq, tk), F32)] * 2,
        compiler_params=_params(("parallel", "parallel", "arbitrary")),
        name="dsa_attention",
    )(qkn, qkn, vals, _key_aux_position(seq), bias, zact)


def _merge_kernel(on_ref, od_ref, sn_ref, sd_ref, x_ref, wun_ref, wud_ref, wout_ref, o_ref):
    y = sn_ref[...] * _dot(on_ref[...], wun_ref[...]) + sd_ref[...] * _dot(od_ref[...], wud_ref[...])
    o_ref[...] = x_ref[...] + _dot(y.astype(BF16), wout_ref[...])


def _merge(o_nsa, o_dsa, gate_act, x, w_up_nsa, w_up_dsa, w_out, tm):
    t, d = x.shape
    const = lambda i: (0, 0)
    return pl.pallas_call(
        _merge_kernel,
        grid=(t // tm,),
        in_specs=[pl.BlockSpec((tm, o_nsa.shape[1]), lambda i: (i, 0)),
                  pl.BlockSpec((tm, o_dsa.shape[1]), lambda i: (i, 0)),
                  pl.BlockSpec((tm, d), lambda i: (i, 0)),
                  pl.BlockSpec((tm, d), lambda i: (i, 1)),
                  pl.BlockSpec((tm, d), lambda i: (i, 0)),
                  pl.BlockSpec(w_up_nsa.shape, const), pl.BlockSpec(w_up_dsa.shape, const),
                  pl.BlockSpec(w_out.shape, const)],
        out_specs=pl.BlockSpec((tm, d), lambda i: (i, 0)),
        out_shape=jax.ShapeDtypeStruct((t, d), F32),
        compiler_params=_params(("parallel",)),
        name="merge_out_proj",
    )(o_nsa, o_dsa, gate_act, gate_act, x, w_up_nsa, w_up_dsa, w_out)


def _ple_kernel(x_ref, p_ref, g_ref, wg_ref, wp_ref, o_ref):
    x = x_ref[...]
    ms = jnp.mean(x * x, axis=-1, keepdims=True)
    r = (x * lax.rsqrt(ms + EPS) * g_ref[...]).astype(BF16)
    gate = _sigmoid(_dot(r, wg_ref[...]))
    o_ref[...] = x + _dot(p_ref[...].astype(BF16), wp_ref[...]) * gate


def _ple(x, p, g, w_gate, w_proj, tm):
    t, d = x.shape
    const = lambda i: (0, 0)
    return pl.pallas_call(
        _ple_kernel,
        grid=(t // tm,),
        in_specs=[pl.BlockSpec((tm, d), lambda i: (i, 0)),
                  pl.BlockSpec((tm, p.shape[1]), lambda i: (i, 0)),
                  pl.BlockSpec((1, d), const),
                  pl.BlockSpec(w_gate.shape, const), pl.BlockSpec(w_proj.shape, const)],
        out_specs=pl.BlockSpec((tm, d), lambda i: (i, 0)),
        out_shape=jax.ShapeDtypeStruct((t, d), F32),
        compiler_params=_params(("parallel",)),
        name="ple_gate",
    )(x, p, g.reshape(1, d), w_gate, w_proj)


def _split_in_proj(w):
    widths = [NSA_HEADS * HEAD_DIM] + [NSA_KV * HEAD_DIM] * 6 + [NSA_HEADS * 3, NSA_HEADS * HEAD_DIM,
              DSA_HEADS * HEAD_DIM, DSA_KV * HEAD_DIM, DSA_KV * HEAD_DIM, IDX_HEADS * IDX_DIM, IDX_DIM,
              IDX_HEADS, DSA_HEADS * HEAD_DIM, D_MODEL, D_MODEL]
    offs = np.concatenate([[0], np.cumsum(widths)])
    assert offs[-1] == w.shape[1]
    (nq, nkc, nvc, nks, nvs, nkw, nvw, ng, nz, dq, dk, dv, iq, ik, iw, dz, mgn, mgd) = [
        w[:, offs[k]:offs[k + 1]] for k in range(len(widths))]
    pad = jnp.zeros((w.shape[0], LANES - IDX_DIM - IDX_HEADS - NSA_HEADS * 3), w.dtype)
    return dict(
        normed=jnp.concatenate([nq, nks, nkw, dq, dk], axis=1).astype(BF16),
        vals=jnp.concatenate([nvs, nvw, dv], axis=1).astype(BF16),
        iq=iq.astype(BF16),
        raw=jnp.concatenate([nkc, nvc, ik, iw, ng, pad], axis=1).astype(BF16),
        zact=jnp.concatenate([nz, dz], axis=1).astype(BF16),
        merge=jnp.concatenate([mgn, mgd], axis=1).astype(BF16),
    )


def _layer(x, p, norm_g, w_in, nsa_q_g, nsa_kc_g, nsa_ks_g, nsa_kw_g, cmp_pe_k, cmp_w1_k, cmp_w2_k,
           cmp_pe_v, cmp_w1_v, cmp_w2_v, dsa_q_g, dsa_k_g, w_up_nsa, w_up_dsa, w_out, ple_norm_g,
           w_ple_gate, w_ple_proj):
    batch, seq, d = x.shape
    t = batch * seq
    assert seq % SEGMENT == 0 and seq >= WINDOW + SEGMENT
    x2 = x.reshape(t, d)
    tiles = _Tiles.for_shape(batch, seq)
    tm = tiles.proj_rows

    w = _split_in_proj(w_in)
    h = _rmsnorm(x2, norm_g, tiles.norm_rows)

    gains = jnp.concatenate([jnp.tile(nsa_q_g * ATTN_SCALE, NSA_HEADS), jnp.tile(nsa_ks_g, NSA_KV),
                             jnp.tile(nsa_kw_g, NSA_KV), jnp.tile(dsa_q_g * ATTN_SCALE, DSA_HEADS),
                             jnp.tile(dsa_k_g, DSA_KV)])
    qkn = _proj(h, w["normed"], tm, w["normed"].shape[1] // 2, BF16, gain=gains)
    vals = _proj(h, w["vals"], tm, w["vals"].shape[1], BF16)
    qi_heads = _proj_heads(h, w["iq"], min(tm, seq), batch, seq)
    kc_raw, vc_raw, misc = _proj_split(h, w["raw"], tm, (NSA_KV * HEAD_DIM, NSA_KV * HEAD_DIM, LANES))
    zact = _proj(h, w["zact"], tm, tiles.proj_cols, F32, act="silu")
    gate_act = _proj(h, w["merge"], tm, tiles.proj_cols, F32, act="sigmoid")

    ki = misc[:, :IDX_DIM].astype(BF16)
    wi = misc[:, IDX_DIM:IDX_DIM + IDX_HEADS]
    ng = misc[:, IDX_DIM + IDX_HEADS:IDX_DIM + IDX_HEADS + NSA_HEADS * 3]
    per_group = HEADS_PER_GROUP * 3
    gates = jnp.pad(ng.reshape(t, NSA_KV, per_group), ((0, 0), (0, 0), (0, LANES - per_group))).reshape(t, NSA_KV * LANES)

    kc = _compress(kc_raw, cmp_pe_k, cmp_w1_k, cmp_w2_k, nsa_kc_g, batch, seq)
    vc = _compress(vc_raw, cmp_pe_v, cmp_w1_v, cmp_w2_v, None, batch, seq)

    tk = tiles.keys
    ocmp, selb, tile_flags = _nsa_cmp(qkn, kc, vc, batch, seq, tiles.cmp_queries, tk)
    o_nsa = _nsa_main(qkn, vals, selb, tile_flags, ocmp, gates, zact, batch, seq, SEGMENT, tk)

    bias = _dsa_index(qi_heads, ki, wi.T, batch, seq, INDEX_QUERIES, tk)
    o_dsa = _dsa_attn(qkn, vals, bias, zact, batch, seq, SEGMENT, tk)

    x1 = _merge(o_nsa, o_dsa, gate_act, x2, w_up_nsa.astype(BF16), w_up_dsa.astype(BF16), w_out.astype(BF16),
                tiles.out_rows)
    x3 = _ple(x1, p.reshape(t, PLE_DIM), ple_norm_g, w_ple_gate.astype(BF16), w_ple_proj.astype(BF16),
              tiles.out_rows)
    return x3.reshape(batch, seq, d)


def kernel(x, p, norm_g, w_in, nsa_q_g, nsa_kc_g, nsa_ks_g, nsa_kw_g, cmp_pe_k, cmp_w1_k, cmp_w2_k, cmp_pe_v, cmp_w1_v, cmp_w2_v, dsa_q_g, dsa_k_g, w_up_nsa, w_up_dsa, w_out, ple_norm_g, w_ple_gate, w_ple_proj):
    depth = w_in.shape[0]
    for i in range(depth):
        x = _layer(x, p[i], norm_g[i], w_in[i], nsa_q_g[i], nsa_kc_g[i], nsa_ks_g[i], nsa_kw_g[i],
                   cmp_pe_k[i], cmp_w1_k[i], cmp_w2_k[i], cmp_pe_v[i], cmp_w1_v[i], cmp_w2_v[i],
                   dsa_q_g[i], dsa_k_g[i], w_up_nsa[i], w_up_dsa[i], w_out[i], ple_norm_g[i],
                   w_ple_gate[i], w_ple_proj[i])
    return x
```

```python
import functools
from typing import NamedTuple

import numpy as np
import jax
import jax.numpy as jnp
from jax import lax
from jax.experimental import pallas as pl
from jax.experimental.pallas import tpu as pltpu

D_MODEL = 2048
HEAD_DIM = 128
NSA_HEADS = 8
NSA_KV = 2
DSA_HEADS = 8
DSA_KV = 2
HEADS_PER_GROUP = 4
CMP_LEN = 32
CMP_STRIDE = 16
CMP_HIDDEN = 256
SEL_BLOCK = 64
SEL_TOPN = 16
WINDOW = 512
IDX_HEADS = 16
IDX_DIM = 64
DSA_TOPK_MAX = 256
PLE_DIM = 256
EPS = 1e-6
NEG = -1e30
ATTN_SCALE = HEAD_DIM ** -0.5
LANES = 128
INT_MIN = -(2 ** 31)
RADIX_CHECKS = (24, 27)
BF16_EXACT_INT = 256
BF16_SUBLANES = 16

VMEM_LIMIT_BYTES = 56 * 1024 * 1024

F32 = jnp.float32
BF16 = jnp.bfloat16

SEGMENT = 256
INDEX_QUERIES = 128
TILE_ROWS = 16


class _Tiles(NamedTuple):
    norm_rows: int
    proj_rows: int
    proj_cols: int
    out_rows: int
    cmp_queries: int
    keys: int

    @classmethod
    def for_shape(cls, batch, seq):
        t = batch * seq
        return cls(norm_rows=min(512, t), proj_rows=min(1024, t), proj_cols=1024, out_rows=min(256, t),
                   cmp_queries=min(256, seq), keys=min(512, seq))


def _params(semantics):
    return pltpu.CompilerParams(dimension_semantics=semantics, vmem_limit_bytes=VMEM_LIMIT_BYTES)


def _sigmoid(x):
    return 1.0 / (1.0 + jnp.exp(-x))


def _dot(a, b):
    return jnp.dot(a, b, preferred_element_type=F32)


def _dot_nt(a, b):
    return lax.dot_general(a, b, (((1,), (1,)), ((), ())), preferred_element_type=F32)


def _log2(n):
    assert n > 0 and n & (n - 1) == 0, n
    return n.bit_length() - 1


def _div_pow2(x, n):
    return lax.shift_right_logical(x, jnp.int32(_log2(n)))


def _group_slope_scale(g):
    return jnp.where(g == 0, 1.0, 2.0 ** -HEADS_PER_GROUP).astype(F32)


def _rmsnorm_kernel(x_ref, g_ref, o_ref):
    x = x_ref[...]
    ms = jnp.mean(x * x, axis=-1, keepdims=True)
    o_ref[...] = (x * lax.rsqrt(ms + EPS) * g_ref[...]).astype(o_ref.dtype)


def _rmsnorm(x, g, tm):
    t, d = x.shape
    return pl.pallas_call(
        _rmsnorm_kernel,
        grid=(t // tm,),
        in_specs=[pl.BlockSpec((tm, d), lambda i: (i, 0)), pl.BlockSpec((1, d), lambda i: (0, 0))],
        out_specs=pl.BlockSpec((tm, d), lambda i: (i, 0)),
        out_shape=jax.ShapeDtypeStruct((t, d), BF16),
        compiler_params=_params(("parallel",)),
        name="in_rmsnorm",
    )(x, g.reshape(1, d))


def _proj_headnorm_kernel(h_ref, w_ref, g_ref, o_ref):
    y = _dot(h_ref[...], w_ref[...])
    for hd in range(y.shape[1] // HEAD_DIM):
        sl = slice(hd * HEAD_DIM, (hd + 1) * HEAD_DIM)
        yh = y[:, sl]
        ms = jnp.mean(yh * yh, axis=-1, keepdims=True)
        o_ref[:, sl] = (yh * lax.rsqrt(ms + EPS) * g_ref[:, sl]).astype(o_ref.dtype)


def _proj_act_kernel(h_ref, w_ref, o_ref, *, act):
    y = _dot(h_ref[...], w_ref[...])
    if act == "silu":
        y = y * _sigmoid(y)
    elif act == "sigmoid":
        y = _sigmoid(y)
    o_ref[...] = y.astype(o_ref.dtype)


def _proj_split_kernel(h_ref, w_ref, *o_refs):
    y = _dot(h_ref[...], w_ref[...])
    off = 0
    for o_ref in o_refs:
        n = o_ref.shape[1]
        o_ref[...] = y[:, off:off + n].astype(o_ref.dtype)
        off += n


def _proj(h, w, tm, tn, out_dtype, act=None, gain=None):
    t, d = h.shape
    n = w.shape[1]
    in_specs = [pl.BlockSpec((tm, d), lambda i, j: (i, 0)), pl.BlockSpec((d, tn), lambda i, j: (0, j))]
    args = [h, w]
    if gain is not None:
        body = _proj_headnorm_kernel
        in_specs.append(pl.BlockSpec((1, tn), lambda i, j: (0, j)))
        args.append(gain.reshape(1, n))
    else:
        body = functools.partial(_proj_act_kernel, act=act)
    return pl.pallas_call(
        body,
        grid=(t // tm, n // tn),
        in_specs=in_specs,
        out_specs=pl.BlockSpec((tm, tn), lambda i, j: (i, j)),
        out_shape=jax.ShapeDtypeStruct((t, n), out_dtype),
        compiler_params=_params(("parallel", "arbitrary")),
        name="in_proj_" + ("headnorm" if gain is not None else str(act)),
    )(*args)


def _proj_heads_kernel(h_ref, w_ref, o_ref):
    y = _dot(h_ref[...], w_ref[...])
    for hd in range(o_ref.shape[1]):
        o_ref[0, hd] = y[:, hd * IDX_DIM:(hd + 1) * IDX_DIM].astype(o_ref.dtype)


def _proj_heads(h, w, tm, batch, seq):
    t, d = h.shape
    per_batch = seq // tm
    return pl.pallas_call(
        _proj_heads_kernel,
        grid=(t // tm,),
        in_specs=[pl.BlockSpec((tm, d), lambda i: (i, 0)), pl.BlockSpec(w.shape, lambda i: (0, 0))],
        out_specs=pl.BlockSpec((1, IDX_HEADS, tm, IDX_DIM), lambda i: (i // per_batch, 0, i % per_batch, 0)),
        out_shape=jax.ShapeDtypeStruct((batch, IDX_HEADS, seq, IDX_DIM), BF16),
        compiler_params=_params(("parallel",)),
        name="in_proj_heads",
    )(h, w)


def _proj_split(h, w, tm, widths):
    t, d = h.shape
    n = w.shape[1]
    return pl.pallas_call(
        _proj_split_kernel,
        grid=(t // tm,),
        in_specs=[pl.BlockSpec((tm, d), lambda i: (i, 0)), pl.BlockSpec((d, n), lambda i: (0, 0))],
        out_specs=[pl.BlockSpec((tm, wd), lambda i: (i, 0)) for wd in widths],
        out_shape=[jax.ShapeDtypeStruct((t, wd), F32) for wd in widths],
        compiler_params=_params(("parallel",)),
        name="in_proj_split",
    )(h, w)


def _compress_kernel(x_ref, pea_ref, peb_ref, wa_ref, wb_ref, w2_ref, g_ref, o_ref, *, normalize):
    x = x_ref[...]
    nch = x.shape[0]
    ha = _dot((x + pea_ref[...]).astype(BF16), wa_ref[...])
    hb = _dot((x + peb_ref[...]).astype(BF16), wb_ref[...])
    hid = ha + pltpu.roll(hb, nch - 1, axis=0)
    act = (hid * _sigmoid(hid)).astype(BF16)
    for g in range(NSA_KV):
        o = _dot(act[:, g * CMP_HIDDEN:(g + 1) * CMP_HIDDEN], w2_ref[...])
        if normalize:
            ms = jnp.mean(o * o, axis=-1, keepdims=True)
            o = o * lax.rsqrt(ms + EPS) * g_ref[...]
        o_ref[0, :, g * HEAD_DIM:(g + 1) * HEAD_DIM] = o.astype(o_ref.dtype)


def _expand_compress_params(pe, w1):
    half = CMP_LEN // 2
    w = w1.reshape(CMP_LEN, HEAD_DIM, CMP_HIDDEN)
    eye = jnp.eye(NSA_KV, dtype=w1.dtype)

    def expand_w(wh):
        return jnp.einsum("ldj,hg->lhdgj", wh, eye).reshape(half * NSA_KV * HEAD_DIM, NSA_KV * CMP_HIDDEN)

    def expand_pe(ph):
        return jnp.broadcast_to(ph[:, None, :], (half, NSA_KV, HEAD_DIM)).reshape(1, half * NSA_KV * HEAD_DIM)

    return (expand_pe(pe[:half]), expand_pe(pe[half:]),
            expand_w(w[:half]).astype(BF16), expand_w(w[half:]).astype(BF16))


def _compress(raw, pe, w1, w2, gain, batch, seq):
    nch = seq // CMP_STRIDE
    width = CMP_STRIDE * NSA_KV * HEAD_DIM
    x = raw.reshape(batch * nch, width)
    pea, peb, wa, wb = _expand_compress_params(pe, w1)
    normalize = gain is not None
    g = (gain if normalize else jnp.ones((HEAD_DIM,), F32)).reshape(1, HEAD_DIM)
    const = lambda b: (0, 0)
    return pl.pallas_call(
        functools.partial(_compress_kernel, normalize=normalize),
        grid=(batch,),
        in_specs=[pl.BlockSpec((nch, width), lambda b: (b, 0)),
                  pl.BlockSpec((1, width), const), pl.BlockSpec((1, width), const),
                  pl.BlockSpec(wa.shape, const), pl.BlockSpec(wb.shape, const),
                  pl.BlockSpec((CMP_HIDDEN, HEAD_DIM), const), pl.BlockSpec((1, HEAD_DIM), const)],
        out_specs=pl.BlockSpec((1, nch, NSA_KV * HEAD_DIM), lambda b: (b, 0, 0)),
        out_shape=jax.ShapeDtypeStruct((batch, nch, NSA_KV * HEAD_DIM), BF16),
        compiler_params=_params(("parallel",)),
        name="nsa_compress",
    )(x, pea, peb, wa, wb, w2.astype(BF16), g)


def _cmp_to_sel_matrix(nch, n_sel_pad):
    n_cmp = nch - 1
    c0 = np.arange(nch)[:, None] * CMP_STRIDE
    s0 = np.arange(n_sel_pad)[None, :] * SEL_BLOCK
    ov = np.clip(np.minimum(c0 + CMP_LEN, s0 + SEL_BLOCK) - np.maximum(c0, s0), 0, None) / CMP_LEN
    ov[n_cmp:] = 0.0
    return jnp.asarray(ov.T, dtype=BF16)


def _nsa_cmp_kernel(q_ref, kc_ref, vc_ref, mt_ref, eye_ref, tile_of_ref, ocmp_ref, selb_ref, flags_ref,
                    *, tq, nch, n_sel):
    g = pl.program_id(1)
    i = pl.program_id(2)
    t = i * tq + lax.broadcasted_iota(jnp.int32, (tq, 1), 0)
    live = (t >= CMP_LEN - 1).astype(F32)
    gsc = _group_slope_scale(g)

    def attend(width):
        c_end = lax.broadcasted_iota(jnp.int32, (1, width), 1) * CMP_STRIDE + (CMP_LEN - 1)
        vis = c_end <= t
        distc = t.astype(F32) - (c_end.astype(F32) - (CMP_LEN - 1) / 2.0)
        kc = kc_ref[0, :width, :]
        vc = vc_ref[0, :width, :]
        psum = jnp.zeros((tq, width), F32)
        for j in range(HEADS_PER_GROUP):
            slope = (2.0 ** -(j + 1)) * gsc
            sl = slice(j * HEAD_DIM, (j + 1) * HEAD_DIM)
            s = _dot_nt(q_ref[:, sl], kc) - slope * distc
            s = jnp.where(vis, s, NEG)
            e = jnp.exp(s - jnp.max(s, axis=-1, keepdims=True))
            p = e * (live / jnp.sum(e, axis=-1, keepdims=True))
            psum = psum + p
            ocmp_ref[:, sl] = _dot(p.astype(BF16), vc)
        return _dot_nt(mt_ref[:, :width], psum.astype(BF16))

    widths = sorted({min(nch, LANES * (k + 1)) for k in range(pl.cdiv(nch, LANES))})
    visible = ((i + 1) * tq - CMP_LEN) // CMP_STRIDE + 1
    imp = lax.switch(jnp.minimum((visible - 1) // LANES, len(widths) - 1),
                     [functools.partial(attend, w) for w in widths])
    blk = lax.broadcasted_iota(jnp.int32, (LANES, 1), 0)
    blk_t = _div_pow2(i * tq + lax.broadcasted_iota(jnp.int32, (1, tq), 1), SEL_BLOCK)
    valid = blk <= blk_t
    forced = (blk == 0) | (blk == blk_t) | (blk == blk_t - 1)
    vals = jnp.where(valid & jnp.logical_not(forced) & (blk < n_sel), imp, -1.0)
    sel = valid & forced
    for _ in range(SEL_TOPN - 3):
        mx = jnp.max(vals, axis=0, keepdims=True)
        first = jnp.min(jnp.where(vals == mx, blk, LANES), axis=0, keepdims=True)
        pick = blk == first
        sel = sel | pick
        vals = jnp.where(pick, -3.0, vals)
    sel = sel & valid
    masked = jnp.where(sel, 0.0, NEG).astype(BF16)
    selb_ref[...] = _dot_nt(eye_ref[...], masked).astype(selb_ref.dtype)

    t_lane = i * tq + lax.broadcasted_iota(jnp.int32, (1, tq), 1)
    early = blk < (SEGMENT // SEL_BLOCK) * _div_pow2(t_lane, SEGMENT)
    per_query = _dot(tile_of_ref[...], jnp.where(sel & early, 1.0, 0.0).astype(BF16))
    out_lane = lax.broadcasted_iota(jnp.int32, flags_ref.shape[1:], 1)
    flags = jnp.zeros(flags_ref.shape[1:], jnp.int32)
    for h in range(tq // SEGMENT):
        hit = jnp.max(per_query[:, h * SEGMENT:(h + 1) * SEGMENT], axis=1, keepdims=True) > 0.0
        flags = jnp.where((out_lane == h) & hit, 1, flags)
    flags_ref[0] = flags


def _nsa_cmp(qkn, kc, vc, batch, seq, tq, tk):
    t = batch * seq
    nch = seq // CMP_STRIDE
    n_sel = seq // SEL_BLOCK
    n_all = seq // tk
    assert n_sel <= LANES and n_all <= TILE_ROWS and tq % SEGMENT == 0
    nq = seq // tq
    gw = HEADS_PER_GROUP * HEAD_DIM
    mt = _cmp_to_sel_matrix(nch, LANES)
    eye = jnp.asarray(np.eye(tq, dtype=np.float32), dtype=BF16)
    tile_of = jnp.asarray(np.arange(LANES)[None, :] // (tk // SEL_BLOCK) == np.arange(TILE_ROWS)[:, None], dtype=BF16)
    row = lambda b, g, i: (b * nq + i, g)
    ocmp, selb, flags = pl.pallas_call(
        functools.partial(_nsa_cmp_kernel, tq=tq, nch=nch, n_sel=n_sel),
        grid=(batch, NSA_KV, nq),
        in_specs=[pl.BlockSpec((tq, gw), row),
                  pl.BlockSpec((1, nch, HEAD_DIM), lambda b, g, i: (b, 0, g)),
                  pl.BlockSpec((1, nch, HEAD_DIM), lambda b, g, i: (b, 0, g)),
                  pl.BlockSpec((LANES, nch), lambda b, g, i: (0, 0)),
                  pl.BlockSpec((tq, tq), lambda b, g, i: (0, 0)),
                  pl.BlockSpec((TILE_ROWS, LANES), lambda b, g, i: (0, 0))],
        out_specs=[pl.BlockSpec((tq, gw), row), pl.BlockSpec((tq, LANES), row),
                   pl.BlockSpec((1, TILE_ROWS, LANES), lambda b, g, i: ((b * NSA_KV + g) * nq + i, 0, 0))],
        out_shape=[jax.ShapeDtypeStruct((t, NSA_HEADS * HEAD_DIM), F32),
                   jax.ShapeDtypeStruct((t, NSA_KV * LANES), BF16),
                   jax.ShapeDtypeStruct((batch * NSA_KV * nq, TILE_ROWS, LANES), jnp.int32)],
        compiler_params=_params(("parallel", "parallel", "parallel")),
        name="nsa_cmp_select",
    )(qkn, kc, vc, mt, eye, tile_of)
    segs = tq // SEGMENT
    tile_flags = flags[:, :n_all, :segs].transpose(0, 2, 1).reshape(batch, NSA_KV, nq * segs, n_all)
    return ocmp, selb, tile_flags


def _key_aux_blocks(seq):
    j = np.arange(seq)
    aux = (j[:, None] // SEL_BLOCK == np.arange(LANES)[None, :]).astype(np.float32)
    aux[:, 0] = j % SEL_BLOCK
    return jnp.asarray(aux, dtype=BF16)


def _key_aux_position(seq):
    assert seq <= BF16_EXACT_INT * BF16_EXACT_INT
    j = np.arange(seq)
    aux = np.zeros((seq, LANES), np.float32)
    aux[:, 0] = j % BF16_EXACT_INT
    aux[:, 1] = j - j % BF16_EXACT_INT
    return jnp.asarray(aux, dtype=BF16)


def _query_aug(q_ref, aux_cols):
    parts = [jnp.concatenate([q_ref[:, j * HEAD_DIM:(j + 1) * HEAD_DIM], aux_cols[j].astype(BF16)], axis=1)
             for j in range(HEADS_PER_GROUP)]
    return jnp.concatenate(parts, axis=0)


def _flash_step(s, vt, carry):
    m, l, acc = carry
    m_new = jnp.maximum(m, jnp.max(s, axis=0, keepdims=True))
    alpha = jnp.exp(m - m_new)
    p = jnp.exp(s - m_new).astype(BF16)
    pv = _dot(jnp.concatenate([vt, jnp.ones((BF16_SUBLANES, vt.shape[1]), BF16)], axis=0), p)
    return m_new, alpha * l + pv[HEAD_DIM:HEAD_DIM + 1], alpha * acc + pv[:HEAD_DIM]


def _flash_init(cols):
    return (jnp.full((1, cols), NEG, F32), jnp.zeros((1, cols), F32), jnp.zeros((HEAD_DIM, cols), F32))


def _transposed_values(v, batch, seq, groups):
    return v.reshape(batch, seq // SEGMENT, SEGMENT, groups, HEAD_DIM).transpose(0, 3, 1, 4, 2)


def _pipelined_tiles(n, n_all, scores_fn, consume_fn, s_a, s_b, carry):
    s_a[...] = scores_fn(0)

    def pairs(k, count, carry):
        for c in range(count):
            s_b[...] = scores_fn(k + 2 * c + 1)
            carry = consume_fn(k + 2 * c, s_a[...], carry)
            s_a[...] = scores_fn(jnp.minimum(k + 2 * c + 2, n_all - 1))
            carry = consume_fn(k + 2 * c + 1, s_b[...], carry)
        return carry

    quads = lax.shift_right_logical(n, 2)
    carry = lax.fori_loop(0, quads, lambda u, c: pairs(4 * u, 2, c), carry)
    done = 4 * quads
    carry = lax.cond((n & 2) == 2, lambda c: pairs(done, 1, c), lambda c: c, carry)
    return lax.cond((n & 1) == 1, lambda c: consume_fn(n - 1, s_a[...], c), lambda c: c, carry)


def _nsa_main_kernel(tiles_ref, count_ref, q_ref, ks_ref, vst_ref, kw_ref, vwt_ref, kaux_ref, kpos_ref, selb_ref,
                     ocmp_ref, gate_ref, z_ref, o_ref, s_a, s_b, *, tq, tk, seq):
    b = pl.program_id(0)
    g = pl.program_id(1)
    i = pl.program_id(2)
    n_all = seq // tk
    per_tile = tk // SEGMENT
    t0 = pl.multiple_of(i * tq, tq)
    cols = HEADS_PER_GROUP * tq
    gsc = _group_slope_scale(g)
    t_cols = t0 + (lax.broadcasted_iota(jnp.int32, (1, cols), 1) & (tq - 1))
    slopes = [(2.0 ** -(j + 1)) * gsc for j in range(HEADS_PER_GROUP)]

    def values(ref, first, count):
        return jnp.concatenate([ref[0, 0, first + c] for c in range(count)], axis=1)

    lane = lax.broadcasted_iota(jnp.int32, (tq, LANES), 1)
    chosen = selb_ref[...].astype(F32) > -1.0
    own_first = _div_pow2(t0, SEL_BLOCK)
    block_start = (lane * SEL_BLOCK).astype(F32)

    def masked_queries(blocks):
        return _query_aug(q_ref, [jnp.where(lane == 0, sl, jnp.where(chosen & blocks, sl * block_start, NEG))
                                  for sl in slopes])

    q_aug = masked_queries(lane < own_first)

    slot = (b * NSA_KV + g) * (seq // tq) + i
    base = slot * n_all

    def scores(u):
        j0 = pl.multiple_of(tiles_ref[base + u] * tk, tk)
        return _dot_nt(jnp.concatenate([ks_ref[pl.ds(j0, tk), :], kaux_ref[pl.ds(j0, tk), :]], axis=1), q_aug)

    def consume(u, s, carry):
        return _flash_step(s, values(vst_ref, tiles_ref[base + u] * per_tile, per_tile), carry)

    carry = _pipelined_tiles(count_ref[slot], n_all, scores, consume, s_a, s_b, _flash_init(cols))

    j_own = t0 + lax.broadcasted_iota(jnp.int32, (tq, 1), 0)
    s = _dot_nt(jnp.concatenate([ks_ref[pl.ds(t0, tq), :], kaux_ref[pl.ds(t0, tq), :]], axis=1),
                masked_queries(lane >= own_first))
    s = jnp.where(j_own <= t_cols, s, NEG)
    _, l_s, acc_s = _flash_step(s, values(vst_ref, i, 1), carry)
    o_slc = acc_s / l_s

    q_pos = _query_aug(q_ref, [jnp.where(lane < 2, sl, 0.0) for sl in slopes])
    wk = WINDOW + tq
    start = pl.multiple_of(jnp.maximum(t0 - WINDOW, 0), SEGMENT)
    j_win = start + lax.broadcasted_iota(jnp.int32, (wk, 1), 0)
    s = _dot_nt(jnp.concatenate([kw_ref[pl.ds(start, wk), :], kpos_ref[pl.ds(start, wk), :]], axis=1), q_pos)
    s = jnp.where((j_win <= t_cols) & (j_win > t_cols - WINDOW), s, NEG)
    _, l_w, acc_w = _flash_step(s, values(vwt_ref, _div_pow2(start, SEGMENT), wk // SEGMENT), _flash_init(cols))
    o_win = acc_w / l_w

    for j in range(HEADS_PER_GROUP):
        sl = slice(j * HEAD_DIM, (j + 1) * HEAD_DIM)
        cs = slice(j * tq, (j + 1) * tq)
        gates = [_sigmoid(gate_ref[:, 3 * j + c:3 * j + c + 1]) for c in range(3)]
        o = gates[0] * ocmp_ref[:, sl] + gates[1] * o_slc[:, cs].T + gates[2] * o_win[:, cs].T
        o_ref[:, sl] = (o * z_ref[:, sl]).astype(o_ref.dtype)


def _nsa_main(qkn, vals, selb, tile_flags, ocmp, gates, zact, batch, seq, tq, tk):
    t = batch * seq
    nq = seq // tq
    n_all = seq // tk
    gw = HEADS_PER_GROUP * HEAD_DIM
    flags = tile_flags.reshape(batch * NSA_KV * nq, n_all)
    tiles = jnp.argsort(1 - flags, axis=-1, stable=True).astype(jnp.int32).reshape(-1)
    counts = jnp.sum(flags, axis=-1).astype(jnp.int32)
    row = lambda b, g, i, *_: (b * nq + i, g)
    ks_blk = (NSA_HEADS * HEAD_DIM) // HEAD_DIM
    kw_blk = ks_blk + NSA_KV
    grid_spec = pltpu.PrefetchScalarGridSpec(
        num_scalar_prefetch=2,
        grid=(batch, NSA_KV, nq),
        in_specs=[pl.BlockSpec((tq, gw), row),
                  pl.BlockSpec((seq, HEAD_DIM), lambda b, g, i, *_: (b, ks_blk + g)),
                  pl.BlockSpec((1, 1, seq // SEGMENT, HEAD_DIM, SEGMENT), lambda b, g, i, *_: (b, g, 0, 0, 0)),
                  pl.BlockSpec((seq, HEAD_DIM), lambda b, g, i, *_: (b, kw_blk + g)),
                  pl.BlockSpec((1, 1, seq // SEGMENT, HEAD_DIM, SEGMENT), lambda b, g, i, *_: (b, g, 0, 0, 0)),
                  pl.BlockSpec((seq, LANES), lambda b, g, i, *_: (0, 0)),
                  pl.BlockSpec((seq, LANES), lambda b, g, i, *_: (0, 0)),
                  pl.BlockSpec((tq, LANES), row),
                  pl.BlockSpec((tq, gw), row),
                  pl.BlockSpec((tq, LANES), row),
                  pl.BlockSpec((tq, gw), row)],
        out_specs=pl.BlockSpec((tq, gw), row),
        scratch_shapes=[pltpu.VMEM((tk, HEADS_PER_GROUP * tq), F32)] * 2,
    )
    width = NSA_KV * HEAD_DIM
    vst = _transposed_values(vals[:, :width], batch, seq, NSA_KV)
    vwt = _transposed_values(vals[:, width:2 * width], batch, seq, NSA_KV)
    return pl.pallas_call(
        functools.partial(_nsa_main_kernel, tq=tq, tk=tk, seq=seq),
        grid_spec=grid_spec,
        out_shape=jax.ShapeDtypeStruct((t, NSA_HEADS * HEAD_DIM), BF16),
        compiler_params=_params(("parallel", "parallel", "arbitrary")),
        name="nsa_select_window",
    )(tiles, counts, qkn, qkn, vst, qkn, vwt, _key_aux_blocks(seq), _key_aux_position(seq), selb, ocmp, gates,
      zact)


def _threshold_value(code):
    return pltpu.bitcast(jnp.where(code < 0, code ^ 0x7FFFFFFF, code), F32)


def _dsa_index_kernel(qi_ref, ki_ref, wi_ref, tri_ref, bias_ref, score_scr, *, tq, tk, seq, topk):
    i = pl.program_id(1)
    t0 = i * tq
    n_all = seq // tk
    n_kt = _div_pow2(t0 + tq + tk - 1, tk)
    t = t0 + lax.broadcasted_iota(jnp.int32, (1, tq), 1)
    wrow = wi_ref[...] * (IDX_HEADS ** -0.5 * IDX_DIM ** -0.5)
    hb = 4

    def key_ids(kt):
        return kt * tk + lax.broadcasted_iota(jnp.int32, (tk, 1), 0)

    def score_tile(kt):
        j0 = pl.multiple_of(kt * tk, tk)
        ki = ki_ref[pl.ds(j0, tk), :]
        score = jnp.zeros((tk, tq), F32)
        for h0 in range(0, IDX_HEADS, hb):
            x = _dot_nt(ki, qi_ref[0, h0:h0 + hb].reshape(hb * tq, IDX_DIM))
            for h in range(hb):
                score = score + jnp.maximum(x[:, h * tq:(h + 1) * tq], 0.0) * wrow[h0 + h:h0 + h + 1, :]
        score = jnp.where(key_ids(kt) <= t, score, -jnp.inf)
        score_scr[kt] = score
        return score

    def hits(pred, score, c):
        return c + jnp.sum(jnp.where(pred(score), 1, 0).reshape(tk // 8, 8, tq), axis=0)

    def non_negative(s):
        return s >= 0.0

    n_pairs = lax.shift_right_logical(n_kt + 1, 1)

    def score_step(u, c):
        for kt in (2 * u, 2 * u + 1):
            c = hits(non_negative, score_tile(kt), c)
        return c

    c0 = jnp.sum(lax.fori_loop(0, n_pairs, score_step, jnp.zeros((8, tq), jnp.int32)), axis=0, keepdims=True)

    def count(pred):
        def step(u, c):
            for kt in (2 * u, 2 * u + 1):
                c = hits(pred, score_scr[kt], c)
            return c
        c = lax.fori_loop(0, n_pairs, step, jnp.zeros((8, tq), jnp.int32))
        return jnp.sum(c, axis=0, keepdims=True)

    everything = jnp.int32(seq + tk)
    state = (jnp.where(c0 >= topk, 0, INT_MIN), jnp.where(c0 >= topk, c0, everything))

    def bit_step(b, state):
        v, n_ge = state
        cand = v | (jnp.int32(1) << (30 - b))
        threshold = _threshold_value(cand)
        c = count(lambda s: s >= threshold)
        return jnp.where(c >= topk, cand, v), jnp.where(c >= topk, c, n_ge)

    def settled(state):
        done = (state[1] == topk) | (t < topk)
        return jnp.min(jnp.where(done, 1, 0)) > 0

    state = lax.fori_loop(0, RADIX_CHECKS[0], bit_step, state)
    for lo, hi in zip(RADIX_CHECKS, RADIX_CHECKS[1:] + (31,)):
        state = lax.cond(settled(state), lambda s: s, lambda s, lo=lo, hi=hi: lax.fori_loop(lo, hi, bit_step, s), state)
    v, n_ge = state
    threshold = _threshold_value(v)
    take_all = v == INT_MIN

    def emit(kt, keep):
        bias_ref[0, 0, kt] = jnp.where(keep & (key_ids(kt) <= t), 0.0, NEG).astype(bias_ref.dtype)

    def fill_step(kt, _):
        bias_ref[0, 0, kt] = jnp.full((tk, tq), NEG, bias_ref.dtype)
        return 0

    def emit_no_ties(_):
        def step(u, c):
            for kt in (2 * u, 2 * u + 1):
                emit(kt, (score_scr[kt] >= threshold) | take_all)
            return c
        return lax.fori_loop(0, n_pairs, step, 0)

    def emit_with_ties(_):
        need = (topk - count(lambda s: s > threshold)).astype(F32)

        def step(kt, carry):
            score = score_scr[kt]
            eq = score == threshold
            eqf = jnp.where(eq, 1.0, 0.0)
            before = carry + _dot(tri_ref[...], eqf.astype(BF16)) - eqf
            emit(kt, (score > threshold) | (eq & (before < need)) | take_all)
            return carry + jnp.sum(eqf, axis=0, keepdims=True)

        lax.fori_loop(0, n_kt, step, jnp.zeros((1, tq), F32))
        return lax.fori_loop(n_kt, 2 * n_pairs, fill_step, 0)

    over = (n_ge > topk) & jnp.logical_not(take_all)
    lax.cond(jnp.max(jnp.where(over, 1, 0)) > 0, emit_with_ties, emit_no_ties, 0)
    lax.fori_loop(2 * n_pairs, n_all, fill_step, 0)


def _dsa_index(qi_heads, ki, wi_t, batch, seq, tq, tk):
    nq = seq // tq
    n_all = seq // tk
    topk = min(DSA_TOPK_MAX, seq // 4)
    tri = jnp.asarray(np.tril(np.ones((tk, tk), np.float32)), dtype=BF16)
    return pl.pallas_call(
        functools.partial(_dsa_index_kernel, tq=tq, tk=tk, seq=seq, topk=topk),
        grid=(batch, nq),
        in_specs=[pl.BlockSpec((1, IDX_HEADS, tq, IDX_DIM), lambda b, i: (b, 0, i, 0)),
                  pl.BlockSpec((seq, IDX_DIM), lambda b, i: (b, 0)),
                  pl.BlockSpec((IDX_HEADS, tq), lambda b, i: (0, b * nq + i)),
                  pl.BlockSpec((tk, tk), lambda b, i: (0, 0))],
        out_specs=pl.BlockSpec((1, 1, n_all, tk, tq), lambda b, i: (b, i, 0, 0, 0)),
        out_shape=jax.ShapeDtypeStruct((batch, nq, n_all, tk, tq), BF16),
        scratch_shapes=[pltpu.VMEM((n_all, tk, tq), F32)],
        compiler_params=_params(("parallel", "arbitrary")),
        name="dsa_index_topk",
    )(qi_heads, ki, wi_t, tri)


def _dsa_attn_kernel(q_ref, k_ref, vt_ref, kaux_ref, bias_ref, z_ref, o_ref, s_a, s_b, *, tq, tk, seq):
    g = pl.program_id(1)
    i = pl.program_id(2)
    t0 = i * tq
    cols = HEADS_PER_GROUP * tq
    gsc = _group_slope_scale(g)
    slopes = [(2.0 ** -(j + 1)) * gsc for j in range(HEADS_PER_GROUP)]
    lane = lax.broadcasted_iota(jnp.int32, (tq, LANES), 1)
    q_aug = _query_aug(q_ref, [jnp.where(lane < 2, sl, 0.0) for sl in slopes])
    per_tile = tk // SEGMENT

    def scores(kt):
        j0 = pl.multiple_of(kt * tk, tk)
        return _dot_nt(jnp.concatenate([k_ref[pl.ds(j0, tk), :], kaux_ref[pl.ds(j0, tk), :]], axis=1), q_aug)

    def consume(kt, s, carry):
        bias = [bias_ref[0, part, kt].astype(F32) for part in range(bias_ref.shape[1])]
        s = s + jnp.concatenate(bias * HEADS_PER_GROUP, axis=1)
        vt = jnp.concatenate([vt_ref[0, 0, kt * per_tile + c] for c in range(per_tile)], axis=1)
        return _flash_step(s, vt, carry)

    n_kt = _div_pow2(t0 + tq + tk - 1, tk)
    _, l, acc = _pipelined_tiles(n_kt, seq // tk, scores, consume, s_a, s_b, _flash_init(cols))
    o = acc / l
    for j in range(HEADS_PER_GROUP):
        sl = slice(j * HEAD_DIM, (j + 1) * HEAD_DIM)
        o_ref[:, sl] = (o[:, j * tq:(j + 1) * tq].T * z_ref[:, sl]).astype(o_ref.dtype)


def _dsa_attn(qkn, vals, bias, zact, batch, seq, tq, tk):
    t = batch * seq
    nq = seq // tq
    gw = HEADS_PER_GROUP * HEAD_DIM
    row = lambda b, g, i: (b * nq + i, g)
    q_blk = (NSA_HEADS + 2 * NSA_KV) * HEAD_DIM // gw
    k_blk = (NSA_HEADS + 2 * NSA_KV + DSA_HEADS)
    v_off = 2 * NSA_KV * HEAD_DIM
    vt = _transposed_values(vals[:, v_off:v_off + DSA_KV * HEAD_DIM], batch, seq, DSA_KV)
    return pl.pallas_call(
        functools.partial(_dsa_attn_kernel, tq=tq, tk=tk, seq=seq),
        grid=(batch, DSA_KV, nq),
        in_specs=[pl.BlockSpec((tq, gw), lambda b, g, i: (b * nq + i, q_blk + g)),
                  pl.BlockSpec((seq, HEAD_DIM), lambda b, g, i: (b, k_blk + g)),
                  pl.BlockSpec((1, 1, seq // SEGMENT, HEAD_DIM, SEGMENT), lambda b, g, i: (b, g, 0, 0, 0)),
                  pl.BlockSpec((seq, LANES), lambda b, g, i: (0, 0)),
                  pl.BlockSpec((1, tq // bias.shape[4]) + bias.shape[2:], lambda b, g, i: (b, i, 0, 0, 0)),
                  pl.BlockSpec((tq, gw), lambda b, g, i: (b * nq + i, NSA_KV + g))],
        out_specs=pl.BlockSpec((tq, gw), row),
        out_shape=jax.ShapeDtypeStruct((t, DSA_HEADS * HEAD_DIM), BF16),
        scratch_shapes=[pltpu.VMEM((tk, HEADS_PER_GROUP * tq), F32)] * 2,
        compiler_params=_params(("parallel", "parallel", "arbitrary")),
        name="dsa_attention",
    )(qkn, qkn, vt, _key_aux_position(seq), bias, zact)


def _merge_kernel(on_ref, od_ref, sn_ref, sd_ref, x_ref, wun_ref, wud_ref, wout_ref, o_ref):
    y = sn_ref[...] * _dot(on_ref[...], wun_ref[...]) + sd_ref[...] * _dot(od_ref[...], wud_ref[...])
    o_ref[...] = x_ref[...] + _dot(y.astype(BF16), wout_ref[...])


def _merge(o_nsa, o_dsa, gate_act, x, w_up_nsa, w_up_dsa, w_out, tm):
    t, d = x.shape
    const = lambda i: (0, 0)
    return pl.pallas_call(
        _merge_kernel,
        grid=(t // tm,),
        in_specs=[pl.BlockSpec((tm, o_nsa.shape[1]), lambda i: (i, 0)),
                  pl.BlockSpec((tm, o_dsa.shape[1]), lambda i: (i, 0)),
                  pl.BlockSpec((tm, d), lambda i: (i, 0)),
                  pl.BlockSpec((tm, d), lambda i: (i, 1)),
                  pl.BlockSpec((tm, d), lambda i: (i, 0)),
                  pl.BlockSpec(w_up_nsa.shape, const), pl.BlockSpec(w_up_dsa.shape, const),
                  pl.BlockSpec(w_out.shape, const)],
        out_specs=pl.BlockSpec((tm, d), lambda i: (i, 0)),
        out_shape=jax.ShapeDtypeStruct((t, d), F32),
        compiler_params=_params(("parallel",)),
        name="merge_out_proj",
    )(o_nsa, o_dsa, gate_act, gate_act, x, w_up_nsa, w_up_dsa, w_out)


def _ple_kernel(x_ref, p_ref, g_ref, wg_ref, wp_ref, o_ref):
    x = x_ref[...]
    ms = jnp.mean(x * x, axis=-1, keepdims=True)
    r = (x * lax.rsqrt(ms + EPS) * g_ref[...]).astype(BF16)
    gate = _sigmoid(_dot(r, wg_ref[...]))
    o_ref[...] = x + _dot(p_ref[...].astype(BF16), wp_ref[...]) * gate


def _ple(x, p, g, w_gate, w_proj, tm):
    t, d = x.shape
    const = lambda i: (0, 0)
    return pl.pallas_call(
        _ple_kernel,
        grid=(t // tm,),
        in_specs=[pl.BlockSpec((tm, d), lambda i: (i, 0)),
                  pl.BlockSpec((tm, p.shape[1]), lambda i: (i, 0)),
                  pl.BlockSpec((1, d), const),
                  pl.BlockSpec(w_gate.shape, const), pl.BlockSpec(w_proj.shape, const)],
        out_specs=pl.BlockSpec((tm, d), lambda i: (i, 0)),
        out_shape=jax.ShapeDtypeStruct((t, d), F32),
        compiler_params=_params(("parallel",)),
        name="ple_gate",
    )(x, p, g.reshape(1, d), w_gate, w_proj)


def _split_in_proj(w):
    widths = [NSA_HEADS * HEAD_DIM] + [NSA_KV * HEAD_DIM] * 6 + [NSA_HEADS * 3, NSA_HEADS * HEAD_DIM,
              DSA_HEADS * HEAD_DIM, DSA_KV * HEAD_DIM, DSA_KV * HEAD_DIM, IDX_HEADS * IDX_DIM, IDX_DIM,
              IDX_HEADS, DSA_HEADS * HEAD_DIM, D_MODEL, D_MODEL]
    offs = np.concatenate([[0], np.cumsum(widths)])
    assert offs[-1] == w.shape[1]
    (nq, nkc, nvc, nks, nvs, nkw, nvw, ng, nz, dq, dk, dv, iq, ik, iw, dz, mgn, mgd) = [
        w[:, offs[k]:offs[k + 1]] for k in range(len(widths))]
    pad = jnp.zeros((w.shape[0], LANES - IDX_DIM - IDX_HEADS - NSA_HEADS * 3), w.dtype)
    return dict(
        normed=jnp.concatenate([nq, nks, nkw, dq, dk], axis=1).astype(BF16),
        vals=jnp.concatenate([nvs, nvw, dv], axis=1).astype(BF16),
        iq=iq.astype(BF16),
        raw=jnp.concatenate([nkc, nvc, ik, iw, ng, pad], axis=1).astype(BF16),
        zact=jnp.concatenate([nz, dz], axis=1).astype(BF16),
        merge=jnp.concatenate([mgn, mgd], axis=1).astype(BF16),
    )


def _layer(x, p, norm_g, w_in, nsa_q_g, nsa_kc_g, nsa_ks_g, nsa_kw_g, cmp_pe_k, cmp_w1_k, cmp_w2_k,
           cmp_pe_v, cmp_w1_v, cmp_w2_v, dsa_q_g, dsa_k_g, w_up_nsa, w_up_dsa, w_out, ple_norm_g,
           w_ple_gate, w_ple_proj):
    batch, seq, d = x.shape
    t = batch * seq
    assert seq % SEGMENT == 0 and seq >= WINDOW + SEGMENT
    x2 = x.reshape(t, d)
    tiles = _Tiles.for_shape(batch, seq)
    tm = tiles.proj_rows

    w = _split_in_proj(w_in)
    h = _rmsnorm(x2, norm_g, tiles.norm_rows)

    gains = jnp.concatenate([jnp.tile(nsa_q_g * ATTN_SCALE, NSA_HEADS), jnp.tile(nsa_ks_g, NSA_KV),
                             jnp.tile(nsa_kw_g, NSA_KV), jnp.tile(dsa_q_g * ATTN_SCALE, DSA_HEADS),
                             jnp.tile(dsa_k_g, DSA_KV)])
    qkn = _proj(h, w["normed"], tm, w["normed"].shape[1] // 2, BF16, gain=gains)
    vals = _proj(h, w["vals"], tm, w["vals"].shape[1], BF16)
    qi_heads = _proj_heads(h, w["iq"], min(tm, seq), batch, seq)
    kc_raw, vc_raw, misc = _proj_split(h, w["raw"], tm, (NSA_KV * HEAD_DIM, NSA_KV * HEAD_DIM, LANES))
    zact = _proj(h, w["zact"], tm, tiles.proj_cols, F32, act="silu")
    gate_act = _proj(h, w["merge"], tm, tiles.proj_cols, F32, act="sigmoid")

    ki = misc[:, :IDX_DIM].astype(BF16)
    wi = misc[:, IDX_DIM:IDX_DIM + IDX_HEADS]
    ng = misc[:, IDX_DIM + IDX_HEADS:IDX_DIM + IDX_HEADS + NSA_HEADS * 3]
    per_group = HEADS_PER_GROUP * 3
    gates = jnp.pad(ng.reshape(t, NSA_KV, per_group), ((0, 0), (0, 0), (0, LANES - per_group))).reshape(t, NSA_KV * LANES)

    kc = _compress(kc_raw, cmp_pe_k, cmp_w1_k, cmp_w2_k, nsa_kc_g, batch, seq)
    vc = _compress(vc_raw, cmp_pe_v, cmp_w1_v, cmp_w2_v, None, batch, seq)

    tk = tiles.keys
    ocmp, selb, tile_flags = _nsa_cmp(qkn, kc, vc, batch, seq, tiles.cmp_queries, tk)
    o_nsa = _nsa_main(qkn, vals, selb, tile_flags, ocmp, gates, zact, batch, seq, SEGMENT, tk)

    bias = _dsa_index(qi_heads, ki, wi.T, batch, seq, INDEX_QUERIES, tk)
    o_dsa = _dsa_attn(qkn, vals, bias, zact, batch, seq, SEGMENT, tk)

    x1 = _merge(o_nsa, o_dsa, gate_act, x2, w_up_nsa.astype(BF16), w_up_dsa.astype(BF16), w_out.astype(BF16),
                tiles.out_rows)
    x3 = _ple(x1, p.reshape(t, PLE_DIM), ple_norm_g, w_ple_gate.astype(BF16), w_ple_proj.astype(BF16),
              tiles.out_rows)
    return x3.reshape(batch, seq, d)


def kernel(x, p, norm_g, w_in, nsa_q_g, nsa_kc_g, nsa_ks_g, nsa_kw_g, cmp_pe_k, cmp_w1_k, cmp_w2_k, cmp_pe_v, cmp_w1_v, cmp_w2_v, dsa_q_g, dsa_k_g, w_up_nsa, w_up_dsa, w_out, ple_norm_g, w_ple_gate, w_ple_proj):
    depth = w_in.shape[0]
    for i in range(depth):
        x = _layer(x, p[i], norm_g[i], w_in[i], nsa_q_g[i], nsa_kc_g[i], nsa_ks_g[i], nsa_kw_g[i],
                   cmp_pe_k[i], cmp_w1_k[i], cmp_w2_k[i], cmp_pe_v[i], cmp_w1_v[i], cmp_w2_v[i],
                   dsa_q_g[i], dsa_k_g[i], w_up_nsa[i], w_up_dsa[i], w_out[i], ple_norm_g[i],
                   w_ple_gate[i], w_ple_proj[i])
    return x
```

```python
import functools
from typing import NamedTuple

import numpy as np
import jax
import jax.numpy as jnp
from jax import lax
from jax.experimental import pallas as pl
from jax.experimental.pallas import tpu as pltpu

D_MODEL = 2048
HEAD_DIM = 128
NSA_HEADS = 8
NSA_KV = 2
DSA_HEADS = 8
DSA_KV = 2
HEADS_PER_GROUP = 4
CMP_LEN = 32
CMP_STRIDE = 16
CMP_HIDDEN = 256
SEL_BLOCK = 64
SEL_TOPN = 16
WINDOW = 512
IDX_HEADS = 16
IDX_DIM = 64
DSA_TOPK_MAX = 256
PLE_DIM = 256
EPS = 1e-6
NEG = -1e30
ATTN_SCALE = HEAD_DIM ** -0.5
LANES = 128
INT_MIN = -(2 ** 31)
RADIX_CHECKS = (24, 27)
BF16_EXACT_INT = 256
BF16_SUBLANES = 16

VMEM_LIMIT_BYTES = 56 * 1024 * 1024

F32 = jnp.float32
BF16 = jnp.bfloat16

SEGMENT = 256
INDEX_QUERIES = 128
TILE_ROWS = 16


class _Tiles(NamedTuple):
    norm_rows: int
    proj_rows: int
    proj_cols: int
    out_rows: int
    cmp_queries: int
    keys: int

    @classmethod
    def for_shape(cls, batch, seq):
        t = batch * seq
        return cls(norm_rows=min(512, t), proj_rows=min(1024, t), proj_cols=1024, out_rows=min(256, t),
                   cmp_queries=min(256, seq), keys=min(512, seq))


def _params(semantics):
    return pltpu.CompilerParams(dimension_semantics=semantics, vmem_limit_bytes=VMEM_LIMIT_BYTES)


def _sigmoid(x):
    return 1.0 / (1.0 + jnp.exp(-x))


def _dot(a, b):
    return jnp.dot(a, b, preferred_element_type=F32)


def _dot_nt(a, b):
    return lax.dot_general(a, b, (((1,), (1,)), ((), ())), preferred_element_type=F32)


def _log2(n):
    assert n > 0 and n & (n - 1) == 0, n
    return n.bit_length() - 1


def _div_pow2(x, n):
    return lax.shift_right_logical(x, jnp.int32(_log2(n)))


def _group_slope_scale(g):
    return jnp.where(g == 0, 1.0, 2.0 ** -HEADS_PER_GROUP).astype(F32)


def _rmsnorm_kernel(x_ref, g_ref, o_ref):
    x = x_ref[...]
    ms = jnp.mean(x * x, axis=-1, keepdims=True)
    o_ref[...] = (x * lax.rsqrt(ms + EPS) * g_ref[...]).astype(o_ref.dtype)


def _rmsnorm(x, g, tm):
    t, d = x.shape
    return pl.pallas_call(
        _rmsnorm_kernel,
        grid=(t // tm,),
        in_specs=[pl.BlockSpec((tm, d), lambda i: (i, 0)), pl.BlockSpec((1, d), lambda i: (0, 0))],
        out_specs=pl.BlockSpec((tm, d), lambda i: (i, 0)),
        out_shape=jax.ShapeDtypeStruct((t, d), BF16),
        compiler_params=_params(("parallel",)),
        name="in_rmsnorm",
    )(x, g.reshape(1, d))


def _proj_headnorm_kernel(h_ref, w_ref, g_ref, o_ref):
    y = _dot(h_ref[...], w_ref[...])
    for hd in range(y.shape[1] // HEAD_DIM):
        sl = slice(hd * HEAD_DIM, (hd + 1) * HEAD_DIM)
        yh = y[:, sl]
        ms = jnp.mean(yh * yh, axis=-1, keepdims=True)
        o_ref[:, sl] = (yh * lax.rsqrt(ms + EPS) * g_ref[:, sl]).astype(o_ref.dtype)


def _proj_act_kernel(h_ref, w_ref, o_ref, *, act):
    y = _dot(h_ref[...], w_ref[...])
    if act == "silu":
        y = y * _sigmoid(y)
    elif act == "sigmoid":
        y = _sigmoid(y)
    o_ref[...] = y.astype(o_ref.dtype)


def _proj_split_kernel(h_ref, w_ref, *o_refs):
    y = _dot(h_ref[...], w_ref[...])
    off = 0
    for o_ref in o_refs:
        n = o_ref.shape[1]
        o_ref[...] = y[:, off:off + n].astype(o_ref.dtype)
        off += n


def _proj(h, w, tm, tn, out_dtype, act=None, gain=None):
    t, d = h.shape
    n = w.shape[1]
    in_specs = [pl.BlockSpec((tm, d), lambda i, j: (i, 0)), pl.BlockSpec((d, tn), lambda i, j: (0, j))]
    args = [h, w]
    if gain is not None:
        body = _proj_headnorm_kernel
        in_specs.append(pl.BlockSpec((1, tn), lambda i, j: (0, j)))
        args.append(gain.reshape(1, n))
    else:
        body = functools.partial(_proj_act_kernel, act=act)
    return pl.pallas_call(
        body,
        grid=(t // tm, n // tn),
        in_specs=in_specs,
        out_specs=pl.BlockSpec((tm, tn), lambda i, j: (i, j)),
        out_shape=jax.ShapeDtypeStruct((t, n), out_dtype),
        compiler_params=_params(("parallel", "arbitrary")),
        name="in_proj_" + ("headnorm" if gain is not None else str(act)),
    )(*args)


def _proj_heads_kernel(h_ref, w_ref, o_ref):
    y = _dot(h_ref[...], w_ref[...])
    for hd in range(o_ref.shape[1]):
        o_ref[0, hd] = y[:, hd * IDX_DIM:(hd + 1) * IDX_DIM].astype(o_ref.dtype)


def _proj_heads(h, w, tm, batch, seq):
    t, d = h.shape
    per_batch = seq // tm
    return pl.pallas_call(
        _proj_heads_kernel,
        grid=(t // tm,),
        in_specs=[pl.BlockSpec((tm, d), lambda i: (i, 0)), pl.BlockSpec(w.shape, lambda i: (0, 0))],
        out_specs=pl.BlockSpec((1, IDX_HEADS, tm, IDX_DIM), lambda i: (i // per_batch, 0, i % per_batch, 0)),
        out_shape=jax.ShapeDtypeStruct((batch, IDX_HEADS, seq, IDX_DIM), BF16),
        compiler_params=_params(("parallel",)),
        name="in_proj_heads",
    )(h, w)


def _proj_split(h, w, tm, widths):
    t, d = h.shape
    n = w.shape[1]
    return pl.pallas_call(
        _proj_split_kernel,
        grid=(t // tm,),
        in_specs=[pl.BlockSpec((tm, d), lambda i: (i, 0)), pl.BlockSpec((d, n), lambda i: (0, 0))],
        out_specs=[pl.BlockSpec((tm, wd), lambda i: (i, 0)) for wd in widths],
        out_shape=[jax.ShapeDtypeStruct((t, wd), F32) for wd in widths],
        compiler_params=_params(("parallel",)),
        name="in_proj_split",
    )(h, w)


def _compress_kernel(x_ref, pea_ref, peb_ref, wa_ref, wb_ref, w2_ref, g_ref, o_ref, *, normalize):
    x = x_ref[...]
    nch = x.shape[0]
    ha = _dot((x + pea_ref[...]).astype(BF16), wa_ref[...])
    hb = _dot((x + peb_ref[...]).astype(BF16), wb_ref[...])
    hid = ha + pltpu.roll(hb, nch - 1, axis=0)
    act = (hid * _sigmoid(hid)).astype(BF16)
    for g in range(NSA_KV):
        o = _dot(act[:, g * CMP_HIDDEN:(g + 1) * CMP_HIDDEN], w2_ref[...])
        if normalize:
            ms = jnp.mean(o * o, axis=-1, keepdims=True)
            o = o * lax.rsqrt(ms + EPS) * g_ref[...]
        o_ref[0, :, g * HEAD_DIM:(g + 1) * HEAD_DIM] = o.astype(o_ref.dtype)


def _expand_compress_params(pe, w1):
    half = CMP_LEN // 2
    w = w1.reshape(CMP_LEN, HEAD_DIM, CMP_HIDDEN)
    eye = jnp.eye(NSA_KV, dtype=w1.dtype)

    def expand_w(wh):
        return jnp.einsum("ldj,hg->lhdgj", wh, eye).reshape(half * NSA_KV * HEAD_DIM, NSA_KV * CMP_HIDDEN)

    def expand_pe(ph):
        return jnp.broadcast_to(ph[:, None, :], (half, NSA_KV, HEAD_DIM)).reshape(1, half * NSA_KV * HEAD_DIM)

    return (expand_pe(pe[:half]), expand_pe(pe[half:]),
            expand_w(w[:half]).astype(BF16), expand_w(w[half:]).astype(BF16))


def _compress(raw, pe, w1, w2, gain, batch, seq):
    nch = seq // CMP_STRIDE
    width = CMP_STRIDE * NSA_KV * HEAD_DIM
    x = raw.reshape(batch * nch, width)
    pea, peb, wa, wb = _expand_compress_params(pe, w1)
    normalize = gain is not None
    g = (gain if normalize else jnp.ones((HEAD_DIM,), F32)).reshape(1, HEAD_DIM)
    const = lambda b: (0, 0)
    return pl.pallas_call(
        functools.partial(_compress_kernel, normalize=normalize),
        grid=(batch,),
        in_specs=[pl.BlockSpec((nch, width), lambda b: (b, 0)),
                  pl.BlockSpec((1, width), const), pl.BlockSpec((1, width), const),
                  pl.BlockSpec(wa.shape, const), pl.BlockSpec(wb.shape, const),
                  pl.BlockSpec((CMP_HIDDEN, HEAD_DIM), const), pl.BlockSpec((1, HEAD_DIM), const)],
        out_specs=pl.BlockSpec((1, nch, NSA_KV * HEAD_DIM), lambda b: (b, 0, 0)),
        out_shape=jax.ShapeDtypeStruct((batch, nch, NSA_KV * HEAD_DIM), BF16),
        compiler_params=_params(("parallel",)),
        name="nsa_compress",
    )(x, pea, peb, wa, wb, w2.astype(BF16), g)


def _cmp_to_sel_matrix(nch, n_sel_pad):
    n_cmp = nch - 1
    c0 = np.arange(nch)[:, None] * CMP_STRIDE
    s0 = np.arange(n_sel_pad)[None, :] * SEL_BLOCK
    ov = np.clip(np.minimum(c0 + CMP_LEN, s0 + SEL_BLOCK) - np.maximum(c0, s0), 0, None) / CMP_LEN
    ov[n_cmp:] = 0.0
    return jnp.asarray(ov.T, dtype=BF16)


def _nsa_cmp_kernel(q_ref, kc_ref, vc_ref, mt_ref, eye_ref, tile_of_ref, ocmp_ref, selb_ref, flags_ref,
                    *, tq, nch, n_sel):
    g = pl.program_id(1)
    i = pl.program_id(2)
    t = i * tq + lax.broadcasted_iota(jnp.int32, (tq, 1), 0)
    live = (t >= CMP_LEN - 1).astype(F32)
    gsc = _group_slope_scale(g)

    def attend(width):
        c_end = lax.broadcasted_iota(jnp.int32, (1, width), 1) * CMP_STRIDE + (CMP_LEN - 1)
        vis = c_end <= t
        distc = t.astype(F32) - (c_end.astype(F32) - (CMP_LEN - 1) / 2.0)
        kc = kc_ref[0, :width, :]
        vc = vc_ref[0, :width, :]
        psum = jnp.zeros((tq, width), F32)
        for j in range(HEADS_PER_GROUP):
            slope = (2.0 ** -(j + 1)) * gsc
            sl = slice(j * HEAD_DIM, (j + 1) * HEAD_DIM)
            s = _dot_nt(q_ref[:, sl], kc) - slope * distc
            s = jnp.where(vis, s, NEG)
            e = jnp.exp(s - jnp.max(s, axis=-1, keepdims=True))
            p = e * (live / jnp.sum(e, axis=-1, keepdims=True))
            psum = psum + p
            ocmp_ref[:, sl] = _dot(p.astype(BF16), vc)
        return _dot_nt(mt_ref[:, :width], psum.astype(BF16))

    widths = sorted({min(nch, LANES * (k + 1)) for k in range(pl.cdiv(nch, LANES))})
    visible = ((i + 1) * tq - CMP_LEN) // CMP_STRIDE + 1
    imp = lax.switch(jnp.minimum((visible - 1) // LANES, len(widths) - 1),
                     [functools.partial(attend, w) for w in widths])
    blk = lax.broadcasted_iota(jnp.int32, (LANES, 1), 0)
    blk_t = _div_pow2(i * tq + lax.broadcasted_iota(jnp.int32, (1, tq), 1), SEL_BLOCK)
    valid = blk <= blk_t
    forced = (blk == 0) | (blk == blk_t) | (blk == blk_t - 1)
    vals = jnp.where(valid & jnp.logical_not(forced) & (blk < n_sel), imp, -1.0)
    sel = valid & forced
    for _ in range(SEL_TOPN - 3):
        mx = jnp.max(vals, axis=0, keepdims=True)
        first = jnp.min(jnp.where(vals == mx, blk, LANES), axis=0, keepdims=True)
        pick = blk == first
        sel = sel | pick
        vals = jnp.where(pick, -3.0, vals)
    sel = sel & valid
    masked = jnp.where(sel, 0.0, NEG).astype(BF16)
    selb_ref[...] = _dot_nt(eye_ref[...], masked).astype(selb_ref.dtype)

    t_lane = i * tq + lax.broadcasted_iota(jnp.int32, (1, tq), 1)
    early = blk < (SEGMENT // SEL_BLOCK) * _div_pow2(t_lane, SEGMENT)
    per_query = _dot(tile_of_ref[...], jnp.where(sel & early, 1.0, 0.0).astype(BF16))
    out_lane = lax.broadcasted_iota(jnp.int32, flags_ref.shape[1:], 1)
    flags = jnp.zeros(flags_ref.shape[1:], jnp.int32)
    for h in range(tq // SEGMENT):
        hit = jnp.max(per_query[:, h * SEGMENT:(h + 1) * SEGMENT], axis=1, keepdims=True) > 0.0
        flags = jnp.where((out_lane == h) & hit, 1, flags)
    flags_ref[0] = flags


def _nsa_cmp(qkn, kc, vc, batch, seq, tq, tk):
    t = batch * seq
    nch = seq // CMP_STRIDE
    n_sel = seq // SEL_BLOCK
    n_all = seq // tk
    assert n_sel <= LANES and n_all <= TILE_ROWS and tq % SEGMENT == 0
    nq = seq // tq
    gw = HEADS_PER_GROUP * HEAD_DIM
    mt = _cmp_to_sel_matrix(nch, LANES)
    eye = jnp.asarray(np.eye(tq, dtype=np.float32), dtype=BF16)
    tile_of = jnp.asarray(np.arange(LANES)[None, :] // (tk // SEL_BLOCK) == np.arange(TILE_ROWS)[:, None], dtype=BF16)
    row = lambda b, g, i: (b * nq + i, g)
    ocmp, selb, flags = pl.pallas_call(
        functools.partial(_nsa_cmp_kernel, tq=tq, nch=nch, n_sel=n_sel),
        grid=(batch, NSA_KV, nq),
        in_specs=[pl.BlockSpec((tq, gw), row),
                  pl.BlockSpec((1, nch, HEAD_DIM), lambda b, g, i: (b, 0, g)),
                  pl.BlockSpec((1, nch, HEAD_DIM), lambda b, g, i: (b, 0, g)),
                  pl.BlockSpec((LANES, nch), lambda b, g, i: (0, 0)),
                  pl.BlockSpec((tq, tq), lambda b, g, i: (0, 0)),
                  pl.BlockSpec((TILE_ROWS, LANES), lambda b, g, i: (0, 0))],
        out_specs=[pl.BlockSpec((tq, gw), row), pl.BlockSpec((tq, LANES), row),
                   pl.BlockSpec((1, TILE_ROWS, LANES), lambda b, g, i: ((b * NSA_KV + g) * nq + i, 0, 0))],
        out_shape=[jax.ShapeDtypeStruct((t, NSA_HEADS * HEAD_DIM), F32),
                   jax.ShapeDtypeStruct((t, NSA_KV * LANES), BF16),
                   jax.ShapeDtypeStruct((batch * NSA_KV * nq, TILE_ROWS, LANES), jnp.int32)],
        compiler_params=_params(("parallel", "parallel", "parallel")),
        name="nsa_cmp_select",
    )(qkn, kc, vc, mt, eye, tile_of)
    segs = tq // SEGMENT
    tile_flags = flags[:, :n_all, :segs].transpose(0, 2, 1).reshape(batch, NSA_KV, nq * segs, n_all)
    return ocmp, selb, tile_flags


def _key_aux_blocks(seq):
    j = np.arange(seq)
    aux = (j[:, None] // SEL_BLOCK == np.arange(LANES)[None, :]).astype(np.float32)
    aux[:, 0] = j % SEL_BLOCK
    return jnp.asarray(aux, dtype=BF16)


def _key_aux_position(seq):
    assert seq <= BF16_EXACT_INT * BF16_EXACT_INT
    j = np.arange(seq)
    aux = np.zeros((seq, LANES), np.float32)
    aux[:, 0] = j % BF16_EXACT_INT
    aux[:, 1] = j - j % BF16_EXACT_INT
    return jnp.asarray(aux, dtype=BF16)


def _query_aug(q_ref, aux_cols):
    parts = [jnp.concatenate([q_ref[:, j * HEAD_DIM:(j + 1) * HEAD_DIM], aux_cols[j].astype(BF16)], axis=1)
             for j in range(HEADS_PER_GROUP)]
    return jnp.concatenate(parts, axis=0)


def _flash_step(s, vt, carry):
    m, l, acc = carry
    m_new = jnp.maximum(m, jnp.max(s, axis=0, keepdims=True))
    alpha = jnp.exp(m - m_new)
    p = jnp.exp(s - m_new).astype(BF16)
    pv = _dot(jnp.concatenate([vt, jnp.ones((BF16_SUBLANES, vt.shape[1]), BF16)], axis=0), p)
    return m_new, alpha * l + pv[HEAD_DIM:HEAD_DIM + 1], alpha * acc + pv[:HEAD_DIM]


def _flash_init(cols):
    return (jnp.full((1, cols), NEG, F32), jnp.zeros((1, cols), F32), jnp.zeros((HEAD_DIM, cols), F32))


def _transposed_values(v, batch, seq, groups):
    return v.reshape(batch, seq // SEGMENT, SEGMENT, groups, HEAD_DIM).transpose(0, 3, 1, 4, 2)


def _pipelined_tiles(n, n_all, scores_fn, consume_fn, s_a, s_b, carry):
    s_a[...] = scores_fn(0)

    def pairs(k, count, carry):
        for c in range(count):
            s_b[...] = scores_fn(k + 2 * c + 1)
            carry = consume_fn(k + 2 * c, s_a[...], carry)
            s_a[...] = scores_fn(jnp.minimum(k + 2 * c + 2, n_all - 1))
            carry = consume_fn(k + 2 * c + 1, s_b[...], carry)
        return carry

    quads = lax.shift_right_logical(n, 2)
    carry = lax.fori_loop(0, quads, lambda u, c: pairs(4 * u, 2, c), carry)
    done = 4 * quads
    carry = lax.cond((n & 2) == 2, lambda c: pairs(done, 1, c), lambda c: c, carry)
    return lax.cond((n & 1) == 1, lambda c: consume_fn(n - 1, s_a[...], c), lambda c: c, carry)


def _nsa_main_kernel(tiles_ref, count_ref, q_ref, ks_ref, vst_ref, kw_ref, vwt_ref, kaux_ref, kpos_ref, selb_ref,
                     ocmp_ref, gate_ref, z_ref, o_ref, s_a, s_b, *, tq, tk, seq):
    b = pl.program_id(0)
    g = pl.program_id(1)
    i = pl.program_id(2)
    n_all = seq // tk
    per_tile = tk // SEGMENT
    t0 = pl.multiple_of(i * tq, tq)
    cols = HEADS_PER_GROUP * tq
    gsc = _group_slope_scale(g)
    t_cols = t0 + (lax.broadcasted_iota(jnp.int32, (1, cols), 1) & (tq - 1))
    slopes = [(2.0 ** -(j + 1)) * gsc for j in range(HEADS_PER_GROUP)]

    def values(ref, first, count):
        return jnp.concatenate([ref[0, 0, first + c] for c in range(count)], axis=1)

    lane = lax.broadcasted_iota(jnp.int32, (tq, LANES), 1)
    chosen = selb_ref[...].astype(F32) > -1.0
    own_first = _div_pow2(t0, SEL_BLOCK)
    block_start = (lane * SEL_BLOCK).astype(F32)

    def masked_queries(blocks):
        return _query_aug(q_ref, [jnp.where(lane == 0, sl, jnp.where(chosen & blocks, sl * block_start, NEG))
                                  for sl in slopes])

    q_aug = masked_queries(lane < own_first)

    slot = (b * NSA_KV + g) * (seq // tq) + i
    base = slot * n_all

    def scores(u):
        j0 = pl.multiple_of(tiles_ref[base + u] * tk, tk)
        return _dot_nt(jnp.concatenate([ks_ref[pl.ds(j0, tk), :], kaux_ref[pl.ds(j0, tk), :]], axis=1), q_aug)

    def consume(u, s, carry):
        return _flash_step(s, values(vst_ref, tiles_ref[base + u] * per_tile, per_tile), carry)

    carry = _pipelined_tiles(count_ref[slot], n_all, scores, consume, s_a, s_b, _flash_init(cols))

    j_own = t0 + lax.broadcasted_iota(jnp.int32, (tq, 1), 0)
    s = _dot_nt(jnp.concatenate([ks_ref[pl.ds(t0, tq), :], kaux_ref[pl.ds(t0, tq), :]], axis=1),
                masked_queries(lane >= own_first))
    s = jnp.where(j_own <= t_cols, s, NEG)
    _, l_s, acc_s = _flash_step(s, values(vst_ref, i, 1), carry)
    o_slc = acc_s / l_s

    q_pos = _query_aug(q_ref, [jnp.where(lane < 2, sl, 0.0) for sl in slopes])
    wk = WINDOW + tq
    start = pl.multiple_of(jnp.maximum(t0 - WINDOW, 0), SEGMENT)
    j_win = start + lax.broadcasted_iota(jnp.int32, (wk, 1), 0)
    s = _dot_nt(jnp.concatenate([kw_ref[pl.ds(start, wk), :], kpos_ref[pl.ds(start, wk), :]], axis=1), q_pos)
    s = jnp.where((j_win <= t_cols) & (j_win > t_cols - WINDOW), s, NEG)
    _, l_w, acc_w = _flash_step(s, values(vwt_ref, _div_pow2(start, SEGMENT), wk // SEGMENT), _flash_init(cols))
    o_win = acc_w / l_w

    for j in range(HEADS_PER_GROUP):
        sl = slice(j * HEAD_DIM, (j + 1) * HEAD_DIM)
        cs = slice(j * tq, (j + 1) * tq)
        gates = [_sigmoid(gate_ref[:, 3 * j + c:3 * j + c + 1]) for c in range(3)]
        o = gates[0] * ocmp_ref[:, sl] + gates[1] * o_slc[:, cs].T + gates[2] * o_win[:, cs].T
        o_ref[:, sl] = (o * z_ref[:, sl]).astype(o_ref.dtype)


def _nsa_main(qkn, vals_t, selb, tile_flags, ocmp, gates, zact, batch, seq, tq, tk):
    t = batch * seq
    nq = seq // tq
    n_all = seq // tk
    gw = HEADS_PER_GROUP * HEAD_DIM
    flags = tile_flags.reshape(batch * NSA_KV * nq, n_all)
    tiles = jnp.argsort(1 - flags, axis=-1, stable=True).astype(jnp.int32).reshape(-1)
    counts = jnp.sum(flags, axis=-1).astype(jnp.int32)
    row = lambda b, g, i, *_: (b * nq + i, g)
    ks_blk = (NSA_HEADS * HEAD_DIM) // HEAD_DIM
    kw_blk = ks_blk + NSA_KV
    grid_spec = pltpu.PrefetchScalarGridSpec(
        num_scalar_prefetch=2,
        grid=(batch, NSA_KV, nq),
        in_specs=[pl.BlockSpec((tq, gw), row),
                  pl.BlockSpec((seq, HEAD_DIM), lambda b, g, i, *_: (b, ks_blk + g)),
                  pl.BlockSpec((1, 1, seq // SEGMENT, HEAD_DIM, SEGMENT), lambda b, g, i, *_: (b, g, 0, 0, 0)),
                  pl.BlockSpec((seq, HEAD_DIM), lambda b, g, i, *_: (b, kw_blk + g)),
                  pl.BlockSpec((1, 1, seq // SEGMENT, HEAD_DIM, SEGMENT),
                               lambda b, g, i, *_: (b, NSA_KV + g, 0, 0, 0)),
                  pl.BlockSpec((seq, LANES), lambda b, g, i, *_: (0, 0)),
                  pl.BlockSpec((seq, LANES), lambda b, g, i, *_: (0, 0)),
                  pl.BlockSpec((tq, LANES), row),
                  pl.BlockSpec((tq, gw), row),
                  pl.BlockSpec((tq, LANES), row),
                  pl.BlockSpec((tq, gw), row)],
        out_specs=pl.BlockSpec((tq, gw), row),
        scratch_shapes=[pltpu.VMEM((tk, HEADS_PER_GROUP * tq), F32)] * 2,
    )
    return pl.pallas_call(
        functools.partial(_nsa_main_kernel, tq=tq, tk=tk, seq=seq),
        grid_spec=grid_spec,
        out_shape=jax.ShapeDtypeStruct((t, NSA_HEADS * HEAD_DIM), BF16),
        compiler_params=_params(("parallel", "parallel", "arbitrary")),
        name="nsa_select_window",
    )(tiles, counts, qkn, qkn, vals_t, qkn, vals_t, _key_aux_blocks(seq), _key_aux_position(seq), selb, ocmp, gates,
      zact)


def _threshold_value(code):
    return pltpu.bitcast(jnp.where(code < 0, code ^ 0x7FFFFFFF, code), F32)


def _dsa_index_kernel(qi_ref, ki_ref, wi_ref, tri_ref, bias_ref, score_scr, *, tq, tk, seq, topk):
    i = pl.program_id(1)
    t0 = i * tq
    n_all = seq // tk
    n_kt = _div_pow2(t0 + tq + tk - 1, tk)
    t = t0 + lax.broadcasted_iota(jnp.int32, (1, tq), 1)
    wrow = wi_ref[...] * (IDX_HEADS ** -0.5 * IDX_DIM ** -0.5)
    hb = 4

    def key_ids(kt):
        return kt * tk + lax.broadcasted_iota(jnp.int32, (tk, 1), 0)

    def score_tile(kt):
        j0 = pl.multiple_of(kt * tk, tk)
        ki = ki_ref[pl.ds(j0, tk), :]
        score = jnp.zeros((tk, tq), F32)
        for h0 in range(0, IDX_HEADS, hb):
            x = _dot_nt(ki, qi_ref[0, h0:h0 + hb].reshape(hb * tq, IDX_DIM))
            for h in range(hb):
                score = score + jnp.maximum(x[:, h * tq:(h + 1) * tq], 0.0) * wrow[h0 + h:h0 + h + 1, :]
        score = jnp.where(key_ids(kt) <= t, score, -jnp.inf)
        score_scr[kt] = score
        return score

    def hits(pred, score, c):
        return c + jnp.sum(jnp.where(pred(score), 1, 0).reshape(tk // 8, 8, tq), axis=0)

    def non_negative(s):
        return s >= 0.0

    n_pairs = lax.shift_right_logical(n_kt + 1, 1)

    def score_step(u, c):
        for kt in (2 * u, 2 * u + 1):
            c = hits(non_negative, score_tile(kt), c)
        return c

    c0 = jnp.sum(lax.fori_loop(0, n_pairs, score_step, jnp.zeros((8, tq), jnp.int32)), axis=0, keepdims=True)

    def count(pred):
        def step(u, c):
            for kt in (2 * u, 2 * u + 1):
                c = hits(pred, score_scr[kt], c)
            return c
        c = lax.fori_loop(0, n_pairs, step, jnp.zeros((8, tq), jnp.int32))
        return jnp.sum(c, axis=0, keepdims=True)

    everything = jnp.int32(seq + tk)
    state = (jnp.where(c0 >= topk, 0, INT_MIN), jnp.where(c0 >= topk, c0, everything))

    def bit_step(b, state):
        v, n_ge = state
        cand = v | (jnp.int32(1) << (30 - b))
        threshold = _threshold_value(cand)
        c = count(lambda s: s >= threshold)
        return jnp.where(c >= topk, cand, v), jnp.where(c >= topk, c, n_ge)

    def settled(state):
        done = (state[1] == topk) | (t < topk)
        return jnp.min(jnp.where(done, 1, 0)) > 0

    state = lax.fori_loop(0, RADIX_CHECKS[0], bit_step, state)
    for lo, hi in zip(RADIX_CHECKS, RADIX_CHECKS[1:] + (31,)):
        state = lax.cond(settled(state), lambda s: s, lambda s, lo=lo, hi=hi: lax.fori_loop(lo, hi, bit_step, s), state)
    v, n_ge = state
    threshold = _threshold_value(v)
    take_all = v == INT_MIN

    def emit(kt, keep):
        bias_ref[0, 0, kt] = jnp.where(keep & (key_ids(kt) <= t), 0.0, NEG).astype(bias_ref.dtype)

    def fill_step(kt, _):
        bias_ref[0, 0, kt] = jnp.full((tk, tq), NEG, bias_ref.dtype)
        return 0

    def emit_no_ties(_):
        def step(u, c):
            for kt in (2 * u, 2 * u + 1):
                emit(kt, (score_scr[kt] >= threshold) | take_all)
            return c
        return lax.fori_loop(0, n_pairs, step, 0)

    def emit_with_ties(_):
        need = (topk - count(lambda s: s > threshold)).astype(F32)

        def step(kt, carry):
            score = score_scr[kt]
            eq = score == threshold
            eqf = jnp.where(eq, 1.0, 0.0)
            before = carry + _dot(tri_ref[...], eqf.astype(BF16)) - eqf
            emit(kt, (score > threshold) | (eq & (before < need)) | take_all)
            return carry + jnp.sum(eqf, axis=0, keepdims=True)

        lax.fori_loop(0, n_kt, step, jnp.zeros((1, tq), F32))
        return lax.fori_loop(n_kt, 2 * n_pairs, fill_step, 0)

    over = (n_ge > topk) & jnp.logical_not(take_all)
    lax.cond(jnp.max(jnp.where(over, 1, 0)) > 0, emit_with_ties, emit_no_ties, 0)
    lax.fori_loop(2 * n_pairs, n_all, fill_step, 0)


def _dsa_index(qi_heads, ki, wi_t, batch, seq, tq, tk):
    nq = seq // tq
    n_all = seq // tk
    topk = min(DSA_TOPK_MAX, seq // 4)
    tri = jnp.asarray(np.tril(np.ones((tk, tk), np.float32)), dtype=BF16)
    return pl.pallas_call(
        functools.partial(_dsa_index_kernel, tq=tq, tk=tk, seq=seq, topk=topk),
        grid=(batch, nq),
        in_specs=[pl.BlockSpec((1, IDX_HEADS, tq, IDX_DIM), lambda b, i: (b, 0, i, 0)),
                  pl.BlockSpec((seq, IDX_DIM), lambda b, i: (b, 0)),
                  pl.BlockSpec((IDX_HEADS, tq), lambda b, i: (0, b * nq + i)),
                  pl.BlockSpec((tk, tk), lambda b, i: (0, 0))],
        out_specs=pl.BlockSpec((1, 1, n_all, tk, tq), lambda b, i: (b, i, 0, 0, 0)),
        out_shape=jax.ShapeDtypeStruct((batch, nq, n_all, tk, tq), BF16),
        scratch_shapes=[pltpu.VMEM((n_all, tk, tq), F32)],
        compiler_params=_params(("parallel", "arbitrary")),
        name="dsa_index_topk",
    )(qi_heads, ki, wi_t, tri)


def _dsa_attn_kernel(q_ref, k_ref, vt_ref, kaux_ref, bias_ref, z_ref, o_ref, s_a, s_b, *, tq, tk, seq):
    g = pl.program_id(1)
    i = pl.program_id(2)
    t0 = i * tq
    cols = HEADS_PER_GROUP * tq
    gsc = _group_slope_scale(g)
    slopes = [(2.0 ** -(j + 1)) * gsc for j in range(HEADS_PER_GROUP)]
    lane = lax.broadcasted_iota(jnp.int32, (tq, LANES), 1)
    q_aug = _query_aug(q_ref, [jnp.where(lane < 2, sl, 0.0) for sl in slopes])
    per_tile = tk // SEGMENT

    def scores(kt):
        j0 = pl.multiple_of(kt * tk, tk)
        return _dot_nt(jnp.concatenate([k_ref[pl.ds(j0, tk), :], kaux_ref[pl.ds(j0, tk), :]], axis=1), q_aug)

    def consume(kt, s, carry):
        bias = [bias_ref[0, part, kt].astype(F32) for part in range(bias_ref.shape[1])]
        s = s + jnp.concatenate(bias * HEADS_PER_GROUP, axis=1)
        vt = jnp.concatenate([vt_ref[0, 0, kt * per_tile + c] for c in range(per_tile)], axis=1)
        return _flash_step(s, vt, carry)

    n_kt = _div_pow2(t0 + tq + tk - 1, tk)
    _, l, acc = _pipelined_tiles(n_kt, seq // tk, scores, consume, s_a, s_b, _flash_init(cols))
    o = acc / l
    for j in range(HEADS_PER_GROUP):
        sl = slice(j * HEAD_DIM, (j + 1) * HEAD_DIM)
        o_ref[:, sl] = (o[:, j * tq:(j + 1) * tq].T * z_ref[:, sl]).astype(o_ref.dtype)


def _dsa_attn(qkn, vals_t, bias, zact, batch, seq, tq, tk):
    t = batch * seq
    nq = seq // tq
    gw = HEADS_PER_GROUP * HEAD_DIM
    row = lambda b, g, i: (b * nq + i, g)
    q_blk = (NSA_HEADS + 2 * NSA_KV) * HEAD_DIM // gw
    k_blk = (NSA_HEADS + 2 * NSA_KV + DSA_HEADS)
    v_grp = 2 * NSA_KV
    return pl.pallas_call(
        functools.partial(_dsa_attn_kernel, tq=tq, tk=tk, seq=seq),
        grid=(batch, DSA_KV, nq),
        in_specs=[pl.BlockSpec((tq, gw), lambda b, g, i: (b * nq + i, q_blk + g)),
                  pl.BlockSpec((seq, HEAD_DIM), lambda b, g, i: (b, k_blk + g)),
                  pl.BlockSpec((1, 1, seq // SEGMENT, HEAD_DIM, SEGMENT), lambda b, g, i: (b, v_grp + g, 0, 0, 0)),
                  pl.BlockSpec((seq, LANES), lambda b, g, i: (0, 0)),
                  pl.BlockSpec((1, tq // bias.shape[4]) + bias.shape[2:], lambda b, g, i: (b, i, 0, 0, 0)),
                  pl.BlockSpec((tq, gw), lambda b, g, i: (b * nq + i, NSA_KV + g))],
        out_specs=pl.BlockSpec((tq, gw), row),
        out_shape=jax.ShapeDtypeStruct((t, DSA_HEADS * HEAD_DIM), BF16),
        scratch_shapes=[pltpu.VMEM((tk, HEADS_PER_GROUP * tq), F32)] * 2,
        compiler_params=_params(("parallel", "parallel", "arbitrary")),
        name="dsa_attention",
    )(qkn, qkn, vals_t, _key_aux_position(seq), bias, zact)


def _merge_kernel(on_ref, od_ref, sn_ref, sd_ref, x_ref, wun_ref, wud_ref, wout_ref, o_ref):
    y = sn_ref[...] * _dot(on_ref[...], wun_ref[...]) + sd_ref[...] * _dot(od_ref[...], wud_ref[...])
    o_ref[...] = x_ref[...] + _dot(y.astype(BF16), wout_ref[...])


def _merge(o_nsa, o_dsa, gate_act, x, w_up_nsa, w_up_dsa, w_out, tm):
    t, d = x.shape
    const = lambda i: (0, 0)
    return pl.pallas_call(
        _merge_kernel,
        grid=(t // tm,),
        in_specs=[pl.BlockSpec((tm, o_nsa.shape[1]), lambda i: (i, 0)),
                  pl.BlockSpec((tm, o_dsa.shape[1]), lambda i: (i, 0)),
                  pl.BlockSpec((tm, d), lambda i: (i, 0)),
                  pl.BlockSpec((tm, d), lambda i: (i, 1)),
                  pl.BlockSpec((tm, d), lambda i: (i, 0)),
                  pl.BlockSpec(w_up_nsa.shape, const), pl.BlockSpec(w_up_dsa.shape, const),
                  pl.BlockSpec(w_out.shape, const)],
        out_specs=pl.BlockSpec((tm, d), lambda i: (i, 0)),
        out_shape=jax.ShapeDtypeStruct((t, d), F32),
        compiler_params=_params(("parallel",)),
        name="merge_out_proj",
    )(o_nsa, o_dsa, gate_act, gate_act, x, w_up_nsa, w_up_dsa, w_out)


def _ple_kernel(x_ref, p_ref, g_ref, wg_ref, wp_ref, o_ref):
    x = x_ref[...]
    ms = jnp.mean(x * x, axis=-1, keepdims=True)
    r = (x * lax.rsqrt(ms + EPS) * g_ref[...]).astype(BF16)
    gate = _sigmoid(_dot(r, wg_ref[...]))
    o_ref[...] = x + _dot(p_ref[...].astype(BF16), wp_ref[...]) * gate


def _ple(x, p, g, w_gate, w_proj, tm):
    t, d = x.shape
    const = lambda i: (0, 0)
    return pl.pallas_call(
        _ple_kernel,
        grid=(t // tm,),
        in_specs=[pl.BlockSpec((tm, d), lambda i: (i, 0)),
                  pl.BlockSpec((tm, p.shape[1]), lambda i: (i, 0)),
                  pl.BlockSpec((1, d), const),
                  pl.BlockSpec(w_gate.shape, const), pl.BlockSpec(w_proj.shape, const)],
        out_specs=pl.BlockSpec((tm, d), lambda i: (i, 0)),
        out_shape=jax.ShapeDtypeStruct((t, d), F32),
        compiler_params=_params(("parallel",)),
        name="ple_gate",
    )(x, p, g.reshape(1, d), w_gate, w_proj)


def _split_in_proj(w):
    widths = [NSA_HEADS * HEAD_DIM] + [NSA_KV * HEAD_DIM] * 6 + [NSA_HEADS * 3, NSA_HEADS * HEAD_DIM,
              DSA_HEADS * HEAD_DIM, DSA_KV * HEAD_DIM, DSA_KV * HEAD_DIM, IDX_HEADS * IDX_DIM, IDX_DIM,
              IDX_HEADS, DSA_HEADS * HEAD_DIM, D_MODEL, D_MODEL]
    offs = np.concatenate([[0], np.cumsum(widths)])
    assert offs[-1] == w.shape[1]
    (nq, nkc, nvc, nks, nvs, nkw, nvw, ng, nz, dq, dk, dv, iq, ik, iw, dz, mgn, mgd) = [
        w[:, offs[k]:offs[k + 1]] for k in range(len(widths))]
    pad = jnp.zeros((w.shape[0], LANES - IDX_DIM - IDX_HEADS - NSA_HEADS * 3), w.dtype)
    return dict(
        normed=jnp.concatenate([nq, nks, nkw, dq, dk], axis=1).astype(BF16),
        vals=jnp.concatenate([nvs, nvw, dv], axis=1).astype(BF16),
        iq=iq.astype(BF16),
        raw=jnp.concatenate([nkc, nvc, ik, iw, ng, pad], axis=1).astype(BF16),
        zact=jnp.concatenate([nz, dz], axis=1).astype(BF16),
        merge=jnp.concatenate([mgn, mgd], axis=1).astype(BF16),
    )


def _layer(x, p, norm_g, w_in, nsa_q_g, nsa_kc_g, nsa_ks_g, nsa_kw_g, cmp_pe_k, cmp_w1_k, cmp_w2_k,
           cmp_pe_v, cmp_w1_v, cmp_w2_v, dsa_q_g, dsa_k_g, w_up_nsa, w_up_dsa, w_out, ple_norm_g,
           w_ple_gate, w_ple_proj):
    batch, seq, d = x.shape
    t = batch * seq
    assert seq % SEGMENT == 0 and seq >= WINDOW + SEGMENT
    x2 = x.reshape(t, d)
    tiles = _Tiles.for_shape(batch, seq)
    tm = tiles.proj_rows

    w = _split_in_proj(w_in)
    h = _rmsnorm(x2, norm_g, tiles.norm_rows)

    gains = jnp.concatenate([jnp.tile(nsa_q_g * ATTN_SCALE, NSA_HEADS), jnp.tile(nsa_ks_g, NSA_KV),
                             jnp.tile(nsa_kw_g, NSA_KV), jnp.tile(dsa_q_g * ATTN_SCALE, DSA_HEADS),
                             jnp.tile(dsa_k_g, DSA_KV)])
    qkn = _proj(h, w["normed"], tm, w["normed"].shape[1] // 2, BF16, gain=gains)
    vals = _proj(h, w["vals"], tm, w["vals"].shape[1], BF16)
    vals_t = _transposed_values(vals, batch, seq, 2 * NSA_KV + DSA_KV)
    qi_heads = _proj_heads(h, w["iq"], min(tm, seq), batch, seq)
    kc_raw, vc_raw, misc = _proj_split(h, w["raw"], tm, (NSA_KV * HEAD_DIM, NSA_KV * HEAD_DIM, LANES))
    zact = _proj(h, w["zact"], tm, tiles.proj_cols, F32, act="silu")
    gate_act = _proj(h, w["merge"], tm, tiles.proj_cols, F32, act="sigmoid")

    ki = misc[:, :IDX_DIM].astype(BF16)
    wi = misc[:, IDX_DIM:IDX_DIM + IDX_HEADS]
    ng = misc[:, IDX_DIM + IDX_HEADS:IDX_DIM + IDX_HEADS + NSA_HEADS * 3]
    per_group = HEADS_PER_GROUP * 3
    gates = jnp.pad(ng.reshape(t, NSA_KV, per_group), ((0, 0), (0, 0), (0, LANES - per_group))).reshape(t, NSA_KV * LANES)

    kc = _compress(kc_raw, cmp_pe_k, cmp_w1_k, cmp_w2_k, nsa_kc_g, batch, seq)
    vc = _compress(vc_raw, cmp_pe_v, cmp_w1_v, cmp_w2_v, None, batch, seq)

    tk = tiles.keys
    ocmp, selb, tile_flags = _nsa_cmp(qkn, kc, vc, batch, seq, tiles.cmp_queries, tk)
    o_nsa = _nsa_main(qkn, vals_t, selb, tile_flags, ocmp, gates, zact, batch, seq, SEGMENT, tk)

    bias = _dsa_index(qi_heads, ki, wi.T, batch, seq, INDEX_QUERIES, tk)
    o_dsa = _dsa_attn(qkn, vals_t, bias, zact, batch, seq, SEGMENT, tk)

    x1 = _merge(o_nsa, o_dsa, gate_act, x2, w_up_nsa.astype(BF16), w_up_dsa.astype(BF16), w_out.astype(BF16),
                tiles.out_rows)
    x3 = _ple(x1, p.reshape(t, PLE_DIM), ple_norm_g, w_ple_gate.astype(BF16), w_ple_proj.astype(BF16),
              tiles.out_rows)
    return x3.reshape(batch, seq, d)


def kernel(x, p, norm_g, w_in, nsa_q_g, nsa_kc_g, nsa_ks_g, nsa_kw_g, cmp_pe_k, cmp_w1_k, cmp_w2_k, cmp_pe_v, cmp_w1_v, cmp_w2_v, dsa_q_g, dsa_k_g, w_up_nsa, w_up_dsa, w_out, ple_norm_g, w_ple_gate, w_ple_proj):
    depth = w_in.shape[0]
    for i in range(depth):
        x = _layer(x, p[i], norm_g[i], w_in[i], nsa_q_g[i], nsa_kc_g[i], nsa_ks_g[i], nsa_kw_g[i],
                   cmp_pe_k[i], cmp_w1_k[i], cmp_w2_k[i], cmp_pe_v[i], cmp_w1_v[i], cmp_w2_v[i],
                   dsa_q_g[i], dsa_k_g[i], w_up_nsa[i], w_up_dsa[i], w_out[i], ple_norm_g[i],
                   w_ple_gate[i], w_ple_proj[i])
    return x
```

```python
import functools
from typing import NamedTuple

import numpy as np
import jax
import jax.numpy as jnp
from jax import lax
from jax.experimental import pallas as pl
from jax.experimental.pallas import tpu as pltpu

D_MODEL = 2048
HEAD_DIM = 128
NSA_HEADS = 8
NSA_KV = 2
DSA_HEADS = 8
DSA_KV = 2
HEADS_PER_GROUP = 4
CMP_LEN = 32
CMP_STRIDE = 16
CMP_HIDDEN = 256
SEL_BLOCK = 64
SEL_TOPN = 16
WINDOW = 512
IDX_HEADS = 16
IDX_DIM = 64
DSA_TOPK_MAX = 256
PLE_DIM = 256
EPS = 1e-6
NEG = -1e30
ATTN_SCALE = HEAD_DIM ** -0.5
LANES = 128
INT_MIN = -(2 ** 31)
RADIX_CHECKS = (24, 27)
BF16_EXACT_INT = 256
BF16_SUBLANES = 16

VMEM_LIMIT_BYTES = 56 * 1024 * 1024

F32 = jnp.float32
BF16 = jnp.bfloat16

SEGMENT = 256
INDEX_QUERIES = 128
TILE_ROWS = 16


class _Tiles(NamedTuple):
    norm_rows: int
    proj_rows: int
    proj_cols: int
    out_rows: int
    cmp_queries: int
    keys: int

    @classmethod
    def for_shape(cls, batch, seq):
        t = batch * seq
        return cls(norm_rows=min(512, t), proj_rows=min(1024, t), proj_cols=1024, out_rows=min(256, t),
                   cmp_queries=min(512, seq), keys=min(512, seq))


def _params(semantics):
    return pltpu.CompilerParams(dimension_semantics=semantics, vmem_limit_bytes=VMEM_LIMIT_BYTES)


def _sigmoid(x):
    return 1.0 / (1.0 + jnp.exp(-x))


def _dot(a, b):
    return jnp.dot(a, b, preferred_element_type=F32)


def _dot_nt(a, b):
    return lax.dot_general(a, b, (((1,), (1,)), ((), ())), preferred_element_type=F32)


def _log2(n):
    assert n > 0 and n & (n - 1) == 0, n
    return n.bit_length() - 1


def _div_pow2(x, n):
    return lax.shift_right_logical(x, jnp.int32(_log2(n)))


def _group_slope_scale(g):
    return jnp.where(g == 0, 1.0, 2.0 ** -HEADS_PER_GROUP).astype(F32)


def _rmsnorm_kernel(x_ref, g_ref, o_ref):
    x = x_ref[...]
    ms = jnp.mean(x * x, axis=-1, keepdims=True)
    o_ref[...] = (x * lax.rsqrt(ms + EPS) * g_ref[...]).astype(o_ref.dtype)


def _rmsnorm(x, g, tm):
    t, d = x.shape
    return pl.pallas_call(
        _rmsnorm_kernel,
        grid=(t // tm,),
        in_specs=[pl.BlockSpec((tm, d), lambda i: (i, 0)), pl.BlockSpec((1, d), lambda i: (0, 0))],
        out_specs=pl.BlockSpec((tm, d), lambda i: (i, 0)),
        out_shape=jax.ShapeDtypeStruct((t, d), BF16),
        compiler_params=_params(("parallel",)),
        name="in_rmsnorm",
    )(x, g.reshape(1, d))


def _proj_headnorm_kernel(h_ref, w_ref, g_ref, o_ref):
    y = _dot(h_ref[...], w_ref[...])
    for hd in range(y.shape[1] // HEAD_DIM):
        sl = slice(hd * HEAD_DIM, (hd + 1) * HEAD_DIM)
        yh = y[:, sl]
        ms = jnp.mean(yh * yh, axis=-1, keepdims=True)
        o_ref[:, sl] = (yh * lax.rsqrt(ms + EPS) * g_ref[:, sl]).astype(o_ref.dtype)


def _proj_act_kernel(h_ref, w_ref, o_ref, *, act):
    y = _dot(h_ref[...], w_ref[...])
    if act == "silu":
        y = y * _sigmoid(y)
    elif act == "sigmoid":
        y = _sigmoid(y)
    o_ref[...] = y.astype(o_ref.dtype)


def _proj_split_kernel(h_ref, w_ref, *o_refs):
    y = _dot(h_ref[...], w_ref[...])
    off = 0
    for o_ref in o_refs:
        n = o_ref.shape[1]
        o_ref[...] = y[:, off:off + n].astype(o_ref.dtype)
        off += n


def _proj(h, w, tm, tn, out_dtype, act=None, gain=None):
    t, d = h.shape
    n = w.shape[1]
    in_specs = [pl.BlockSpec((tm, d), lambda i, j: (i, 0)), pl.BlockSpec((d, tn), lambda i, j: (0, j))]
    args = [h, w]
    if gain is not None:
        body = _proj_headnorm_kernel
        in_specs.append(pl.BlockSpec((1, tn), lambda i, j: (0, j)))
        args.append(gain.reshape(1, n))
    else:
        body = functools.partial(_proj_act_kernel, act=act)
    return pl.pallas_call(
        body,
        grid=(t // tm, n // tn),
        in_specs=in_specs,
        out_specs=pl.BlockSpec((tm, tn), lambda i, j: (i, j)),
        out_shape=jax.ShapeDtypeStruct((t, n), out_dtype),
        compiler_params=_params(("parallel", "arbitrary")),
        name="in_proj_" + ("headnorm" if gain is not None else str(act)),
    )(*args)


def _proj_heads_kernel(h_ref, w_ref, o_ref):
    y = _dot(h_ref[...], w_ref[...])
    for hd in range(o_ref.shape[1]):
        o_ref[0, hd] = y[:, hd * IDX_DIM:(hd + 1) * IDX_DIM].astype(o_ref.dtype)


def _proj_heads(h, w, tm, batch, seq):
    t, d = h.shape
    per_batch = seq // tm
    return pl.pallas_call(
        _proj_heads_kernel,
        grid=(t // tm,),
        in_specs=[pl.BlockSpec((tm, d), lambda i: (i, 0)), pl.BlockSpec(w.shape, lambda i: (0, 0))],
        out_specs=pl.BlockSpec((1, IDX_HEADS, tm, IDX_DIM), lambda i: (i // per_batch, 0, i % per_batch, 0)),
        out_shape=jax.ShapeDtypeStruct((batch, IDX_HEADS, seq, IDX_DIM), BF16),
        compiler_params=_params(("parallel",)),
        name="in_proj_heads",
    )(h, w)


def _proj_split(h, w, tm, widths):
    t, d = h.shape
    n = w.shape[1]
    return pl.pallas_call(
        _proj_split_kernel,
        grid=(t // tm,),
        in_specs=[pl.BlockSpec((tm, d), lambda i: (i, 0)), pl.BlockSpec((d, n), lambda i: (0, 0))],
        out_specs=[pl.BlockSpec((tm, wd), lambda i: (i, 0)) for wd in widths],
        out_shape=[jax.ShapeDtypeStruct((t, wd), F32) for wd in widths],
        compiler_params=_params(("parallel",)),
        name="in_proj_split",
    )(h, w)


def _compress_kernel(x_ref, pea_ref, peb_ref, wa_ref, wb_ref, w2_ref, g_ref, o_ref, *, normalize):
    x = x_ref[...]
    nch = x.shape[0]
    ha = _dot((x + pea_ref[...]).astype(BF16), wa_ref[...])
    hb = _dot((x + peb_ref[...]).astype(BF16), wb_ref[...])
    hid = ha + pltpu.roll(hb, nch - 1, axis=0)
    act = (hid * _sigmoid(hid)).astype(BF16)
    for g in range(NSA_KV):
        o = _dot(act[:, g * CMP_HIDDEN:(g + 1) * CMP_HIDDEN], w2_ref[...])
        if normalize:
            ms = jnp.mean(o * o, axis=-1, keepdims=True)
            o = o * lax.rsqrt(ms + EPS) * g_ref[...]
        o_ref[0, :, g * HEAD_DIM:(g + 1) * HEAD_DIM] = o.astype(o_ref.dtype)


def _expand_compress_params(pe, w1):
    half = CMP_LEN // 2
    w = w1.reshape(CMP_LEN, HEAD_DIM, CMP_HIDDEN)
    eye = jnp.eye(NSA_KV, dtype=w1.dtype)

    def expand_w(wh):
        return jnp.einsum("ldj,hg->lhdgj", wh, eye).reshape(half * NSA_KV * HEAD_DIM, NSA_KV * CMP_HIDDEN)

    def expand_pe(ph):
        return jnp.broadcast_to(ph[:, None, :], (half, NSA_KV, HEAD_DIM)).reshape(1, half * NSA_KV * HEAD_DIM)

    return (expand_pe(pe[:half]), expand_pe(pe[half:]),
            expand_w(w[:half]).astype(BF16), expand_w(w[half:]).astype(BF16))


def _compress(raw, pe, w1, w2, gain, batch, seq):
    nch = seq // CMP_STRIDE
    width = CMP_STRIDE * NSA_KV * HEAD_DIM
    x = raw.reshape(batch * nch, width)
    pea, peb, wa, wb = _expand_compress_params(pe, w1)
    normalize = gain is not None
    g = (gain if normalize else jnp.ones((HEAD_DIM,), F32)).reshape(1, HEAD_DIM)
    const = lambda b: (0, 0)
    return pl.pallas_call(
        functools.partial(_compress_kernel, normalize=normalize),
        grid=(batch,),
        in_specs=[pl.BlockSpec((nch, width), lambda b: (b, 0)),
                  pl.BlockSpec((1, width), const), pl.BlockSpec((1, width), const),
                  pl.BlockSpec(wa.shape, const), pl.BlockSpec(wb.shape, const),
                  pl.BlockSpec((CMP_HIDDEN, HEAD_DIM), const), pl.BlockSpec((1, HEAD_DIM), const)],
        out_specs=pl.BlockSpec((1, nch, NSA_KV * HEAD_DIM), lambda b: (b, 0, 0)),
        out_shape=jax.ShapeDtypeStruct((batch, nch, NSA_KV * HEAD_DIM), BF16),
        compiler_params=_params(("parallel",)),
        name="nsa_compress",
    )(x, pea, peb, wa, wb, w2.astype(BF16), g)


def _cmp_to_sel_matrix(nch, n_sel_pad):
    n_cmp = nch - 1
    c0 = np.arange(nch)[:, None] * CMP_STRIDE
    s0 = np.arange(n_sel_pad)[None, :] * SEL_BLOCK
    ov = np.clip(np.minimum(c0 + CMP_LEN, s0 + SEL_BLOCK) - np.maximum(c0, s0), 0, None) / CMP_LEN
    ov[n_cmp:] = 0.0
    return jnp.asarray(ov.T, dtype=BF16)


def _nsa_cmp_kernel(q_ref, kc_ref, vc_ref, mt_ref, eye_ref, tile_of_ref, ocmp_ref, selb_ref, flags_ref,
                    *, tq, nch, n_sel):
    g = pl.program_id(1)
    i = pl.program_id(2)
    t = i * tq + lax.broadcasted_iota(jnp.int32, (tq, 1), 0)
    live = (t >= CMP_LEN - 1).astype(F32)
    gsc = _group_slope_scale(g)

    def attend(width):
        c_end = lax.broadcasted_iota(jnp.int32, (1, width), 1) * CMP_STRIDE + (CMP_LEN - 1)
        vis = c_end <= t
        distc = t.astype(F32) - (c_end.astype(F32) - (CMP_LEN - 1) / 2.0)
        kc = kc_ref[0, :width, :]
        vc = vc_ref[0, :width, :]
        psum = jnp.zeros((tq, width), F32)
        for j in range(HEADS_PER_GROUP):
            slope = (2.0 ** -(j + 1)) * gsc
            sl = slice(j * HEAD_DIM, (j + 1) * HEAD_DIM)
            s = _dot_nt(q_ref[:, sl], kc) - slope * distc
            s = jnp.where(vis, s, NEG)
            e = jnp.exp(s - jnp.max(s, axis=-1, keepdims=True))
            p = e * (live / jnp.sum(e, axis=-1, keepdims=True))
            psum = psum + p
            ocmp_ref[:, sl] = _dot(p.astype(BF16), vc)
        return _dot_nt(mt_ref[:, :width], psum.astype(BF16))

    widths = sorted({min(nch, LANES * (k + 1)) for k in range(pl.cdiv(nch, LANES))})
    visible = ((i + 1) * tq - CMP_LEN) // CMP_STRIDE + 1
    imp = lax.switch(jnp.minimum((visible - 1) // LANES, len(widths) - 1),
                     [functools.partial(attend, w) for w in widths])
    blk = lax.broadcasted_iota(jnp.int32, (LANES, 1), 0)
    blk_t = _div_pow2(i * tq + lax.broadcasted_iota(jnp.int32, (1, tq), 1), SEL_BLOCK)
    valid = blk <= blk_t
    forced = (blk == 0) | (blk == blk_t) | (blk == blk_t - 1)
    vals = jnp.where(valid & jnp.logical_not(forced) & (blk < n_sel), imp, -1.0)
    sel = valid & forced
    for _ in range(SEL_TOPN - 3):
        mx = jnp.max(vals, axis=0, keepdims=True)
        first = jnp.min(jnp.where(vals == mx, blk, LANES), axis=0, keepdims=True)
        pick = blk == first
        sel = sel | pick
        vals = jnp.where(pick, -3.0, vals)
    sel = sel & valid
    masked = jnp.where(sel, 0.0, NEG).astype(BF16)
    selb_ref[...] = _dot_nt(eye_ref[...], masked).astype(selb_ref.dtype)

    t_lane = i * tq + lax.broadcasted_iota(jnp.int32, (1, tq), 1)
    early = blk < (SEGMENT // SEL_BLOCK) * _div_pow2(t_lane, SEGMENT)
    per_query = _dot(tile_of_ref[...], jnp.where(sel & early, 1.0, 0.0).astype(BF16))
    out_lane = lax.broadcasted_iota(jnp.int32, flags_ref.shape[1:], 1)
    flags = jnp.zeros(flags_ref.shape[1:], jnp.int32)
    for h in range(tq // SEGMENT):
        hit = jnp.max(per_query[:, h * SEGMENT:(h + 1) * SEGMENT], axis=1, keepdims=True) > 0.0
        flags = jnp.where((out_lane == h) & hit, 1, flags)
    flags_ref[0] = flags


def _nsa_cmp(qkn, kc, vc, batch, seq, tq, tk):
    t = batch * seq
    nch = seq // CMP_STRIDE
    n_sel = seq // SEL_BLOCK
    n_all = seq // tk
    assert n_sel <= LANES and n_all <= TILE_ROWS and tq % SEGMENT == 0
    nq = seq // tq
    gw = HEADS_PER_GROUP * HEAD_DIM
    mt = _cmp_to_sel_matrix(nch, LANES)
    eye = jnp.asarray(np.eye(tq, dtype=np.float32), dtype=BF16)
    tile_of = jnp.asarray(np.arange(LANES)[None, :] // (tk // SEL_BLOCK) == np.arange(TILE_ROWS)[:, None], dtype=BF16)
    row = lambda b, g, i: (b * nq + i, g)
    ocmp, selb, flags = pl.pallas_call(
        functools.partial(_nsa_cmp_kernel, tq=tq, nch=nch, n_sel=n_sel),
        grid=(batch, NSA_KV, nq),
        in_specs=[pl.BlockSpec((tq, gw), row),
                  pl.BlockSpec((1, nch, HEAD_DIM), lambda b, g, i: (b, 0, g)),
                  pl.BlockSpec((1, nch, HEAD_DIM), lambda b, g, i: (b, 0, g)),
                  pl.BlockSpec((LANES, nch), lambda b, g, i: (0, 0)),
                  pl.BlockSpec((tq, tq), lambda b, g, i: (0, 0)),
                  pl.BlockSpec((TILE_ROWS, LANES), lambda b, g, i: (0, 0))],
        out_specs=[pl.BlockSpec((tq, gw), row), pl.BlockSpec((tq, LANES), row),
                   pl.BlockSpec((1, TILE_ROWS, LANES), lambda b, g, i: ((b * NSA_KV + g) * nq + i, 0, 0))],
        out_shape=[jax.ShapeDtypeStruct((t, NSA_HEADS * HEAD_DIM), F32),
                   jax.ShapeDtypeStruct((t, NSA_KV * LANES), BF16),
                   jax.ShapeDtypeStruct((batch * NSA_KV * nq, TILE_ROWS, LANES), jnp.int32)],
        compiler_params=_params(("parallel", "parallel", "parallel")),
        name="nsa_cmp_select",
    )(qkn, kc, vc, mt, eye, tile_of)
    segs = tq // SEGMENT
    tile_flags = flags[:, :n_all, :segs].transpose(0, 2, 1).reshape(batch, NSA_KV, nq * segs, n_all)
    return ocmp, selb, tile_flags


def _key_aux_blocks(seq):
    j = np.arange(seq)
    aux = (j[:, None] // SEL_BLOCK == np.arange(LANES)[None, :]).astype(np.float32)
    aux[:, 0] = j % SEL_BLOCK
    return jnp.asarray(aux, dtype=BF16)


def _key_aux_position(seq):
    assert seq <= BF16_EXACT_INT * BF16_EXACT_INT
    j = np.arange(seq)
    aux = np.zeros((seq, LANES), np.float32)
    aux[:, 0] = j % BF16_EXACT_INT
    aux[:, 1] = j - j % BF16_EXACT_INT
    return jnp.asarray(aux, dtype=BF16)


def _query_aug(q_ref, aux_cols):
    parts = [jnp.concatenate([q_ref[:, j * HEAD_DIM:(j + 1) * HEAD_DIM], aux_cols[j].astype(BF16)], axis=1)
             for j in range(HEADS_PER_GROUP)]
    return jnp.concatenate(parts, axis=0)


def _flash_step(s, vt, carry):
    m, l, acc = carry
    m_new = jnp.maximum(m, jnp.max(s, axis=0, keepdims=True))
    alpha = jnp.exp(m - m_new)
    p = jnp.exp(s - m_new).astype(BF16)
    pv = _dot(jnp.concatenate([vt, jnp.ones((BF16_SUBLANES, vt.shape[1]), BF16)], axis=0), p)
    return m_new, alpha * l + pv[HEAD_DIM:HEAD_DIM + 1], alpha * acc + pv[:HEAD_DIM]


def _flash_init(cols):
    return (jnp.full((1, cols), NEG, F32), jnp.zeros((1, cols), F32), jnp.zeros((HEAD_DIM, cols), F32))


def _transposed_values(v, batch, seq, groups):
    return v.reshape(batch, seq // SEGMENT, SEGMENT, groups, HEAD_DIM).transpose(0, 3, 1, 4, 2)


def _pipelined_tiles(n, n_all, scores_fn, consume_fn, s_a, s_b, carry):
    s_a[...] = scores_fn(0)

    def pairs(k, count, carry):
        for c in range(count):
            s_b[...] = scores_fn(k + 2 * c + 1)
            carry = consume_fn(k + 2 * c, s_a[...], carry)
            s_a[...] = scores_fn(jnp.minimum(k + 2 * c + 2, n_all - 1))
            carry = consume_fn(k + 2 * c + 1, s_b[...], carry)
        return carry

    quads = lax.shift_right_logical(n, 2)
    carry = lax.fori_loop(0, quads, lambda u, c: pairs(4 * u, 2, c), carry)
    done = 4 * quads
    carry = lax.cond((n & 2) == 2, lambda c: pairs(done, 1, c), lambda c: c, carry)
    return lax.cond((n & 1) == 1, lambda c: consume_fn(n - 1, s_a[...], c), lambda c: c, carry)


def _nsa_main_kernel(tiles_ref, count_ref, q_ref, ks_ref, vst_ref, kw_ref, vwt_ref, kaux_ref, kpos_ref, selb_ref,
                     ocmp_ref, gate_ref, z_ref, o_ref, s_a, s_b, *, tq, tk, seq):
    b = pl.program_id(0)
    g = pl.program_id(1)
    i = pl.program_id(2)
    n_all = seq // tk
    per_tile = tk // SEGMENT
    t0 = pl.multiple_of(i * tq, tq)
    cols = HEADS_PER_GROUP * tq
    gsc = _group_slope_scale(g)
    t_cols = t0 + (lax.broadcasted_iota(jnp.int32, (1, cols), 1) & (tq - 1))
    slopes = [(2.0 ** -(j + 1)) * gsc for j in range(HEADS_PER_GROUP)]

    def values(ref, first, count):
        return jnp.concatenate([ref[0, 0, first + c] for c in range(count)], axis=1)

    lane = lax.broadcasted_iota(jnp.int32, (tq, LANES), 1)
    chosen = selb_ref[...].astype(F32) > -1.0
    own_first = _div_pow2(t0, SEL_BLOCK)
    block_start = (lane * SEL_BLOCK).astype(F32)

    def masked_queries(blocks):
        return _query_aug(q_ref, [jnp.where(lane == 0, sl, jnp.where(chosen & blocks, sl * block_start, NEG))
                                  for sl in slopes])

    q_aug = masked_queries(lane < own_first)

    slot = (b * NSA_KV + g) * (seq // tq) + i
    base = slot * n_all

    def scores(u):
        j0 = pl.multiple_of(tiles_ref[base + u] * tk, tk)
        return _dot_nt(jnp.concatenate([ks_ref[pl.ds(j0, tk), :], kaux_ref[pl.ds(j0, tk), :]], axis=1), q_aug)

    def consume(u, s, carry):
        return _flash_step(s, values(vst_ref, tiles_ref[base + u] * per_tile, per_tile), carry)

    carry = _pipelined_tiles(count_ref[slot], n_all, scores, consume, s_a, s_b, _flash_init(cols))

    j_own = t0 + lax.broadcasted_iota(jnp.int32, (tq, 1), 0)
    s = _dot_nt(jnp.concatenate([ks_ref[pl.ds(t0, tq), :], kaux_ref[pl.ds(t0, tq), :]], axis=1),
                masked_queries(lane >= own_first))
    s = jnp.where(j_own <= t_cols, s, NEG)
    _, l_s, acc_s = _flash_step(s, values(vst_ref, i, 1), carry)
    o_slc = acc_s / l_s

    q_pos = _query_aug(q_ref, [jnp.where(lane < 2, sl, 0.0) for sl in slopes])
    wk = WINDOW + tq
    start = pl.multiple_of(jnp.maximum(t0 - WINDOW, 0), SEGMENT)
    j_win = start + lax.broadcasted_iota(jnp.int32, (wk, 1), 0)
    s = _dot_nt(jnp.concatenate([kw_ref[pl.ds(start, wk), :], kpos_ref[pl.ds(start, wk), :]], axis=1), q_pos)
    s = jnp.where((j_win <= t_cols) & (j_win > t_cols - WINDOW), s, NEG)
    _, l_w, acc_w = _flash_step(s, values(vwt_ref, _div_pow2(start, SEGMENT), wk // SEGMENT), _flash_init(cols))
    o_win = acc_w / l_w

    for j in range(HEADS_PER_GROUP):
        sl = slice(j * HEAD_DIM, (j + 1) * HEAD_DIM)
        cs = slice(j * tq, (j + 1) * tq)
        gates = [_sigmoid(gate_ref[:, 3 * j + c:3 * j + c + 1]) for c in range(3)]
        o = gates[0] * ocmp_ref[:, sl] + gates[1] * o_slc[:, cs].T + gates[2] * o_win[:, cs].T
        o_ref[:, sl] = (o * z_ref[:, sl]).astype(o_ref.dtype)


def _nsa_main(qkn, vals_t, selb, tile_flags, ocmp, gates, zact, batch, seq, tq, tk):
    t = batch * seq
    nq = seq // tq
    n_all = seq // tk
    gw = HEADS_PER_GROUP * HEAD_DIM
    flags = tile_flags.reshape(batch * NSA_KV * nq, n_all)
    tiles = jnp.argsort(1 - flags, axis=-1, stable=True).astype(jnp.int32).reshape(-1)
    counts = jnp.sum(flags, axis=-1).astype(jnp.int32)
    row = lambda b, g, i, *_: (b * nq + i, g)
    ks_blk = (NSA_HEADS * HEAD_DIM) // HEAD_DIM
    kw_blk = ks_blk + NSA_KV
    grid_spec = pltpu.PrefetchScalarGridSpec(
        num_scalar_prefetch=2,
        grid=(batch, NSA_KV, nq),
        in_specs=[pl.BlockSpec((tq, gw), row),
                  pl.BlockSpec((seq, HEAD_DIM), lambda b, g, i, *_: (b, ks_blk + g)),
                  pl.BlockSpec((1, 1, seq // SEGMENT, HEAD_DIM, SEGMENT), lambda b, g, i, *_: (b, g, 0, 0, 0)),
                  pl.BlockSpec((seq, HEAD_DIM), lambda b, g, i, *_: (b, kw_blk + g)),
                  pl.BlockSpec((1, 1, seq // SEGMENT, HEAD_DIM, SEGMENT),
                               lambda b, g, i, *_: (b, NSA_KV + g, 0, 0, 0)),
                  pl.BlockSpec((seq, LANES), lambda b, g, i, *_: (0, 0)),
                  pl.BlockSpec((seq, LANES), lambda b, g, i, *_: (0, 0)),
                  pl.BlockSpec((tq, LANES), row),
                  pl.BlockSpec((tq, gw), row),
                  pl.BlockSpec((tq, LANES), row),
                  pl.BlockSpec((tq, gw), row)],
        out_specs=pl.BlockSpec((tq, gw), row),
        scratch_shapes=[pltpu.VMEM((tk, HEADS_PER_GROUP * tq), F32)] * 2,
    )
    return pl.pallas_call(
        functools.partial(_nsa_main_kernel, tq=tq, tk=tk, seq=seq),
        grid_spec=grid_spec,
        out_shape=jax.ShapeDtypeStruct((t, NSA_HEADS * HEAD_DIM), BF16),
        compiler_params=_params(("parallel", "parallel", "arbitrary")),
        name="nsa_select_window",
    )(tiles, counts, qkn, qkn, vals_t, qkn, vals_t, _key_aux_blocks(seq), _key_aux_position(seq), selb, ocmp, gates,
      zact)


def _threshold_value(code):
    return pltpu.bitcast(jnp.where(code < 0, code ^ 0x7FFFFFFF, code), F32)


def _dsa_index_kernel(qi_ref, ki_ref, wi_ref, tri_ref, bias_ref, score_scr, *, tq, tk, seq, topk):
    i = pl.program_id(1)
    t0 = i * tq
    n_all = seq // tk
    n_kt = _div_pow2(t0 + tq + tk - 1, tk)
    t = t0 + lax.broadcasted_iota(jnp.int32, (1, tq), 1)
    wrow = wi_ref[...] * (IDX_HEADS ** -0.5 * IDX_DIM ** -0.5)
    hb = 4

    def key_ids(kt):
        return kt * tk + lax.broadcasted_iota(jnp.int32, (tk, 1), 0)

    def score_tile(kt):
        j0 = pl.multiple_of(kt * tk, tk)
        ki = ki_ref[pl.ds(j0, tk), :]
        score = jnp.zeros((tk, tq), F32)
        for h0 in range(0, IDX_HEADS, hb):
            x = _dot_nt(ki, qi_ref[0, h0:h0 + hb].reshape(hb * tq, IDX_DIM))
            for h in range(hb):
                score = score + jnp.maximum(x[:, h * tq:(h + 1) * tq], 0.0) * wrow[h0 + h:h0 + h + 1, :]
        score = jnp.where(key_ids(kt) <= t, score, -jnp.inf)
        score_scr[kt] = score
        return score

    def hits(pred, score, c):
        return c + jnp.sum(jnp.where(pred(score), 1, 0).reshape(tk // 8, 8, tq), axis=0)

    def non_negative(s):
        return s >= 0.0

    n_pairs = lax.shift_right_logical(n_kt + 1, 1)

    def score_step(u, c):
        for kt in (2 * u, 2 * u + 1):
            c = hits(non_negative, score_tile(kt), c)
        return c

    c0 = jnp.sum(lax.fori_loop(0, n_pairs, score_step, jnp.zeros((8, tq), jnp.int32)), axis=0, keepdims=True)

    def count(pred):
        def step(u, c):
            for kt in (2 * u, 2 * u + 1):
                c = hits(pred, score_scr[kt], c)
            return c
        c = lax.fori_loop(0, n_pairs, step, jnp.zeros((8, tq), jnp.int32))
        return jnp.sum(c, axis=0, keepdims=True)

    everything = jnp.int32(seq + tk)
    state = (jnp.where(c0 >= topk, 0, INT_MIN), jnp.where(c0 >= topk, c0, everything))

    def bit_step(b, state):
        v, n_ge = state
        cand = v | (jnp.int32(1) << (30 - b))
        threshold = _threshold_value(cand)
        c = count(lambda s: s >= threshold)
        return jnp.where(c >= topk, cand, v), jnp.where(c >= topk, c, n_ge)

    def settled(state):
        done = (state[1] == topk) | (t < topk)
        return jnp.min(jnp.where(done, 1, 0)) > 0

    state = lax.fori_loop(0, RADIX_CHECKS[0], bit_step, state)
    for lo, hi in zip(RADIX_CHECKS, RADIX_CHECKS[1:] + (31,)):
        state = lax.cond(settled(state), lambda s: s, lambda s, lo=lo, hi=hi: lax.fori_loop(lo, hi, bit_step, s), state)
    v, n_ge = state
    threshold = _threshold_value(v)
    take_all = v == INT_MIN

    def emit(kt, keep):
        bias_ref[0, 0, kt] = jnp.where(keep & (key_ids(kt) <= t), 0.0, NEG).astype(bias_ref.dtype)

    def fill_step(kt, _):
        bias_ref[0, 0, kt] = jnp.full((tk, tq), NEG, bias_ref.dtype)
        return 0

    def emit_no_ties(_):
        def step(u, c):
            for kt in (2 * u, 2 * u + 1):
                emit(kt, (score_scr[kt] >= threshold) | take_all)
            return c
        return lax.fori_loop(0, n_pairs, step, 0)

    def emit_with_ties(_):
        need = (topk - count(lambda s: s > threshold)).astype(F32)

        def step(kt, carry):
            score = score_scr[kt]
            eq = score == threshold
            eqf = jnp.where(eq, 1.0, 0.0)
            before = carry + _dot(tri_ref[...], eqf.astype(BF16)) - eqf
            emit(kt, (score > threshold) | (eq & (before < need)) | take_all)
            return carry + jnp.sum(eqf, axis=0, keepdims=True)

        lax.fori_loop(0, n_kt, step, jnp.zeros((1, tq), F32))
        return lax.fori_loop(n_kt, 2 * n_pairs, fill_step, 0)

    over = (n_ge > topk) & jnp.logical_not(take_all)
    lax.cond(jnp.max(jnp.where(over, 1, 0)) > 0, emit_with_ties, emit_no_ties, 0)
    lax.fori_loop(2 * n_pairs, n_all, fill_step, 0)


def _dsa_index(qi_heads, ki, wi_t, batch, seq, tq, tk):
    nq = seq // tq
    n_all = seq // tk
    topk = min(DSA_TOPK_MAX, seq // 4)
    tri = jnp.asarray(np.tril(np.ones((tk, tk), np.float32)), dtype=BF16)
    return pl.pallas_call(
        functools.partial(_dsa_index_kernel, tq=tq, tk=tk, seq=seq, topk=topk),
        grid=(batch, nq),
        in_specs=[pl.BlockSpec((1, IDX_HEADS, tq, IDX_DIM), lambda b, i: (b, 0, i, 0)),
                  pl.BlockSpec((seq, IDX_DIM), lambda b, i: (b, 0)),
                  pl.BlockSpec((IDX_HEADS, tq), lambda b, i: (0, b * nq + i)),
                  pl.BlockSpec((tk, tk), lambda b, i: (0, 0))],
        out_specs=pl.BlockSpec((1, 1, n_all, tk, tq), lambda b, i: (b, i, 0, 0, 0)),
        out_shape=jax.ShapeDtypeStruct((batch, nq, n_all, tk, tq), BF16),
        scratch_shapes=[pltpu.VMEM((n_all, tk, tq), F32)],
        compiler_params=_params(("parallel", "arbitrary")),
        name="dsa_index_topk",
    )(qi_heads, ki, wi_t, tri)


def _dsa_attn_kernel(q_ref, k_ref, vt_ref, kaux_ref, bias_ref, z_ref, o_ref, s_a, s_b, *, tq, tk, seq):
    g = pl.program_id(1)
    i = pl.program_id(2)
    t0 = i * tq
    cols = HEADS_PER_GROUP * tq
    gsc = _group_slope_scale(g)
    slopes = [(2.0 ** -(j + 1)) * gsc for j in range(HEADS_PER_GROUP)]
    lane = lax.broadcasted_iota(jnp.int32, (tq, LANES), 1)
    q_aug = _query_aug(q_ref, [jnp.where(lane < 2, sl, 0.0) for sl in slopes])
    per_tile = tk // SEGMENT

    def scores(kt):
        j0 = pl.multiple_of(kt * tk, tk)
        return _dot_nt(jnp.concatenate([k_ref[pl.ds(j0, tk), :], kaux_ref[pl.ds(j0, tk), :]], axis=1), q_aug)

    def consume(kt, s, carry):
        bias = [bias_ref[0, part, kt].astype(F32) for part in range(bias_ref.shape[1])]
        s = s + jnp.concatenate(bias * HEADS_PER_GROUP, axis=1)
        vt = jnp.concatenate([vt_ref[0, 0, kt * per_tile + c] for c in range(per_tile)], axis=1)
        return _flash_step(s, vt, carry)

    n_kt = _div_pow2(t0 + tq + tk - 1, tk)
    _, l, acc = _pipelined_tiles(n_kt, seq // tk, scores, consume, s_a, s_b, _flash_init(cols))
    o = acc / l
    for j in range(HEADS_PER_GROUP):
        sl = slice(j * HEAD_DIM, (j + 1) * HEAD_DIM)
        o_ref[:, sl] = (o[:, j * tq:(j + 1) * tq].T * z_ref[:, sl]).astype(o_ref.dtype)


def _dsa_attn(qkn, vals_t, bias, zact, batch, seq, tq, tk):
    t = batch * seq
    nq = seq // tq
    gw = HEADS_PER_GROUP * HEAD_DIM
    row = lambda b, g, i: (b * nq + i, g)
    q_blk = (NSA_HEADS + 2 * NSA_KV) * HEAD_DIM // gw
    k_blk = (NSA_HEADS + 2 * NSA_KV + DSA_HEADS)
    v_grp = 2 * NSA_KV
    return pl.pallas_call(
        functools.partial(_dsa_attn_kernel, tq=tq, tk=tk, seq=seq),
        grid=(batch, DSA_KV, nq),
        in_specs=[pl.BlockSpec((tq, gw), lambda b, g, i: (b * nq + i, q_blk + g)),
                  pl.BlockSpec((seq, HEAD_DIM), lambda b, g, i: (b, k_blk + g)),
                  pl.BlockSpec((1, 1, seq // SEGMENT, HEAD_DIM, SEGMENT), lambda b, g, i: (b, v_grp + g, 0, 0, 0)),
                  pl.BlockSpec((seq, LANES), lambda b, g, i: (0, 0)),
                  pl.BlockSpec((1, tq // bias.shape[4]) + bias.shape[2:], lambda b, g, i: (b, i, 0, 0, 0)),
                  pl.BlockSpec((tq, gw), lambda b, g, i: (b * nq + i, NSA_KV + g))],
        out_specs=pl.BlockSpec((tq, gw), row),
        out_shape=jax.ShapeDtypeStruct((t, DSA_HEADS * HEAD_DIM), BF16),
        scratch_shapes=[pltpu.VMEM((tk, HEADS_PER_GROUP * tq), F32)] * 2,
        compiler_params=_params(("parallel", "parallel", "arbitrary")),
        name="dsa_attention",
    )(qkn, qkn, vals_t, _key_aux_position(seq), bias, zact)


def _merge_kernel(on_ref, od_ref, sn_ref, sd_ref, x_ref, wun_ref, wud_ref, wout_ref, o_ref):
    y = sn_ref[...] * _dot(on_ref[...], wun_ref[...]) + sd_ref[...] * _dot(od_ref[...], wud_ref[...])
    o_ref[...] = x_ref[...] + _dot(y.astype(BF16), wout_ref[...])


def _merge(o_nsa, o_dsa, gate_act, x, w_up_nsa, w_up_dsa, w_out, tm):
    t, d = x.shape
    const = lambda i: (0, 0)
    return pl.pallas_call(
        _merge_kernel,
        grid=(t // tm,),
        in_specs=[pl.BlockSpec((tm, o_nsa.shape[1]), lambda i: (i, 0)),
                  pl.BlockSpec((tm, o_dsa.shape[1]), lambda i: (i, 0)),
                  pl.BlockSpec((tm, d), lambda i: (i, 0)),
                  pl.BlockSpec((tm, d), lambda i: (i, 1)),
                  pl.BlockSpec((tm, d), lambda i: (i, 0)),
                  pl.BlockSpec(w_up_nsa.shape, const), pl.BlockSpec(w_up_dsa.shape, const),
                  pl.BlockSpec(w_out.shape, const)],
        out_specs=pl.BlockSpec((tm, d), lambda i: (i, 0)),
        out_shape=jax.ShapeDtypeStruct((t, d), F32),
        compiler_params=_params(("parallel",)),
        name="merge_out_proj",
    )(o_nsa, o_dsa, gate_act, gate_act, x, w_up_nsa, w_up_dsa, w_out)


def _ple_kernel(x_ref, p_ref, g_ref, wg_ref, wp_ref, o_ref):
    x = x_ref[...]
    ms = jnp.mean(x * x, axis=-1, keepdims=True)
    r = (x * lax.rsqrt(ms + EPS) * g_ref[...]).astype(BF16)
    gate = _sigmoid(_dot(r, wg_ref[...]))
    o_ref[...] = x + _dot(p_ref[...].astype(BF16), wp_ref[...]) * gate


def _ple(x, p, g, w_gate, w_proj, tm):
    t, d = x.shape
    const = lambda i: (0, 0)
    return pl.pallas_call(
        _ple_kernel,
        grid=(t // tm,),
        in_specs=[pl.BlockSpec((tm, d), lambda i: (i, 0)),
                  pl.BlockSpec((tm, p.shape[1]), lambda i: (i, 0)),
                  pl.BlockSpec((1, d), const),
                  pl.BlockSpec(w_gate.shape, const), pl.BlockSpec(w_proj.shape, const)],
        out_specs=pl.BlockSpec((tm, d), lambda i: (i, 0)),
        out_shape=jax.ShapeDtypeStruct((t, d), F32),
        compiler_params=_params(("parallel",)),
        name="ple_gate",
    )(x, p, g.reshape(1, d), w_gate, w_proj)


def _split_in_proj(w):
    widths = [NSA_HEADS * HEAD_DIM] + [NSA_KV * HEAD_DIM] * 6 + [NSA_HEADS * 3, NSA_HEADS * HEAD_DIM,
              DSA_HEADS * HEAD_DIM, DSA_KV * HEAD_DIM, DSA_KV * HEAD_DIM, IDX_HEADS * IDX_DIM, IDX_DIM,
              IDX_HEADS, DSA_HEADS * HEAD_DIM, D_MODEL, D_MODEL]
    offs = np.concatenate([[0], np.cumsum(widths)])
    assert offs[-1] == w.shape[1]
    (nq, nkc, nvc, nks, nvs, nkw, nvw, ng, nz, dq, dk, dv, iq, ik, iw, dz, mgn, mgd) = [
        w[:, offs[k]:offs[k + 1]] for k in range(len(widths))]
    pad = jnp.zeros((w.shape[0], LANES - IDX_DIM - IDX_HEADS - NSA_HEADS * 3), w.dtype)
    return dict(
        normed=jnp.concatenate([nq, nks, nkw, dq, dk], axis=1).astype(BF16),
        vals=jnp.concatenate([nvs, nvw, dv], axis=1).astype(BF16),
        iq=iq.astype(BF16),
        raw=jnp.concatenate([nkc, nvc, ik, iw, ng, pad], axis=1).astype(BF16),
        zact=jnp.concatenate([nz, dz], axis=1).astype(BF16),
        merge=jnp.concatenate([mgn, mgd], axis=1).astype(BF16),
    )


def _layer(x, p, norm_g, w_in, nsa_q_g, nsa_kc_g, nsa_ks_g, nsa_kw_g, cmp_pe_k, cmp_w1_k, cmp_w2_k,
           cmp_pe_v, cmp_w1_v, cmp_w2_v, dsa_q_g, dsa_k_g, w_up_nsa, w_up_dsa, w_out, ple_norm_g,
           w_ple_gate, w_ple_proj):
    batch, seq, d = x.shape
    t = batch * seq
    assert seq % SEGMENT == 0 and seq >= WINDOW + SEGMENT
    x2 = x.reshape(t, d)
    tiles = _Tiles.for_shape(batch, seq)
    tm = tiles.proj_rows

    w = _split_in_proj(w_in)
    h = _rmsnorm(x2, norm_g, tiles.norm_rows)

    gains = jnp.concatenate([jnp.tile(nsa_q_g * ATTN_SCALE, NSA_HEADS), jnp.tile(nsa_ks_g, NSA_KV),
                             jnp.tile(nsa_kw_g, NSA_KV), jnp.tile(dsa_q_g * ATTN_SCALE, DSA_HEADS),
                             jnp.tile(dsa_k_g, DSA_KV)])
    qkn = _proj(h, w["normed"], tm, w["normed"].shape[1] // 2, BF16, gain=gains)
    vals = _proj(h, w["vals"], tm, w["vals"].shape[1], BF16)
    vals_t = _transposed_values(vals, batch, seq, 2 * NSA_KV + DSA_KV)
    qi_heads = _proj_heads(h, w["iq"], min(tm, seq), batch, seq)
    kc_raw, vc_raw, misc = _proj_split(h, w["raw"], tm, (NSA_KV * HEAD_DIM, NSA_KV * HEAD_DIM, LANES))
    zact = _proj(h, w["zact"], tm, tiles.proj_cols, F32, act="silu")
    gate_act = _proj(h, w["merge"], tm, tiles.proj_cols, F32, act="sigmoid")

    ki = misc[:, :IDX_DIM].astype(BF16)
    wi = misc[:, IDX_DIM:IDX_DIM + IDX_HEADS]
    ng = misc[:, IDX_DIM + IDX_HEADS:IDX_DIM + IDX_HEADS + NSA_HEADS * 3]
    per_group = HEADS_PER_GROUP * 3
    gates = jnp.pad(ng.reshape(t, NSA_KV, per_group), ((0, 0), (0, 0), (0, LANES - per_group))).reshape(t, NSA_KV * LANES)

    kc = _compress(kc_raw, cmp_pe_k, cmp_w1_k, cmp_w2_k, nsa_kc_g, batch, seq)
    vc = _compress(vc_raw, cmp_pe_v, cmp_w1_v, cmp_w2_v, None, batch, seq)

    tk = tiles.keys
    ocmp, selb, tile_flags = _nsa_cmp(qkn, kc, vc, batch, seq, tiles.cmp_queries, tk)
    o_nsa = _nsa_main(qkn, vals_t, selb, tile_flags, ocmp, gates, zact, batch, seq, SEGMENT, tk)

    bias = _dsa_index(qi_heads, ki, wi.T, batch, seq, INDEX_QUERIES, tk)
    o_dsa = _dsa_attn(qkn, vals_t, bias, zact, batch, seq, SEGMENT, tk)

    x1 = _merge(o_nsa, o_dsa, gate_act, x2, w_up_nsa.astype(BF16), w_up_dsa.astype(BF16), w_out.astype(BF16),
                tiles.out_rows)
    x3 = _ple(x1, p.reshape(t, PLE_DIM), ple_norm_g, w_ple_gate.astype(BF16), w_ple_proj.astype(BF16),
              tiles.out_rows)
    return x3.reshape(batch, seq, d)


def kernel(x, p, norm_g, w_in, nsa_q_g, nsa_kc_g, nsa_ks_g, nsa_kw_g, cmp_pe_k, cmp_w1_k, cmp_w2_k, cmp_pe_v, cmp_w1_v, cmp_w2_v, dsa_q_g, dsa_k_g, w_up_nsa, w_up_dsa, w_out, ple_norm_g, w_ple_gate, w_ple_proj):
    depth = w_in.shape[0]
    for i in range(depth):
        x = _layer(x, p[i], norm_g[i], w_in[i], nsa_q_g[i], nsa_kc_g[i], nsa_ks_g[i], nsa_kw_g[i],
                   cmp_pe_k[i], cmp_w1_k[i], cmp_w2_k[i], cmp_pe_v[i], cmp_w1_v[i], cmp_w2_v[i],
                   dsa_q_g[i], dsa_k_g[i], w_up_nsa[i], w_up_dsa[i], w_out[i], ple_norm_g[i],
                   w_ple_gate[i], w_ple_proj[i])
    return x
```

```python
import functools
from typing import NamedTuple

import numpy as np
import jax
import jax.numpy as jnp
from jax import lax
from jax.experimental import pallas as pl
from jax.experimental.pallas import tpu as pltpu

D_MODEL = 2048
HEAD_DIM = 128
NSA_HEADS = 8
NSA_KV = 2
DSA_HEADS = 8
DSA_KV = 2
HEADS_PER_GROUP = 4
CMP_LEN = 32
CMP_STRIDE = 16
CMP_HIDDEN = 256
SEL_BLOCK = 64
SEL_TOPN = 16
WINDOW = 512
IDX_HEADS = 16
IDX_DIM = 64
DSA_TOPK_MAX = 256
PLE_DIM = 256
EPS = 1e-6
NEG = -1e30
ATTN_SCALE = HEAD_DIM ** -0.5
LANES = 128
INT_MIN = -(2 ** 31)
RADIX_CHECKS = (24, 27)
BF16_EXACT_INT = 256
BF16_SUBLANES = 16

VMEM_LIMIT_BYTES = 56 * 1024 * 1024

F32 = jnp.float32
BF16 = jnp.bfloat16

SEGMENT = 256
INDEX_QUERIES = 128
TILE_ROWS = 16


class _Tiles(NamedTuple):
    norm_rows: int
    proj_rows: int
    proj_cols: int
    out_rows: int
    cmp_queries: int
    keys: int

    @classmethod
    def for_shape(cls, batch, seq):
        t = batch * seq
        return cls(norm_rows=min(512, t), proj_rows=min(1024, t), proj_cols=1024, out_rows=min(256, t),
                   cmp_queries=min(512, seq), keys=min(512, seq))


def _params(semantics):
    return pltpu.CompilerParams(dimension_semantics=semantics, vmem_limit_bytes=VMEM_LIMIT_BYTES)


def _sigmoid(x):
    return 1.0 / (1.0 + jnp.exp(-x))


def _dot(a, b):
    return jnp.dot(a, b, preferred_element_type=F32)


def _dot_nt(a, b):
    return lax.dot_general(a, b, (((1,), (1,)), ((), ())), preferred_element_type=F32)


def _log2(n):
    assert n > 0 and n & (n - 1) == 0, n
    return n.bit_length() - 1


def _div_pow2(x, n):
    return lax.shift_right_logical(x, jnp.int32(_log2(n)))


def _group_slope_scale(g):
    return jnp.where(g == 0, 1.0, 2.0 ** -HEADS_PER_GROUP).astype(F32)


def _rmsnorm_kernel(x_ref, g_ref, o_ref):
    x = x_ref[...]
    ms = jnp.mean(x * x, axis=-1, keepdims=True)
    o_ref[...] = (x * lax.rsqrt(ms + EPS) * g_ref[...]).astype(o_ref.dtype)


def _rmsnorm(x, g, tm):
    t, d = x.shape
    return pl.pallas_call(
        _rmsnorm_kernel,
        grid=(t // tm,),
        in_specs=[pl.BlockSpec((tm, d), lambda i: (i, 0)), pl.BlockSpec((1, d), lambda i: (0, 0))],
        out_specs=pl.BlockSpec((tm, d), lambda i: (i, 0)),
        out_shape=jax.ShapeDtypeStruct((t, d), BF16),
        compiler_params=_params(("parallel",)),
        name="in_rmsnorm",
    )(x, g.reshape(1, d))


def _proj_headnorm_kernel(h_ref, w_ref, g_ref, o_ref):
    y = _dot(h_ref[...], w_ref[...])
    for hd in range(y.shape[1] // HEAD_DIM):
        sl = slice(hd * HEAD_DIM, (hd + 1) * HEAD_DIM)
        yh = y[:, sl]
        ms = jnp.mean(yh * yh, axis=-1, keepdims=True)
        o_ref[:, sl] = (yh * lax.rsqrt(ms + EPS) * g_ref[:, sl]).astype(o_ref.dtype)


def _proj_act_kernel(h_ref, w_ref, o_ref, *, act):
    y = _dot(h_ref[...], w_ref[...])
    if act == "silu":
        y = y * _sigmoid(y)
    elif act == "sigmoid":
        y = _sigmoid(y)
    o_ref[...] = y.astype(o_ref.dtype)


def _proj_split_kernel(h_ref, w_ref, *o_refs):
    y = _dot(h_ref[...], w_ref[...])
    off = 0
    for o_ref in o_refs:
        n = o_ref.shape[1]
        o_ref[...] = y[:, off:off + n].astype(o_ref.dtype)
        off += n


def _proj(h, w, tm, tn, out_dtype, act=None, gain=None):
    t, d = h.shape
    n = w.shape[1]
    in_specs = [pl.BlockSpec((tm, d), lambda i, j: (i, 0)), pl.BlockSpec((d, tn), lambda i, j: (0, j))]
    args = [h, w]
    if gain is not None:
        body = _proj_headnorm_kernel
        in_specs.append(pl.BlockSpec((1, tn), lambda i, j: (0, j)))
        args.append(gain.reshape(1, n))
    else:
        body = functools.partial(_proj_act_kernel, act=act)
    return pl.pallas_call(
        body,
        grid=(t // tm, n // tn),
        in_specs=in_specs,
        out_specs=pl.BlockSpec((tm, tn), lambda i, j: (i, j)),
        out_shape=jax.ShapeDtypeStruct((t, n), out_dtype),
        compiler_params=_params(("parallel", "arbitrary")),
        name="in_proj_" + ("headnorm" if gain is not None else str(act)),
    )(*args)


def _proj_heads_kernel(h_ref, w_ref, o_ref):
    y = _dot(h_ref[...], w_ref[...])
    for hd in range(o_ref.shape[1]):
        o_ref[0, hd] = y[:, hd * IDX_DIM:(hd + 1) * IDX_DIM].astype(o_ref.dtype)


def _proj_heads(h, w, tm, batch, seq):
    t, d = h.shape
    per_batch = seq // tm
    return pl.pallas_call(
        _proj_heads_kernel,
        grid=(t // tm,),
        in_specs=[pl.BlockSpec((tm, d), lambda i: (i, 0)), pl.BlockSpec(w.shape, lambda i: (0, 0))],
        out_specs=pl.BlockSpec((1, IDX_HEADS, tm, IDX_DIM), lambda i: (i // per_batch, 0, i % per_batch, 0)),
        out_shape=jax.ShapeDtypeStruct((batch, IDX_HEADS, seq, IDX_DIM), BF16),
        compiler_params=_params(("parallel",)),
        name="in_proj_heads",
    )(h, w)


def _proj_split(h, w, tm, widths):
    t, d = h.shape
    n = w.shape[1]
    return pl.pallas_call(
        _proj_split_kernel,
        grid=(t // tm,),
        in_specs=[pl.BlockSpec((tm, d), lambda i: (i, 0)), pl.BlockSpec((d, n), lambda i: (0, 0))],
        out_specs=[pl.BlockSpec((tm, wd), lambda i: (i, 0)) for wd in widths],
        out_shape=[jax.ShapeDtypeStruct((t, wd), F32) for wd in widths],
        compiler_params=_params(("parallel",)),
        name="in_proj_split",
    )(h, w)


def _compress_kernel(x_ref, pea_ref, peb_ref, wa_ref, wb_ref, w2_ref, g_ref, o_ref, *, normalize):
    x = x_ref[...]
    nch = x.shape[0]
    ha = _dot((x + pea_ref[...]).astype(BF16), wa_ref[...])
    hb = _dot((x + peb_ref[...]).astype(BF16), wb_ref[...])
    hid = ha + pltpu.roll(hb, nch - 1, axis=0)
    act = (hid * _sigmoid(hid)).astype(BF16)
    for g in range(NSA_KV):
        o = _dot(act[:, g * CMP_HIDDEN:(g + 1) * CMP_HIDDEN], w2_ref[...])
        if normalize:
            ms = jnp.mean(o * o, axis=-1, keepdims=True)
            o = o * lax.rsqrt(ms + EPS) * g_ref[...]
        o_ref[0, :, g * HEAD_DIM:(g + 1) * HEAD_DIM] = o.astype(o_ref.dtype)


def _expand_compress_params(pe, w1):
    half = CMP_LEN // 2
    w = w1.reshape(CMP_LEN, HEAD_DIM, CMP_HIDDEN)
    eye = jnp.eye(NSA_KV, dtype=w1.dtype)

    def expand_w(wh):
        return jnp.einsum("ldj,hg->lhdgj", wh, eye).reshape(half * NSA_KV * HEAD_DIM, NSA_KV * CMP_HIDDEN)

    def expand_pe(ph):
        return jnp.broadcast_to(ph[:, None, :], (half, NSA_KV, HEAD_DIM)).reshape(1, half * NSA_KV * HEAD_DIM)

    return (expand_pe(pe[:half]), expand_pe(pe[half:]),
            expand_w(w[:half]).astype(BF16), expand_w(w[half:]).astype(BF16))


def _compress(raw, pe, w1, w2, gain, batch, seq):
    nch = seq // CMP_STRIDE
    width = CMP_STRIDE * NSA_KV * HEAD_DIM
    x = raw.reshape(batch * nch, width)
    pea, peb, wa, wb = _expand_compress_params(pe, w1)
    normalize = gain is not None
    g = (gain if normalize else jnp.ones((HEAD_DIM,), F32)).reshape(1, HEAD_DIM)
    const = lambda b: (0, 0)
    return pl.pallas_call(
        functools.partial(_compress_kernel, normalize=normalize),
        grid=(batch,),
        in_specs=[pl.BlockSpec((nch, width), lambda b: (b, 0)),
                  pl.BlockSpec((1, width), const), pl.BlockSpec((1, width), const),
                  pl.BlockSpec(wa.shape, const), pl.BlockSpec(wb.shape, const),
                  pl.BlockSpec((CMP_HIDDEN, HEAD_DIM), const), pl.BlockSpec((1, HEAD_DIM), const)],
        out_specs=pl.BlockSpec((1, nch, NSA_KV * HEAD_DIM), lambda b: (b, 0, 0)),
        out_shape=jax.ShapeDtypeStruct((batch, nch, NSA_KV * HEAD_DIM), BF16),
        compiler_params=_params(("parallel",)),
        name="nsa_compress",
    )(x, pea, peb, wa, wb, w2.astype(BF16), g)


def _cmp_to_sel_matrix(nch, n_sel_pad):
    n_cmp = nch - 1
    c0 = np.arange(nch)[:, None] * CMP_STRIDE
    s0 = np.arange(n_sel_pad)[None, :] * SEL_BLOCK
    ov = np.clip(np.minimum(c0 + CMP_LEN, s0 + SEL_BLOCK) - np.maximum(c0, s0), 0, None) / CMP_LEN
    ov[n_cmp:] = 0.0
    return jnp.asarray(ov.T, dtype=BF16)


def _nsa_cmp_kernel(q_ref, kc_ref, vc_ref, mt_ref, eye_ref, tile_of_ref, ocmp_ref, selb_ref, flags_ref,
                    *, tq, nch, n_sel):
    g = pl.program_id(1)
    i = pl.program_id(2)
    t = i * tq + lax.broadcasted_iota(jnp.int32, (tq, 1), 0)
    live = (t >= CMP_LEN - 1).astype(F32)
    gsc = _group_slope_scale(g)

    def attend(width):
        c_end = lax.broadcasted_iota(jnp.int32, (1, width), 1) * CMP_STRIDE + (CMP_LEN - 1)
        vis = c_end <= t
        distc = t.astype(F32) - (c_end.astype(F32) - (CMP_LEN - 1) / 2.0)
        kc = kc_ref[0, :width, :]
        vc = vc_ref[0, :width, :]
        psum = jnp.zeros((tq, width), F32)
        for j in range(HEADS_PER_GROUP):
            slope = (2.0 ** -(j + 1)) * gsc
            sl = slice(j * HEAD_DIM, (j + 1) * HEAD_DIM)
            s = _dot_nt(q_ref[:, sl], kc) - slope * distc
            s = jnp.where(vis, s, NEG)
            e = jnp.exp(s - jnp.max(s, axis=-1, keepdims=True))
            p = e * (live / jnp.sum(e, axis=-1, keepdims=True))
            psum = psum + p
            ocmp_ref[:, sl] = _dot(p.astype(BF16), vc)
        return _dot_nt(mt_ref[:, :width], psum.astype(BF16))

    widths = sorted({min(nch, LANES * (k + 1)) for k in range(pl.cdiv(nch, LANES))})
    visible = ((i + 1) * tq - CMP_LEN) // CMP_STRIDE + 1
    imp = lax.switch(jnp.minimum((visible - 1) // LANES, len(widths) - 1),
                     [functools.partial(attend, w) for w in widths])
    blk = lax.broadcasted_iota(jnp.int32, (LANES, 1), 0)
    blk_t = _div_pow2(i * tq + lax.broadcasted_iota(jnp.int32, (1, tq), 1), SEL_BLOCK)
    valid = blk <= blk_t
    forced = (blk == 0) | (blk == blk_t) | (blk == blk_t - 1)
    vals = jnp.where(valid & jnp.logical_not(forced) & (blk < n_sel), imp, -1.0)
    sel = valid & forced
    for _ in range(SEL_TOPN - 3):
        mx = jnp.max(vals, axis=0, keepdims=True)
        first = jnp.min(jnp.where(vals == mx, blk, LANES), axis=0, keepdims=True)
        pick = blk == first
        sel = sel | pick
        vals = jnp.where(pick, -3.0, vals)
    sel = sel & valid
    masked = jnp.where(sel, 0.0, NEG).astype(BF16)
    selb_ref[...] = _dot_nt(eye_ref[...], masked).astype(selb_ref.dtype)

    t_lane = i * tq + lax.broadcasted_iota(jnp.int32, (1, tq), 1)
    early = blk < (SEGMENT // SEL_BLOCK) * _div_pow2(t_lane, SEGMENT)
    per_query = _dot(tile_of_ref[...], jnp.where(sel & early, 1.0, 0.0).astype(BF16))
    out_lane = lax.broadcasted_iota(jnp.int32, flags_ref.shape[1:], 1)
    flags = jnp.zeros(flags_ref.shape[1:], jnp.int32)
    for h in range(tq // SEGMENT):
        hit = jnp.max(per_query[:, h * SEGMENT:(h + 1) * SEGMENT], axis=1, keepdims=True) > 0.0
        flags = jnp.where((out_lane == h) & hit, 1, flags)
    flags_ref[0] = flags


def _nsa_cmp(qkn, kc, vc, batch, seq, tq, tk):
    t = batch * seq
    nch = seq // CMP_STRIDE
    n_sel = seq // SEL_BLOCK
    n_all = seq // tk
    assert n_sel <= LANES and n_all <= TILE_ROWS and tq % SEGMENT == 0
    nq = seq // tq
    gw = HEADS_PER_GROUP * HEAD_DIM
    mt = _cmp_to_sel_matrix(nch, LANES)
    eye = jnp.asarray(np.eye(tq, dtype=np.float32), dtype=BF16)
    tile_of = jnp.asarray(np.arange(LANES)[None, :] // (tk // SEL_BLOCK) == np.arange(TILE_ROWS)[:, None], dtype=BF16)
    row = lambda b, g, i: (b * nq + i, g)
    ocmp, selb, flags = pl.pallas_call(
        functools.partial(_nsa_cmp_kernel, tq=tq, nch=nch, n_sel=n_sel),
        grid=(batch, NSA_KV, nq),
        in_specs=[pl.BlockSpec((tq, gw), row),
                  pl.BlockSpec((1, nch, HEAD_DIM), lambda b, g, i: (b, 0, g)),
                  pl.BlockSpec((1, nch, HEAD_DIM), lambda b, g, i: (b, 0, g)),
                  pl.BlockSpec((LANES, nch), lambda b, g, i: (0, 0)),
                  pl.BlockSpec((tq, tq), lambda b, g, i: (0, 0)),
                  pl.BlockSpec((TILE_ROWS, LANES), lambda b, g, i: (0, 0))],
        out_specs=[pl.BlockSpec((tq, gw), row), pl.BlockSpec((tq, LANES), row),
                   pl.BlockSpec((1, TILE_ROWS, LANES), lambda b, g, i: ((b * NSA_KV + g) * nq + i, 0, 0))],
        out_shape=[jax.ShapeDtypeStruct((t, NSA_HEADS * HEAD_DIM), F32),
                   jax.ShapeDtypeStruct((t, NSA_KV * LANES), BF16),
                   jax.ShapeDtypeStruct((batch * NSA_KV * nq, TILE_ROWS, LANES), jnp.int32)],
        compiler_params=_params(("parallel", "parallel", "parallel")),
        name="nsa_cmp_select",
    )(qkn, kc, vc, mt, eye, tile_of)
    segs = tq // SEGMENT
    tile_flags = flags[:, :n_all, :segs].transpose(0, 2, 1).reshape(batch, NSA_KV, nq * segs, n_all)
    return ocmp, selb, tile_flags


def _key_aux_blocks(seq):
    j = np.arange(seq)
    aux = (j[:, None] // SEL_BLOCK == np.arange(LANES)[None, :]).astype(np.float32)
    aux[:, 0] = j % SEL_BLOCK
    return jnp.asarray(aux, dtype=BF16)


def _key_aux_position(seq):
    assert seq <= BF16_EXACT_INT * BF16_EXACT_INT
    j = np.arange(seq)
    aux = np.zeros((seq, LANES), np.float32)
    aux[:, 0] = j % BF16_EXACT_INT
    aux[:, 1] = j - j % BF16_EXACT_INT
    return jnp.asarray(aux, dtype=BF16)


def _query_aug(q_ref, aux_cols):
    parts = [jnp.concatenate([q_ref[:, j * HEAD_DIM:(j + 1) * HEAD_DIM], aux_cols[j].astype(BF16)], axis=1)
             for j in range(HEADS_PER_GROUP)]
    return jnp.concatenate(parts, axis=0)


def _flash_step(s, vt, carry):
    m, l, acc = carry
    m_new = jnp.maximum(m, jnp.max(s, axis=0, keepdims=True))
    alpha = jnp.exp(m - m_new)
    p = jnp.exp(s - m_new).astype(BF16)
    pv = _dot(jnp.concatenate([vt, jnp.ones((BF16_SUBLANES, vt.shape[1]), BF16)], axis=0), p)
    return m_new, alpha * l + pv[HEAD_DIM:HEAD_DIM + 1], alpha * acc + pv[:HEAD_DIM]


def _flash_init(cols):
    return (jnp.full((1, cols), NEG, F32), jnp.zeros((1, cols), F32), jnp.zeros((HEAD_DIM, cols), F32))


def _transposed_values(v, batch, seq, groups):
    return v.reshape(batch, seq // SEGMENT, SEGMENT, groups, HEAD_DIM).transpose(0, 3, 1, 4, 2)


def _pipelined_tiles(n, n_all, scores_fn, consume_fn, s_a, s_b, carry):
    s_a[...] = scores_fn(0)

    def pairs(k, count, carry):
        for c in range(count):
            s_b[...] = scores_fn(k + 2 * c + 1)
            carry = consume_fn(k + 2 * c, s_a[...], carry)
            s_a[...] = scores_fn(jnp.minimum(k + 2 * c + 2, n_all - 1))
            carry = consume_fn(k + 2 * c + 1, s_b[...], carry)
        return carry

    quads = lax.shift_right_logical(n, 2)
    carry = lax.fori_loop(0, quads, lambda u, c: pairs(4 * u, 2, c), carry)
    done = 4 * quads
    carry = lax.cond((n & 2) == 2, lambda c: pairs(done, 1, c), lambda c: c, carry)
    return lax.cond((n & 1) == 1, lambda c: consume_fn(n - 1, s_a[...], c), lambda c: c, carry)


def _nsa_main_kernel(tiles_ref, count_ref, q_ref, ks_ref, vst_ref, kw_ref, vwt_ref, kaux_ref, kpos_ref, selb_ref,
                     ocmp_ref, gate_ref, z_ref, o_ref, s_a, s_b, *, tq, tk, seq):
    b = pl.program_id(0)
    g = pl.program_id(1)
    i = pl.program_id(2)
    n_all = seq // tk
    per_tile = tk // SEGMENT
    t0 = pl.multiple_of(i * tq, tq)
    cols = HEADS_PER_GROUP * tq
    gsc = _group_slope_scale(g)
    t_cols = t0 + (lax.broadcasted_iota(jnp.int32, (1, cols), 1) & (tq - 1))
    slopes = [(2.0 ** -(j + 1)) * gsc for j in range(HEADS_PER_GROUP)]

    def values(ref, first, count):
        return jnp.concatenate([ref[0, 0, first + c] for c in range(count)], axis=1)

    lane = lax.broadcasted_iota(jnp.int32, (tq, LANES), 1)
    chosen = selb_ref[...].astype(F32) > -1.0
    own_first = _div_pow2(t0, SEL_BLOCK)
    block_start = (lane * SEL_BLOCK).astype(F32)

    def masked_queries(blocks):
        return _query_aug(q_ref, [jnp.where(lane == 0, sl, jnp.where(chosen & blocks, sl * block_start, NEG))
                                  for sl in slopes])

    q_aug = masked_queries(lane < own_first)

    slot = (b * NSA_KV + g) * (seq // tq) + i
    base = slot * n_all

    def scores(u):
        j0 = pl.multiple_of(tiles_ref[base + u] * tk, tk)
        return _dot_nt(jnp.concatenate([ks_ref[pl.ds(j0, tk), :], kaux_ref[pl.ds(j0, tk), :]], axis=1), q_aug)

    def consume(u, s, carry):
        return _flash_step(s, values(vst_ref, tiles_ref[base + u] * per_tile, per_tile), carry)

    carry = _pipelined_tiles(count_ref[slot], n_all, scores, consume, s_a, s_b, _flash_init(cols))

    j_own = t0 + lax.broadcasted_iota(jnp.int32, (tq, 1), 0)
    s = _dot_nt(jnp.concatenate([ks_ref[pl.ds(t0, tq), :], kaux_ref[pl.ds(t0, tq), :]], axis=1),
                masked_queries(lane >= own_first))
    s = jnp.where(j_own <= t_cols, s, NEG)
    _, l_s, acc_s = _flash_step(s, values(vst_ref, i, 1), carry)
    o_slc = acc_s / l_s

    q_pos = _query_aug(q_ref, [jnp.where(lane < 2, sl, 0.0) for sl in slopes])
    wk = WINDOW + tq
    start = pl.multiple_of(jnp.maximum(t0 - WINDOW, 0), SEGMENT)
    j_win = start + lax.broadcasted_iota(jnp.int32, (wk, 1), 0)
    s = _dot_nt(jnp.concatenate([kw_ref[pl.ds(start, wk), :], kpos_ref[pl.ds(start, wk), :]], axis=1), q_pos)
    s = jnp.where((j_win <= t_cols) & (j_win > t_cols - WINDOW), s, NEG)
    _, l_w, acc_w = _flash_step(s, values(vwt_ref, _div_pow2(start, SEGMENT), wk // SEGMENT), _flash_init(cols))
    o_win = acc_w / l_w

    for j in range(HEADS_PER_GROUP):
        sl = slice(j * HEAD_DIM, (j + 1) * HEAD_DIM)
        cs = slice(j * tq, (j + 1) * tq)
        gates = [_sigmoid(gate_ref[:, 3 * j + c:3 * j + c + 1]) for c in range(3)]
        o = gates[0] * ocmp_ref[:, sl] + gates[1] * o_slc[:, cs].T + gates[2] * o_win[:, cs].T
        o_ref[:, sl] = (o * z_ref[:, sl]).astype(o_ref.dtype)


def _nsa_main(qkn, vals_t, selb, tile_flags, ocmp, gates, zact, batch, seq, tq, tk):
    t = batch * seq
    nq = seq // tq
    n_all = seq // tk
    gw = HEADS_PER_GROUP * HEAD_DIM
    flags = tile_flags.reshape(batch * NSA_KV * nq, n_all)
    tiles = jnp.argsort(1 - flags, axis=-1, stable=True).astype(jnp.int32).reshape(-1)
    counts = jnp.sum(flags, axis=-1).astype(jnp.int32)
    row = lambda b, g, i, *_: (b * nq + i, g)
    ks_blk = (NSA_HEADS * HEAD_DIM) // HEAD_DIM
    kw_blk = ks_blk + NSA_KV
    grid_spec = pltpu.PrefetchScalarGridSpec(
        num_scalar_prefetch=2,
        grid=(batch, NSA_KV, nq),
        in_specs=[pl.BlockSpec((tq, gw), row),
                  pl.BlockSpec((seq, HEAD_DIM), lambda b, g, i, *_: (b, ks_blk + g)),
                  pl.BlockSpec((1, 1, seq // SEGMENT, HEAD_DIM, SEGMENT), lambda b, g, i, *_: (b, g, 0, 0, 0)),
                  pl.BlockSpec((seq, HEAD_DIM), lambda b, g, i, *_: (b, kw_blk + g)),
                  pl.BlockSpec((1, 1, seq // SEGMENT, HEAD_DIM, SEGMENT),
                               lambda b, g, i, *_: (b, NSA_KV + g, 0, 0, 0)),
                  pl.BlockSpec((seq, LANES), lambda b, g, i, *_: (0, 0)),
                  pl.BlockSpec((seq, LANES), lambda b, g, i, *_: (0, 0)),
                  pl.BlockSpec((tq, LANES), row),
                  pl.BlockSpec((tq, gw), row),
                  pl.BlockSpec((tq, LANES), row),
                  pl.BlockSpec((tq, gw), row)],
        out_specs=pl.BlockSpec((tq, gw), row),
        scratch_shapes=[pltpu.VMEM((tk, HEADS_PER_GROUP * tq), F32)] * 2,
    )
    return pl.pallas_call(
        functools.partial(_nsa_main_kernel, tq=tq, tk=tk, seq=seq),
        grid_spec=grid_spec,
        out_shape=jax.ShapeDtypeStruct((t, NSA_HEADS * HEAD_DIM), BF16),
        compiler_params=_params(("parallel", "parallel", "arbitrary")),
        name="nsa_select_window",
    )(tiles, counts, qkn, qkn, vals_t, qkn, vals_t, _key_aux_blocks(seq), _key_aux_position(seq), selb, ocmp, gates,
      zact)


def _threshold_value(code):
    return pltpu.bitcast(jnp.where(code < 0, code ^ 0x7FFFFFFF, code), F32)


def _dsa_index_kernel(qi_ref, ki_ref, wi_ref, tri_ref, bias_ref, score_scr, *, tq, tk, seq, topk):
    i = pl.program_id(1)
    t0 = i * tq
    n_all = seq // tk
    n_kt = _div_pow2(t0 + tq + tk - 1, tk)
    t = t0 + lax.broadcasted_iota(jnp.int32, (1, tq), 1)
    wrow = wi_ref[...] * (IDX_HEADS ** -0.5 * IDX_DIM ** -0.5)
    hb = 4

    def key_ids(kt):
        return kt * tk + lax.broadcasted_iota(jnp.int32, (tk, 1), 0)

    def score_tile(kt):
        j0 = pl.multiple_of(kt * tk, tk)
        ki = ki_ref[pl.ds(j0, tk), :]
        score = jnp.zeros((tk, tq), F32)
        for h0 in range(0, IDX_HEADS, hb):
            x = _dot_nt(ki, qi_ref[0, h0:h0 + hb].reshape(hb * tq, IDX_DIM))
            for h in range(hb):
                score = score + jnp.maximum(x[:, h * tq:(h + 1) * tq], 0.0) * wrow[h0 + h:h0 + h + 1, :]
        score = jnp.where(key_ids(kt) <= t, score, -jnp.inf)
        score_scr[kt] = score
        return score

    def hits(pred, score, c):
        return c + jnp.sum(jnp.where(pred(score), 1, 0).reshape(tk // 8, 8, tq), axis=0)

    def non_negative(s):
        return s >= 0.0

    n_pairs = lax.shift_right_logical(n_kt + 1, 1)

    def score_step(u, c):
        for kt in (2 * u, 2 * u + 1):
            c = hits(non_negative, score_tile(kt), c)
        return c

    c0 = jnp.sum(lax.fori_loop(0, n_pairs, score_step, jnp.zeros((8, tq), jnp.int32)), axis=0, keepdims=True)

    def count(pred):
        def step(u, c):
            for kt in (2 * u, 2 * u + 1):
                c = hits(pred, score_scr[kt], c)
            return c
        c = lax.fori_loop(0, n_pairs, step, jnp.zeros((8, tq), jnp.int32))
        return jnp.sum(c, axis=0, keepdims=True)

    everything = jnp.int32(seq + tk)
    state = (jnp.where(c0 >= topk, 0, INT_MIN), jnp.where(c0 >= topk, c0, everything))

    def bit_step(b, state):
        v, n_ge = state
        cand = v | (jnp.int32(1) << (30 - b))
        threshold = _threshold_value(cand)
        c = count(lambda s: s >= threshold)
        return jnp.where(c >= topk, cand, v), jnp.where(c >= topk, c, n_ge)

    def settled(state):
        done = (state[1] == topk) | (t < topk)
        return jnp.min(jnp.where(done, 1, 0)) > 0

    state = lax.fori_loop(0, RADIX_CHECKS[0], bit_step, state)
    for lo, hi in zip(RADIX_CHECKS, RADIX_CHECKS[1:] + (31,)):
        state = lax.cond(settled(state), lambda s: s, lambda s, lo=lo, hi=hi: lax.fori_loop(lo, hi, bit_step, s), state)
    v, n_ge = state
    threshold = _threshold_value(v)
    take_all = v == INT_MIN

    def emit(kt, keep):
        bias_ref[0, 0, kt] = jnp.where(keep & (key_ids(kt) <= t), 0.0, NEG).astype(bias_ref.dtype)

    def fill_step(kt, _):
        bias_ref[0, 0, kt] = jnp.full((tk, tq), NEG, bias_ref.dtype)
        return 0

    def emit_no_ties(_):
        def step(u, c):
            for kt in (2 * u, 2 * u + 1):
                emit(kt, (score_scr[kt] >= threshold) | take_all)
            return c
        return lax.fori_loop(0, n_pairs, step, 0)

    def emit_with_ties(_):
        need = (topk - count(lambda s: s > threshold)).astype(F32)

        def step(kt, carry):
            score = score_scr[kt]
            eq = score == threshold
            eqf = jnp.where(eq, 1.0, 0.0)
            before = carry + _dot(tri_ref[...], eqf.astype(BF16)) - eqf
            emit(kt, (score > threshold) | (eq & (before < need)) | take_all)
            return carry + jnp.sum(eqf, axis=0, keepdims=True)

        lax.fori_loop(0, n_kt, step, jnp.zeros((1, tq), F32))
        return lax.fori_loop(n_kt, 2 * n_pairs, fill_step, 0)

    over = (n_ge > topk) & jnp.logical_not(take_all)
    lax.cond(jnp.max(jnp.where(over, 1, 0)) > 0, emit_with_ties, emit_no_ties, 0)
    lax.fori_loop(2 * n_pairs, n_all, fill_step, 0)


def _dsa_index(qi_heads, ki, wi_t, batch, seq, tq, tk):
    nq = seq // tq
    n_all = seq // tk
    topk = min(DSA_TOPK_MAX, seq // 4)
    tri = jnp.asarray(np.tril(np.ones((tk, tk), np.float32)), dtype=BF16)
    return pl.pallas_call(
        functools.partial(_dsa_index_kernel, tq=tq, tk=tk, seq=seq, topk=topk),
        grid=(batch, nq),
        in_specs=[pl.BlockSpec((1, IDX_HEADS, tq, IDX_DIM), lambda b, i: (b, 0, i, 0)),
                  pl.BlockSpec((seq, IDX_DIM), lambda b, i: (b, 0)),
                  pl.BlockSpec((IDX_HEADS, tq), lambda b, i: (0, b * nq + i)),
                  pl.BlockSpec((tk, tk), lambda b, i: (0, 0))],
        out_specs=pl.BlockSpec((1, 1, n_all, tk, tq), lambda b, i: (b, i, 0, 0, 0)),
        out_shape=jax.ShapeDtypeStruct((batch, nq, n_all, tk, tq), BF16),
        scratch_shapes=[pltpu.VMEM((n_all, tk, tq), F32)],
        compiler_params=_params(("parallel", "arbitrary")),
        name="dsa_index_topk",
    )(qi_heads, ki, wi_t, tri)


def _dsa_attn_kernel(q_ref, k_ref, vt_ref, kaux_ref, bias_ref, z_ref, o_ref, s_a, s_b, *, tq, tk, seq):
    g = pl.program_id(1)
    i = pl.program_id(2)
    t0 = i * tq
    cols = HEADS_PER_GROUP * tq
    gsc = _group_slope_scale(g)
    slopes = [(2.0 ** -(j + 1)) * gsc for j in range(HEADS_PER_GROUP)]
    lane = lax.broadcasted_iota(jnp.int32, (tq, LANES), 1)
    q_aug = _query_aug(q_ref, [jnp.where(lane < 2, sl, 0.0) for sl in slopes])
    per_tile = tk // SEGMENT

    def scores(kt):
        j0 = pl.multiple_of(kt * tk, tk)
        return _dot_nt(jnp.concatenate([k_ref[pl.ds(j0, tk), :], kaux_ref[pl.ds(j0, tk), :]], axis=1), q_aug)

    def consume(kt, s, carry):
        bias = [bias_ref[0, part, kt].astype(F32) for part in range(bias_ref.shape[1])]
        s = s + jnp.concatenate(bias * HEADS_PER_GROUP, axis=1)
        vt = jnp.concatenate([vt_ref[0, 0, kt * per_tile + c] for c in range(per_tile)], axis=1)
        return _flash_step(s, vt, carry)

    n_kt = _div_pow2(t0 + tq + tk - 1, tk)
    _, l, acc = _pipelined_tiles(n_kt, seq // tk, scores, consume, s_a, s_b, _flash_init(cols))
    o = acc / l
    for j in range(HEADS_PER_GROUP):
        sl = slice(j * HEAD_DIM, (j + 1) * HEAD_DIM)
        o_ref[:, sl] = (o[:, j * tq:(j + 1) * tq].T * z_ref[:, sl]).astype(o_ref.dtype)


def _dsa_attn(qkn, vals_t, bias, zact, batch, seq, tq, tk):
    t = batch * seq
    nq = seq // tq
    gw = HEADS_PER_GROUP * HEAD_DIM
    row = lambda b, g, i: (b * nq + i, g)
    q_blk = (NSA_HEADS + 2 * NSA_KV) * HEAD_DIM // gw
    k_blk = (NSA_HEADS + 2 * NSA_KV + DSA_HEADS)
    v_grp = 2 * NSA_KV
    return pl.pallas_call(
        functools.partial(_dsa_attn_kernel, tq=tq, tk=tk, seq=seq),
        grid=(batch, DSA_KV, nq),
        in_specs=[pl.BlockSpec((tq, gw), lambda b, g, i: (b * nq + i, q_blk + g)),
                  pl.BlockSpec((seq, HEAD_DIM), lambda b, g, i: (b, k_blk + g)),
                  pl.BlockSpec((1, 1, seq // SEGMENT, HEAD_DIM, SEGMENT), lambda b, g, i: (b, v_grp + g, 0, 0, 0)),
                  pl.BlockSpec((seq, LANES), lambda b, g, i: (0, 0)),
                  pl.BlockSpec((1, tq // bias.shape[4]) + bias.shape[2:], lambda b, g, i: (b, i, 0, 0, 0)),
                  pl.BlockSpec((tq, gw), lambda b, g, i: (b * nq + i, NSA_KV + g))],
        out_specs=pl.BlockSpec((tq, gw), row),
        out_shape=jax.ShapeDtypeStruct((t, DSA_HEADS * HEAD_DIM), BF16),
        scratch_shapes=[pltpu.VMEM((tk, HEADS_PER_GROUP * tq), F32)] * 2,
        compiler_params=_params(("parallel", "parallel", "arbitrary")),
        name="dsa_attention",
    )(qkn, qkn, vals_t, _key_aux_position(seq), bias, zact)


def _merge_ple_kernel(on_ref, od_ref, sn_ref, sd_ref, x_ref, p_ref, g_ref, wun_ref, wud_ref, wout_ref, wg_ref,
                      wp_ref, o_ref):
    y = sn_ref[...] * _dot(on_ref[...], wun_ref[...]) + sd_ref[...] * _dot(od_ref[...], wud_ref[...])
    x = x_ref[...] + _dot(y.astype(BF16), wout_ref[...])
    ms = jnp.mean(x * x, axis=-1, keepdims=True)
    r = (x * lax.rsqrt(ms + EPS) * g_ref[...]).astype(BF16)
    gate = _sigmoid(_dot(r, wg_ref[...]))
    o_ref[...] = x + _dot(p_ref[...].astype(BF16), wp_ref[...]) * gate


def _merge_ple(o_nsa, o_dsa, gate_act, x, p, g, w_up_nsa, w_up_dsa, w_out, w_gate, w_proj, tm):
    t, d = x.shape
    resident = lambda shape: pl.BlockSpec(shape, lambda i: (0, 0), pipeline_mode=pl.Buffered(1))
    return pl.pallas_call(
        _merge_ple_kernel,
        grid=(t // tm,),
        in_specs=[pl.BlockSpec((tm, o_nsa.shape[1]), lambda i: (i, 0)),
                  pl.BlockSpec((tm, o_dsa.shape[1]), lambda i: (i, 0)),
                  pl.BlockSpec((tm, d), lambda i: (i, 0)),
                  pl.BlockSpec((tm, d), lambda i: (i, 1)),
                  pl.BlockSpec((tm, d), lambda i: (i, 0)),
                  pl.BlockSpec((tm, p.shape[1]), lambda i: (i, 0)),
                  resident((1, d)), resident(w_up_nsa.shape), resident(w_up_dsa.shape), resident(w_out.shape),
                  resident(w_gate.shape), resident(w_proj.shape)],
        out_specs=pl.BlockSpec((tm, d), lambda i: (i, 0)),
        out_shape=jax.ShapeDtypeStruct((t, d), F32),
        compiler_params=_params(("parallel",)),
        name="merge_ple",
    )(o_nsa, o_dsa, gate_act, gate_act, x, p, g.reshape(1, d), w_up_nsa, w_up_dsa, w_out, w_gate, w_proj)


def _split_in_proj(w):
    widths = [NSA_HEADS * HEAD_DIM] + [NSA_KV * HEAD_DIM] * 6 + [NSA_HEADS * 3, NSA_HEADS * HEAD_DIM,
              DSA_HEADS * HEAD_DIM, DSA_KV * HEAD_DIM, DSA_KV * HEAD_DIM, IDX_HEADS * IDX_DIM, IDX_DIM,
              IDX_HEADS, DSA_HEADS * HEAD_DIM, D_MODEL, D_MODEL]
    offs = np.concatenate([[0], np.cumsum(widths)])
    assert offs[-1] == w.shape[1]
    (nq, nkc, nvc, nks, nvs, nkw, nvw, ng, nz, dq, dk, dv, iq, ik, iw, dz, mgn, mgd) = [
        w[:, offs[k]:offs[k + 1]] for k in range(len(widths))]
    pad = jnp.zeros((w.shape[0], LANES - IDX_DIM - IDX_HEADS - NSA_HEADS * 3), w.dtype)
    return dict(
        normed=jnp.concatenate([nq, nks, nkw, dq, dk], axis=1).astype(BF16),
        vals=jnp.concatenate([nvs, nvw, dv], axis=1).astype(BF16),
        iq=iq.astype(BF16),
        raw=jnp.concatenate([nkc, nvc, ik, iw, ng, pad], axis=1).astype(BF16),
        zact=jnp.concatenate([nz, dz], axis=1).astype(BF16),
        merge=jnp.concatenate([mgn, mgd], axis=1).astype(BF16),
    )


def _layer(x, p, norm_g, w_in, nsa_q_g, nsa_kc_g, nsa_ks_g, nsa_kw_g, cmp_pe_k, cmp_w1_k, cmp_w2_k,
           cmp_pe_v, cmp_w1_v, cmp_w2_v, dsa_q_g, dsa_k_g, w_up_nsa, w_up_dsa, w_out, ple_norm_g,
           w_ple_gate, w_ple_proj):
    batch, seq, d = x.shape
    t = batch * seq
    assert seq % SEGMENT == 0 and seq >= WINDOW + SEGMENT
    x2 = x.reshape(t, d)
    tiles = _Tiles.for_shape(batch, seq)
    tm = tiles.proj_rows

    w = _split_in_proj(w_in)
    h = _rmsnorm(x2, norm_g, tiles.norm_rows)

    gains = jnp.concatenate([jnp.tile(nsa_q_g * ATTN_SCALE, NSA_HEADS), jnp.tile(nsa_ks_g, NSA_KV),
                             jnp.tile(nsa_kw_g, NSA_KV), jnp.tile(dsa_q_g * ATTN_SCALE, DSA_HEADS),
                             jnp.tile(dsa_k_g, DSA_KV)])
    qkn = _proj(h, w["normed"], tm, w["normed"].shape[1] // 2, BF16, gain=gains)
    vals = _proj(h, w["vals"], tm, w["vals"].shape[1], BF16)
    vals_t = _transposed_values(vals, batch, seq, 2 * NSA_KV + DSA_KV)
    qi_heads = _proj_heads(h, w["iq"], min(tm, seq), batch, seq)
    kc_raw, vc_raw, misc = _proj_split(h, w["raw"], tm, (NSA_KV * HEAD_DIM, NSA_KV * HEAD_DIM, LANES))
    zact = _proj(h, w["zact"], tm, tiles.proj_cols, F32, act="silu")
    gate_act = _proj(h, w["merge"], tm, tiles.proj_cols, F32, act="sigmoid")

    ki = misc[:, :IDX_DIM].astype(BF16)
    wi = misc[:, IDX_DIM:IDX_DIM + IDX_HEADS]
    ng = misc[:, IDX_DIM + IDX_HEADS:IDX_DIM + IDX_HEADS + NSA_HEADS * 3]
    per_group = HEADS_PER_GROUP * 3
    gates = jnp.pad(ng.reshape(t, NSA_KV, per_group), ((0, 0), (0, 0), (0, LANES - per_group))).reshape(t, NSA_KV * LANES)

    kc = _compress(kc_raw, cmp_pe_k, cmp_w1_k, cmp_w2_k, nsa_kc_g, batch, seq)
    vc = _compress(vc_raw, cmp_pe_v, cmp_w1_v, cmp_w2_v, None, batch, seq)

    tk = tiles.keys
    ocmp, selb, tile_flags = _nsa_cmp(qkn, kc, vc, batch, seq, tiles.cmp_queries, tk)
    o_nsa = _nsa_main(qkn, vals_t, selb, tile_flags, ocmp, gates, zact, batch, seq, SEGMENT, tk)

    bias = _dsa_index(qi_heads, ki, wi.T, batch, seq, INDEX_QUERIES, tk)
    o_dsa = _dsa_attn(qkn, vals_t, bias, zact, batch, seq, SEGMENT, tk)

    x3 = _merge_ple(o_nsa, o_dsa, gate_act, x2, p.reshape(t, PLE_DIM), ple_norm_g, w_up_nsa.astype(BF16),
                    w_up_dsa.astype(BF16), w_out.astype(BF16), w_ple_gate.astype(BF16), w_ple_proj.astype(BF16),
                    tiles.out_rows)
    return x3.reshape(batch, seq, d)


def kernel(x, p, norm_g, w_in, nsa_q_g, nsa_kc_g, nsa_ks_g, nsa_kw_g, cmp_pe_k, cmp_w1_k, cmp_w2_k, cmp_pe_v, cmp_w1_v, cmp_w2_v, dsa_q_g, dsa_k_g, w_up_nsa, w_up_dsa, w_out, ple_norm_g, w_ple_gate, w_ple_proj):
    depth = w_in.shape[0]
    for i in range(depth):
        x = _layer(x, p[i], norm_g[i], w_in[i], nsa_q_g[i], nsa_kc_g[i], nsa_ks_g[i], nsa_kw_g[i],
                   cmp_pe_k[i], cmp_w1_k[i], cmp_w2_k[i], cmp_pe_v[i], cmp_w1_v[i], cmp_w2_v[i],
                   dsa_q_g[i], dsa_k_g[i], w_up_nsa[i], w_up_dsa[i], w_out[i], ple_norm_g[i],
                   w_ple_gate[i], w_ple_proj[i])
    return x
```
